```python
import jax
import jax.numpy as jnp
from jax import lax
import numpy as np

D_MODEL = 1024
BATCH = 4
SEQ = 8192
DEPTH = 1

RET_HEADS = 4
RET_DK = 256
RET_DV = 512
RET_QK = RET_HEADS * RET_DK
RET_V = RET_HEADS * RET_DV
RET_CHUNK = 128
ROPE_BASE = 10000.0

LRU_WIDTH = 1536
LRU_BLOCKS = 8
LRU_BLOCK = LRU_WIDTH // LRU_BLOCKS
CONV_WIDTH = 4
LRU_C = 8.0

N_GROUPS = 4
EXPERTS_PER_GROUP = 8
N_EXPERTS = N_GROUPS * EXPERTS_PER_GROUP
TOP_K = 2
D_EXPERT = 512
MOE_ROWS = 128

IN_WIDTHS = (RET_QK, RET_QK, RET_V, RET_V, LRU_WIDTH, LRU_WIDTH, 2 * D_MODEL)

LN_EPS = 1e-5
DEEPNORM_ALPHA = (2.0 * DEPTH) ** 0.25
DEEPNORM_BETA = (8.0 * DEPTH) ** -0.25

kernel_name = 'hybrid_retention_rglru_hmoe_deepnorm'


def _layer_norm(x, g, b):
    xf = x.astype(jnp.float32)
    mu = xf.mean(-1, keepdims=True)
    var = jnp.square(xf - mu).mean(-1, keepdims=True)
    return ((xf - mu) * lax.rsqrt(var + LN_EPS) * g + b).astype(x.dtype)


def _head_norm(t):
    mu = t.mean(-1, keepdims=True)
    var = jnp.square(t - mu).mean(-1, keepdims=True)
    return (t - mu) * lax.rsqrt(var + LN_EPS)


def _rotary(t):
    S = t.shape[1]
    half = t.shape[-1] // 2
    inv = ROPE_BASE ** (-jnp.arange(half, dtype=jnp.float32) / half)
    ang = jnp.arange(S, dtype=jnp.float32)[:, None] * inv[None, :]
    cos = jnp.cos(ang)[None, :, None, :]
    sin = jnp.sin(ang)[None, :, None, :]
    t1, t2 = t[..., :half], t[..., half:]
    return jnp.concatenate([t1 * cos - t2 * sin, t1 * sin + t2 * cos], axis=-1)


def _retention(q, k, v):
    B, S, H, DK = q.shape
    DV = v.shape[-1]
    C = RET_CHUNK
    NC = S // C
    log_g = jnp.log1p(-(2.0 ** (-5.0 - jnp.arange(H, dtype=jnp.float32))))
    pos = jnp.arange(C, dtype=jnp.float32)
    diff = pos[:, None] - pos[None, :]
    causal = diff >= 0
    d_mask = jnp.where(causal[None], jnp.exp(log_g[:, None, None] * jnp.where(causal, diff, 0.0)[None]), 0.0)
    xi = jnp.exp(log_g[:, None] * (pos + 1.0)[None])
    zeta = jnp.exp(log_g[:, None] * (C - 1.0 - pos)[None])
    chunk_decay = jnp.exp(log_g * C)
    k = k * (DK ** -0.5)

    def to_chunks(t):
        return t.reshape(B, NC, C, H, t.shape[-1]).transpose(1, 0, 3, 2, 4)

    qc, kc, vc = to_chunks(q), to_chunks(k), to_chunks(v)

    def step(state, inp):
        qi, ki, vi = inp
        scores = jnp.einsum('bhnd,bhmd->bhnm', qi, ki) * d_mask
        inner = jnp.einsum('bhnm,bhme->bhne', scores, vi)
        cross = jnp.einsum('bhnd,bhde->bhne', qi, state) * xi[None, :, :, None]
        new_state = state * chunk_decay[None, :, None, None] + jnp.einsum(
            'bhmd,bhme->bhde', ki * zeta[None, :, :, None], vi)
        return new_state, inner + cross

    state0 = jnp.zeros((B, H, DK, DV), jnp.float32)
    _, out = lax.scan(step, state0, (qc, kc, vc))
    return out.transpose(1, 0, 3, 2, 4).reshape(B, S, H, DV)


def _causal_conv(u, w, b):
    out = lax.conv_general_dilated(
        u, w[:, None, :].astype(u.dtype), window_strides=(1,), padding=[(CONV_WIDTH - 1, 0)],
        dimension_numbers=('NWC', 'WIO', 'NWC'), feature_group_count=u.shape[-1])
    return out + b


def _rg_lru(u, w_r, b_r, w_i, b_i, lam):
    B, S, W = u.shape
    ub = u.reshape(B, S, LRU_BLOCKS, LRU_BLOCK)
    r = jax.nn.sigmoid(jnp.einsum('bsnc,ncd->bsnd', ub, w_r).reshape(B, S, W) + b_r)
    i = jax.nn.sigmoid(jnp.einsum('bsnc,ncd->bsnd', ub, w_i).reshape(B, S, W) + b_i)
    log_a = -LRU_C * r * jax.nn.softplus(-lam.astype(jnp.float32))
    a = jnp.exp(log_a)
    inp = jnp.sqrt(-jnp.expm1(2.0 * log_a)) * (i * u)

    def combine(left, right):
        a1, b1 = left
        a2, b2 = right
        return a1 * a2, a2 * b1 + b2

    _, h = lax.associative_scan(combine, (a, inp), axis=1)
    return h


def _hier_moe(x, w_group, b_group, w_exp_router, b_exp_router, w_e_gate, w_e_up, w_e_down):
    B, S, D = x.shape
    N = B * S
    xt = x.reshape(N, D)
    rows = jnp.arange(N)
    g_logits = (xt @ w_group + b_group).astype(jnp.float32)
    g_prob = jax.nn.softmax(g_logits, axis=-1)
    g_idx = jnp.argmax(g_logits, axis=-1)
    g_w = g_prob[rows, g_idx]
    e_logits = (xt @ w_exp_router + b_exp_router).astype(jnp.float32)
    e_logits = e_logits.reshape(N, N_GROUPS, EXPERTS_PER_GROUP)[rows, g_idx]
    top_v, top_i = lax.top_k(e_logits, TOP_K)
    weights = g_w[:, None] * jax.nn.softmax(top_v, axis=-1)
    expert = g_idx[:, None] * EXPERTS_PER_GROUP + top_i

    M = N * TOP_K
    flat_e = expert.reshape(M)
    flat_w = weights.reshape(M)
    flat_tok = jnp.repeat(rows, TOP_K)
    order = jnp.argsort(flat_e)
    se = flat_e[order]
    counts = jnp.zeros((N_EXPERTS,), jnp.int32).at[flat_e].add(1)
    padded = (counts + MOE_ROWS - 1) // MOE_ROWS * MOE_ROWS
    starts = jnp.cumsum(counts) - counts
    pad_ends = jnp.cumsum(padded)
    pad_starts = pad_ends - padded
    dest = pad_starts[se] + jnp.arange(M) - starts[se]
    n_groups_rows = -(-(M + N_EXPERTS * (MOE_ROWS - 1)) // MOE_ROWS)
    m_pad = n_groups_rows * MOE_ROWS
    row_tok = jnp.full((m_pad,), N, jnp.int32).at[dest].set(flat_tok[order])
    row_w = jnp.zeros((m_pad,), jnp.float32).at[dest].set(flat_w[order])
    group_e = jnp.minimum(
        jnp.searchsorted(pad_ends, jnp.arange(n_groups_rows) * MOE_ROWS, side='right'), N_EXPERTS - 1)
    x_rows = jnp.concatenate([xt, jnp.zeros((1, D), xt.dtype)], axis=0)[row_tok]
    x_rows = x_rows.reshape(n_groups_rows, MOE_ROWS, D)

    def expert_rows(args):
        xb, e = args
        h = jax.nn.silu(xb @ w_e_gate[e]) * (xb @ w_e_up[e])
        return h @ w_e_down[e]

    y_rows = lax.map(expert_rows, (x_rows, group_e)).reshape(m_pad, D)
    y = jax.ops.segment_sum(y_rows * row_w[:, None].astype(y_rows.dtype), row_tok, num_segments=N + 1)[:N]
    return y.reshape(B, S, D)


def _layer(x, w_in, b_merge, conv_w, conv_b, w_rg_r, b_rg_r, w_rg_i, b_rg_i, lru_lambda,
           w_ret_o, w_lru_o, w_out, ln1_g, ln1_b, w_group, b_group, w_exp_router, b_exp_router,
           w_e_gate, w_e_up, w_e_down, ln2_g, ln2_b):
    B, S, D = x.shape
    f32 = jnp.float32
    splits = [int(s) for s in np.cumsum(IN_WIDTHS)[:-1]]
    q, k, v, g, u_lru, gate_lru, gate_logits = jnp.split(x @ w_in, splits, axis=-1)

    q = _rotary(q.astype(f32).reshape(B, S, RET_HEADS, RET_DK))
    k = _rotary(k.astype(f32).reshape(B, S, RET_HEADS, RET_DK))
    v = v.astype(f32).reshape(B, S, RET_HEADS, RET_DV)
    ret = _head_norm(_retention(q, k, v)).reshape(B, S, RET_V)
    ret = (jax.nn.silu(g.astype(f32)) * ret).astype(x.dtype)

    u = _causal_conv(u_lru, conv_w, conv_b)
    h = _rg_lru(u.astype(f32), w_rg_r, b_rg_r, w_rg_i, b_rg_i, lru_lambda)
    lru = (jax.nn.gelu(gate_lru.astype(f32)) * h).astype(x.dtype)

    gates = jax.nn.sigmoid((gate_logits + b_merge).astype(f32)).astype(x.dtype)
    gate_ret, gate_rec = jnp.split(gates, 2, axis=-1)
    merged = gate_ret * (ret @ w_ret_o) + gate_rec * (lru @ w_lru_o)
    x = _layer_norm(DEEPNORM_ALPHA * x + merged @ w_out, ln1_g, ln1_b)

    moe = _hier_moe(x, w_group, b_group, w_exp_router, b_exp_router, w_e_gate, w_e_up, w_e_down)
    return _layer_norm(DEEPNORM_ALPHA * x + moe, ln2_g, ln2_b)


def setup_inputs(seed: int = 0) -> dict:
    key = jax.random.key(seed)
    ks = jax.random.split(key, 32)
    f32 = jnp.float32
    D = D_MODEL
    L = DEPTH
    sd = D ** -0.5
    beta = DEEPNORM_BETA

    def nrm(k, shape, scale):
        return jax.random.normal(k, shape, f32) * scale

    x = nrm(ks[0], (BATCH, SEQ, D), 1.0)
    w_in = jnp.concatenate([
        nrm(ks[1], (L, D, RET_QK), sd),
        nrm(ks[2], (L, D, RET_QK), sd),
        nrm(ks[3], (L, D, RET_V), sd * beta),
        nrm(ks[4], (L, D, RET_V), sd),
        nrm(ks[5], (L, D, LRU_WIDTH), sd),
        nrm(ks[6], (L, D, LRU_WIDTH), sd),
        nrm(ks[7], (L, D, 2 * D), sd)], axis=-1)
    b_merge = nrm(ks[8], (L, 2 * D), 0.01)
    conv_w = nrm(ks[9], (L, CONV_WIDTH, LRU_WIDTH), CONV_WIDTH ** -0.5)
    conv_b = nrm(ks[10], (L, LRU_WIDTH), 0.01)
    w_rg_r = nrm(ks[11], (L, LRU_BLOCKS, LRU_BLOCK, LRU_BLOCK), LRU_BLOCK ** -0.5)
    b_rg_r = nrm(ks[12], (L, LRU_WIDTH), 0.01)
    w_rg_i = nrm(ks[13], (L, LRU_BLOCKS, LRU_BLOCK, LRU_BLOCK), LRU_BLOCK ** -0.5)
    b_rg_i = nrm(ks[14], (L, LRU_WIDTH), 0.01)
    a_c = jax.random.uniform(ks[15], (L, LRU_WIDTH), f32, 0.9, 0.999)
    a0 = a_c ** (1.0 / LRU_C)
    lru_lambda = jnp.log(a0) - jnp.log1p(-a0)
    w_ret_o = nrm(ks[16], (L, RET_V, D), beta * RET_V ** -0.5)
    w_lru_o = nrm(ks[17], (L, LRU_WIDTH, D), beta * LRU_WIDTH ** -0.5)
    w_out = nrm(ks[18], (L, D, D), beta * sd)
    ln1_g = 1.0 + nrm(ks[19], (L, D), 0.01)
    ln1_b = nrm(ks[20], (L, D), 0.01)
    w_group = nrm(ks[21], (L, D, N_GROUPS), sd)
    b_group = nrm(ks[22], (L, N_GROUPS), 0.01)
    w_exp_router = nrm(ks[23], (L, D, N_EXPERTS), sd)
    b_exp_router = nrm(ks[24], (L, N_EXPERTS), 0.01)
    w_e_gate = nrm(ks[25], (L, N_EXPERTS, D, D_EXPERT), sd)
    w_e_up = nrm(ks[26], (L, N_EXPERTS, D, D_EXPERT), beta * sd)
    w_e_down = nrm(ks[27], (L, N_EXPERTS, D_EXPERT, D), beta * D_EXPERT ** -0.5)
    ln2_g = 1.0 + nrm(ks[28], (L, D), 0.01)
    ln2_b = nrm(ks[29], (L, D), 0.01)
    return {'x': x, 'w_in': w_in, 'b_merge': b_merge, 'conv_w': conv_w, 'conv_b': conv_b,
            'w_rg_r': w_rg_r, 'b_rg_r': b_rg_r, 'w_rg_i': w_rg_i, 'b_rg_i': b_rg_i,
            'lru_lambda': lru_lambda, 'w_ret_o': w_ret_o, 'w_lru_o': w_lru_o, 'w_out': w_out,
            'ln1_g': ln1_g, 'ln1_b': ln1_b, 'w_group': w_group, 'b_group': b_group,
            'w_exp_router': w_exp_router, 'b_exp_router': b_exp_router, 'w_e_gate': w_e_gate,
            'w_e_up': w_e_up, 'w_e_down': w_e_down, 'ln2_g': ln2_g, 'ln2_b': ln2_b}


def reference(x, w_in, b_merge, conv_w, conv_b, w_rg_r, b_rg_r, w_rg_i, b_rg_i, lru_lambda,
              w_ret_o, w_lru_o, w_out, ln1_g, ln1_b, w_group, b_group, w_exp_router, b_exp_router,
              w_e_gate, w_e_up, w_e_down, ln2_g, ln2_b):
    for l in range(DEPTH):
        x = _layer(x, w_in[l], b_merge[l], conv_w[l], conv_b[l], w_rg_r[l], b_rg_r[l], w_rg_i[l],
                   b_rg_i[l], lru_lambda[l], w_ret_o[l], w_lru_o[l], w_out[l], ln1_g[l], ln1_b[l],
                   w_group[l], b_group[l], w_exp_router[l], b_exp_router[l], w_e_gate[l], w_e_up[l],
                   w_e_down[l], ln2_g[l], ln2_b[l])
    return x
```

```python
import functools

import jax
import jax.numpy as jnp
import numpy as np
from jax import lax
from jax.experimental import pallas as pl
from jax.experimental.pallas import tpu as pltpu

F32 = jnp.float32
BF16 = jnp.bfloat16
I32 = jnp.int32

D_MODEL = 1024
RET_HEADS = 4
RET_DK = 256
RET_DV = 512
RET_QK = RET_HEADS * RET_DK
RET_V = RET_HEADS * RET_DV
ROPE_BASE = 10000.0
LRU_WIDTH = 1536
LRU_BLOCKS = 8
LRU_BLOCK = LRU_WIDTH // LRU_BLOCKS
LRU_PAIR = 2 * LRU_BLOCK
LRU_PAIRS = LRU_BLOCKS // 2
CONV_WIDTH = 4
LRU_C = 8.0
N_GROUPS = 4
EXPERTS_PER_GROUP = 8
N_EXPERTS = N_GROUPS * EXPERTS_PER_GROUP
TOP_K = 2
D_EXPERT = 512
LN_EPS = 1e-5

LANES = 128
SUBLANES = 8
VMEM_LIMIT = 56 * 1024 * 1024

PROJ_TM = 1024
PROJ_TN = 512
RET_CHUNK = 128
LRU_T = 256
MERGE_TM = 256
MOE_RT = 256
COMB_TM = 256
DISP_BATCH = 256


def _cparams(sem):
    return pltpu.CompilerParams(dimension_semantics=sem, vmem_limit_bytes=VMEM_LIMIT)


def _proj_body(mode, n_q_tiles, *refs):
    if mode == "rot":
        x_ref, w_ref, cos_ref, sin_ref, o_ref, xb_ref = refs
    elif mode == "sigb":
        x_ref, w_ref, b_ref, o_ref, xb_ref = refs
    else:
        x_ref, w_ref, o_ref, xb_ref = refs
    j = pl.program_id(1)

    @pl.when(j == 0)
    def _():
        xb_ref[...] = x_ref[...].astype(BF16)

    acc = jnp.dot(xb_ref[...], w_ref[...], preferred_element_type=F32)
    if mode == "rot":
        half = RET_DK // 2
        t1, t2 = acc[:, :half], acc[:, half:]
        c, s = cos_ref[...], sin_ref[...]
        scale = jnp.where(j >= n_q_tiles, RET_DK ** -0.5, 1.0).astype(F32)
        o_ref[:, :half] = ((t1 * c - t2 * s) * scale).astype(o_ref.dtype)
        o_ref[:, half:] = ((t1 * s + t2 * c) * scale).astype(o_ref.dtype)
    elif mode == "silu":
        o_ref[...] = (acc * jax.nn.sigmoid(acc)).astype(o_ref.dtype)
    elif mode == "gelu":
        o_ref[...] = jax.nn.gelu(acc).astype(o_ref.dtype)
    elif mode == "sigb":
        o_ref[...] = jax.nn.sigmoid(acc + b_ref[...]).astype(o_ref.dtype)
    else:
        o_ref[...] = acc.astype(o_ref.dtype)


def _proj(x2, w, mode, seq, extra=(), out_dtype=BF16):
    n, d = x2.shape
    width = w.shape[1]
    tm = min(PROJ_TM, seq)
    tn = RET_DK if mode == "rot" else min(PROJ_TN, width)
    grid = (n // tm, width // tn)
    in_specs = [pl.BlockSpec((tm, d), lambda i, j: (i, 0)),
                pl.BlockSpec((d, tn), lambda i, j: (0, j))]
    if mode == "rot":
        per_seq = seq // tm
        spec = pl.BlockSpec((tm, RET_DK // 2), lambda i, j: (i % per_seq, 0))
        in_specs += [spec, spec]
    elif mode == "sigb":
        in_specs += [pl.BlockSpec((1, tn), lambda i, j: (0, j))]
    return pl.pallas_call(
        functools.partial(_proj_body, mode, RET_HEADS),
        grid=grid,
        in_specs=in_specs,
        out_specs=pl.BlockSpec((tm, tn), lambda i, j: (i, j)),
        out_shape=jax.ShapeDtypeStruct((n, width), out_dtype),
        scratch_shapes=[pltpu.VMEM((tm, d), BF16)],
        compiler_params=_cparams(("parallel", "arbitrary")),
        name="proj_" + mode,
    )(x2, w, *extra)


def _ret_body(dec_ref, q_ref, k_ref, v_ref, sg_ref, dm_ref, xi_ref, zeta_ref, o_ref, st_ref):
    h = pl.program_id(1)
    c = pl.program_id(2)

    @pl.when(c == 0)
    def _():
        st_ref[...] = jnp.zeros_like(st_ref)

    q, k, v = q_ref[...], k_ref[...], v_ref[...]
    scores = lax.dot_general(q, k, (((1,), (1,)), ((), ())), preferred_element_type=F32) * dm_ref[...]
    inner = jnp.dot(scores.astype(BF16), v, preferred_element_type=F32)
    st = st_ref[...]
    cross = jnp.dot(q, st.astype(BF16), preferred_element_type=F32) * xi_ref[...]
    kz = (k.astype(F32) * zeta_ref[...]).astype(BF16)
    upd = lax.dot_general(kz, v, (((0,), (0,)), ((), ())), preferred_element_type=F32)
    st_ref[...] = st * dec_ref[h] + upd
    o = inner + cross
    mu = jnp.mean(o, axis=-1, keepdims=True)
    oc = o - mu
    var = jnp.mean(oc * oc, axis=-1, keepdims=True)
    on = oc * lax.rsqrt(var + LN_EPS)
    o_ref[...] = (sg_ref[...].astype(F32) * on).astype(o_ref.dtype)


def _retention(qk, v, sg, batch, seq):
    n = qk.shape[0]
    C = min(RET_CHUNK, seq)
    nc = seq // C
    H = RET_HEADS
    log_g = jnp.log1p(-(2.0 ** (-5.0 - jnp.arange(H, dtype=F32))))
    pos = jnp.arange(C, dtype=F32)
    diff = pos[:, None] - pos[None, :]
    causal = diff >= 0
    d_mask = jnp.where(causal[None], jnp.exp(log_g[:, None, None] * jnp.where(causal, diff, 0.0)[None]), 0.0)
    xi = jnp.exp(log_g[:, None] * (pos + 1.0)[None])[:, :, None]
    zeta = jnp.exp(log_g[:, None] * (C - 1.0 - pos)[None])[:, :, None]
    chunk_decay = jnp.exp(log_g * C)
    row = lambda b, h, c: b * nc + c
    return pl.pallas_call(
        _ret_body,
        grid=(batch, H, nc),
        in_specs=[
            pl.BlockSpec(memory_space=pltpu.SMEM),
            pl.BlockSpec((C, RET_DK), lambda b, h, c: (row(b, h, c), h)),
            pl.BlockSpec((C, RET_DK), lambda b, h, c: (row(b, h, c), H + h)),
            pl.BlockSpec((C, RET_DV), lambda b, h, c: (row(b, h, c), h)),
            pl.BlockSpec((C, RET_DV), lambda b, h, c: (row(b, h, c), h)),
            pl.BlockSpec((None, C, C), lambda b, h, c: (h, 0, 0)),
            pl.BlockSpec((None, C, 1), lambda b, h, c: (h, 0, 0)),
            pl.BlockSpec((None, C, 1), lambda b, h, c: (h, 0, 0)),
        ],
        out_specs=pl.BlockSpec((C, RET_DV), lambda b, h, c: (row(b, h, c), h)),
        out_shape=jax.ShapeDtypeStruct((n, RET_V), BF16),
        scratch_shapes=[pltpu.VMEM((RET_DK, RET_DV), F32)],
        compiler_params=_cparams(("parallel", "parallel", "arbitrary")),
        name="retention",
    )(chunk_decay, qk, qk, v, sg, d_mask, xi, zeta)


def _lru_body(u_ref, g_ref, cw_ref, cb_ref, wr_ref, wi_ref, br_ref, bi_ref, lam_ref,
              o_ref, ubuf_ref, a_ref, b_ref, carry_ref):
    T = u_ref.shape[0]
    G = T // SUBLANES
    ncb = LRU_WIDTH // LANES
    t = pl.program_id(1)

    @pl.when(t == 0)
    def _():
        ubuf_ref[0:SUBLANES, :] = jnp.zeros((SUBLANES, LRU_WIDTH), F32)
        carry_ref[...] = jnp.zeros_like(carry_ref)

    ubuf_ref[SUBLANES:SUBLANES + T, :] = u_ref[...].astype(F32)
    cw = cw_ref[...]
    uc = cb_ref[...] + cw[CONV_WIDTH - 1:CONV_WIDTH, :] * ubuf_ref[SUBLANES:SUBLANES + T, :]
    for j in range(CONV_WIDTH - 1):
        back = CONV_WIDTH - 1 - j
        uc = uc + cw[j:j + 1, :] * ubuf_ref[SUBLANES - back:SUBLANES - back + T, :]
    ubuf_ref[0:SUBLANES, :] = ubuf_ref[T:T + SUBLANES, :]

    ucb = uc.astype(BF16)
    neg_c_sp = -LRU_C * jax.nn.softplus(-lam_ref[...])
    for p in range(LRU_PAIRS):
        lo = p * LRU_PAIR
        sl = ucb[:, lo:lo + LRU_PAIR]
        r = jax.nn.sigmoid(jnp.dot(sl, wr_ref[p], preferred_element_type=F32) + br_ref[:, lo:lo + LRU_PAIR])
        i = jax.nn.sigmoid(jnp.dot(sl, wi_ref[p], preferred_element_type=F32) + bi_ref[:, lo:lo + LRU_PAIR])
        log_a = r * neg_c_sp[:, lo:lo + LRU_PAIR]
        a = jnp.exp(log_a)
        inp = jnp.sqrt(-jnp.tanh(log_a) * (a * a + 1.0)) * (i * uc[:, lo:lo + LRU_PAIR])
        for cc in range(LRU_PAIR // LANES):
            cb = p * (LRU_PAIR // LANES) + cc
            a_ref[cb] = a[:, cc * LANES:(cc + 1) * LANES]
            b_ref[cb] = inp[:, cc * LANES:(cc + 1) * LANES]

    def step(j, hp):
        hs, ps = hp
        nh, npr = [], []
        for cb in range(ncb):
            a = a_ref[cb, pl.ds(j, SUBLANES, stride=G), :]
            b = b_ref[cb, pl.ds(j, SUBLANES, stride=G), :]
            hn = a * hs[cb] + b
            pn = a * ps[cb]
            b_ref[cb, pl.ds(j, SUBLANES, stride=G), :] = hn
            a_ref[cb, pl.ds(j, SUBLANES, stride=G), :] = pn
            nh.append(hn)
            npr.append(pn)
        return tuple(nh), tuple(npr)

    zeros = tuple(jnp.zeros((SUBLANES, LANES), F32) for _ in range(ncb))
    ones = tuple(jnp.ones((SUBLANES, LANES), F32) for _ in range(ncb))
    h_end, p_end = lax.fori_loop(0, G, step, (zeros, ones))

    for cb in range(ncb):
        cin = carry_ref[:, cb * LANES:(cb + 1) * LANES]
        for s in range(SUBLANES):
            rows = slice(s * G, (s + 1) * G)
            hseg = b_ref[cb, rows, :] + a_ref[cb, rows, :] * cin
            gate = g_ref[rows, cb * LANES:(cb + 1) * LANES].astype(F32)
            o_ref[rows, cb * LANES:(cb + 1) * LANES] = (gate * hseg).astype(o_ref.dtype)
            cin = h_end[cb][s:s + 1, :] + p_end[cb][s:s + 1, :] * cin
        carry_ref[:, cb * LANES:(cb + 1) * LANES] = cin


def _rglru(u, gate, conv_w, conv_b, w_r, b_r, w_i, b_i, lam, batch, seq):
    n = u.shape[0]
    T = min(LRU_T, seq)
    nt = seq // T
    ncb = LRU_WIDTH // LANES

    def pairs(w):
        z = jnp.zeros((LRU_PAIRS, LRU_PAIR, LRU_PAIR), F32)
        w4 = w.reshape(LRU_PAIRS, 2, LRU_BLOCK, LRU_BLOCK)
        z = z.at[:, :LRU_BLOCK, :LRU_BLOCK].set(w4[:, 0])
        z = z.at[:, LRU_BLOCK:, LRU_BLOCK:].set(w4[:, 1])
        return z.astype(BF16)

    row = lambda b, t: (b * nt + t, 0)
    full2 = lambda b, t: (0, 0)
    full3 = lambda b, t: (0, 0, 0)
    return pl.pallas_call(
        _lru_body,
        grid=(batch, nt),
        in_specs=[
            pl.BlockSpec((T, LRU_WIDTH), row),
            pl.BlockSpec((T, LRU_WIDTH), row),
            pl.BlockSpec((CONV_WIDTH, LRU_WIDTH), full2),
            pl.BlockSpec((1, LRU_WIDTH), full2),
            pl.BlockSpec((LRU_PAIRS, LRU_PAIR, LRU_PAIR), full3),
            pl.BlockSpec((LRU_PAIRS, LRU_PAIR, LRU_PAIR), full3),
            pl.BlockSpec((1, LRU_WIDTH), full2),
            pl.BlockSpec((1, LRU_WIDTH), full2),
            pl.BlockSpec((1, LRU_WIDTH), full2),
        ],
        out_specs=pl.BlockSpec((T, LRU_WIDTH), row),
        out_shape=jax.ShapeDtypeStruct((n, LRU_WIDTH), BF16),
        scratch_shapes=[
            pltpu.VMEM((T + SUBLANES, LRU_WIDTH), F32),
            pltpu.VMEM((ncb, T, LANES), F32),
            pltpu.VMEM((ncb, T, LANES), F32),
            pltpu.VMEM((1, LRU_WIDTH), F32),
        ],
        compiler_params=_cparams(("parallel", "arbitrary")),
        name="rglru",
    )(u, gate, conv_w, conv_b.reshape(1, -1), pairs(w_r), pairs(w_i),
      b_r.reshape(1, -1), b_i.reshape(1, -1), lam.reshape(1, -1))


def _layer_norm_rows(y, g, b):
    mu = jnp.mean(y, axis=-1, keepdims=True)
    yc = y - mu
    var = jnp.mean(yc * yc, axis=-1, keepdims=True)
    return yc * lax.rsqrt(var + LN_EPS) * g + b


def _merge_body(alpha, ret_ref, lru_ref, gm_ref, x_ref, wro_ref, wlo_ref, wo_ref, g1_ref, b1_ref,
                wrt_ref, brt_ref, x1_ref, ri_ref, rw_ref, cnt_ref, carry_ref):
    i = pl.program_id(0)

    @pl.when(i == 0)
    def _():
        carry_ref[...] = jnp.zeros_like(carry_ref)

    pr = jnp.dot(ret_ref[...], wro_ref[...], preferred_element_type=F32)
    pu = jnp.dot(lru_ref[...], wlo_ref[...], preferred_element_type=F32)
    merged = gm_ref[:, :D_MODEL].astype(F32) * pr + gm_ref[:, D_MODEL:].astype(F32) * pu
    y = alpha * x_ref[...] + jnp.dot(merged.astype(BF16), wo_ref[...], preferred_element_type=F32)
    x1 = _layer_norm_rows(y, g1_ref[...], b1_ref[...])
    x1_ref[...] = x1

    tm = x1.shape[0]
    lg = jnp.dot(x1, wrt_ref[...], preferred_element_type=F32, precision=lax.Precision.HIGHEST) + brt_ref[...]
    lane = lax.broadcasted_iota(I32, (tm, LANES), 1)
    big = jnp.int32(LANES)
    neg = jnp.float32(-jnp.inf)
    gmask = lane < N_GROUPS
    gl = jnp.where(gmask, lg, neg)
    gmax = jnp.max(gl, axis=-1, keepdims=True)
    g_idx = jnp.min(jnp.where(gmask & (gl == gmax), lane, big), axis=-1, keepdims=True)
    g_w = 1.0 / jnp.sum(jnp.where(gmask, jnp.exp(gl - gmax), 0.0), axis=-1, keepdims=True)
    e_lo = N_GROUPS + EXPERTS_PER_GROUP * g_idx
    emask = (lane >= e_lo) & (lane < e_lo + EXPERTS_PER_GROUP)
    el = jnp.where(emask, lg, neg)
    v1 = jnp.max(el, axis=-1, keepdims=True)
    i1 = jnp.min(jnp.where(emask & (el == v1), lane, big), axis=-1, keepdims=True)
    emask2 = emask & (lane != i1)
    el2 = jnp.where(emask2, lg, neg)
    v2 = jnp.max(el2, axis=-1, keepdims=True)
    i2 = jnp.min(jnp.where(emask2 & (el2 == v2), lane, big), axis=-1, keepdims=True)
    ex = jnp.exp(v2 - v1)
    den = 1.0 + ex
    w1 = g_w / den
    w2 = g_w * ex / den
    e1 = i1 - N_GROUPS
    e2 = i2 - N_GROUPS

    oh = ((lane == e1).astype(F32) + (lane == e2).astype(F32))
    rowi = lax.broadcasted_iota(I32, (tm, tm), 0)
    coli = lax.broadcasted_iota(I32, (tm, tm), 1)
    tri = (coli < rowi).astype(BF16)
    before = jnp.dot(tri, oh.astype(BF16), preferred_element_type=F32) + carry_ref[...]
    r1 = jnp.sum(jnp.where(lane == e1, before, 0.0), axis=-1, keepdims=True)
    r2 = jnp.sum(jnp.where(lane == e2, before, 0.0), axis=-1, keepdims=True)
    new_carry = carry_ref[...] + jnp.sum(oh, axis=0, keepdims=True)
    carry_ref[...] = new_carry
    cnt_ref[...] = jnp.broadcast_to(new_carry, cnt_ref.shape)

    ri = jnp.where(lane == 0, e1, jnp.where(lane == 1, e2, jnp.where(
        lane == 2, r1.astype(I32), jnp.where(lane == 3, r2.astype(I32), 0))))
    ri_ref[...] = ri
    rw_ref[...] = jnp.where(lane == 0, w1, jnp.where(lane == 1, w2, 0.0))


def _merge(ret, lru, gm, x2, w_ret_o, w_lru_o, w_out, ln_g, ln_b, w_rt, b_rt, alpha):
    n = x2.shape[0]
    tm = min(MERGE_TM, n)
    rowmap = lambda i: (i, 0)
    full = lambda i: (0, 0)
    return pl.pallas_call(
        functools.partial(_merge_body, alpha),
        grid=(n // tm,),
        in_specs=[
            pl.BlockSpec((tm, RET_V), rowmap),
            pl.BlockSpec((tm, LRU_WIDTH), rowmap),
            pl.BlockSpec((tm, 2 * D_MODEL), rowmap),
            pl.BlockSpec((tm, D_MODEL), rowmap),
            pl.BlockSpec((RET_V, D_MODEL), full),
            pl.BlockSpec((LRU_WIDTH, D_MODEL), full),
            pl.BlockSpec((D_MODEL, D_MODEL), full),
            pl.BlockSpec((1, D_MODEL), full),
            pl.BlockSpec((1, D_MODEL), full),
            pl.BlockSpec((D_MODEL, LANES), full),
            pl.BlockSpec((1, LANES), full),
        ],
        out_specs=[
            pl.BlockSpec((tm, D_MODEL), rowmap),
            pl.BlockSpec((tm, LANES), rowmap),
            pl.BlockSpec((tm, LANES), rowmap),
            pl.BlockSpec((SUBLANES, LANES), full),
        ],
        out_shape=[
            jax.ShapeDtypeStruct((n, D_MODEL), F32),
            jax.ShapeDtypeStruct((n, LANES), I32),
            jax.ShapeDtypeStruct((n, LANES), F32),
            jax.ShapeDtypeStruct((SUBLANES, LANES), F32),
        ],
        scratch_shapes=[pltpu.VMEM((1, LANES), F32)],
        compiler_params=_cparams(("arbitrary",)),
        name="merge_ln_route",
    )(ret, lru, gm, x2, w_ret_o, w_lru_o, w_out, ln_g.reshape(1, -1), ln_b.reshape(1, -1), w_rt, b_rt)


def _row_copy(src_hbm, dst_hbm, sem, src_row, dst_row):
    return pltpu.make_async_copy(src_hbm.at[pl.ds(src_row, 1), :], dst_hbm.at[pl.ds(dst_row, 1), :], sem)


def _dispatch_body(dest_ref, x_hbm, xs_hbm, sem):
    base = pl.program_id(0) * DISP_BATCH

    def copy(r):
        cp = base + r
        return _row_copy(x_hbm, xs_hbm, sem, lax.shift_right_logical(cp, 1), dest_ref[cp])

    def issue(r, carry):
        copy(r).start()
        return carry

    def drain(r, carry):
        copy(r).wait()
        return carry

    lax.fori_loop(0, DISP_BATCH, issue, 0)
    lax.fori_loop(0, DISP_BATCH, drain, 0)


def _dispatch(dest, x1):
    m = dest.shape[0]
    assert TOP_K == 2 and m % DISP_BATCH == 0
    return pl.pallas_call(
        _dispatch_body,
        grid_spec=pltpu.PrefetchScalarGridSpec(
            num_scalar_prefetch=1,
            grid=(m // DISP_BATCH,),
            in_specs=[pl.BlockSpec(memory_space=pl.ANY)],
            out_specs=pl.BlockSpec(memory_space=pl.ANY),
            scratch_shapes=[pltpu.SemaphoreType.DMA],
        ),
        out_shape=jax.ShapeDtypeStruct((m, D_MODEL), F32),
        compiler_params=_cparams(("arbitrary",)),
        name="dispatch",
    )(dest, x1)


def _expert_body(tile_ref, exp_ref, lo_ref, hi_ref, first_ref, xs_ref, wg_ref, wu_ref, wd_ref, y_ref,
                 wgb_ref, wub_ref, wdb_ref, cur_ref):
    w = pl.program_id(0)
    e = exp_ref[w]
    lo = lo_ref[w]
    hi = hi_ref[w]

    @pl.when(w == 0)
    def _():
        cur_ref[0] = -1

    @pl.when(first_ref[w] == 1)
    def _():
        y_ref[...] = jnp.zeros_like(y_ref)

    @pl.when(hi > lo)
    def _():
        @pl.when(cur_ref[0] != e)
        def _():
            wgb_ref[...] = wg_ref[...].astype(BF16)
            wub_ref[...] = wu_ref[...].astype(BF16)
            wdb_ref[...] = wd_ref[...].astype(BF16)
            cur_ref[0] = e

        rt = xs_ref.shape[0]
        rowi = lax.broadcasted_iota(I32, (rt, 1), 0)
        live = (rowi >= lo) & (rowi < hi)
        xb = jnp.where(live, xs_ref[...], 0.0).astype(BF16)
        hg = jnp.dot(xb, wgb_ref[...], preferred_element_type=F32)
        hu = jnp.dot(xb, wub_ref[...], preferred_element_type=F32)
        hm = (hg * jax.nn.sigmoid(hg) * hu).astype(BF16)
        y_ref[...] += jnp.dot(hm, wdb_ref[...], preferred_element_type=F32)


def _experts(sched, xs, w_gate, w_up, w_down):
    m = xs.shape[0]
    tile, expert, lo, hi, first = sched
    nw = tile.shape[0]
    rt = min(MOE_RT, m)
    return pl.pallas_call(
        _expert_body,
        grid_spec=pltpu.PrefetchScalarGridSpec(
            num_scalar_prefetch=5,
            grid=(nw,),
            in_specs=[
                pl.BlockSpec((rt, D_MODEL), lambda w, t, e, l, h, f: (t[w], 0)),
                pl.BlockSpec((None, D_MODEL, D_EXPERT), lambda w, t, e, l, h, f: (e[w], 0, 0)),
                pl.BlockSpec((None, D_MODEL, D_EXPERT), lambda w, t, e, l, h, f: (e[w], 0, 0)),
                pl.BlockSpec((None, D_EXPERT, D_MODEL), lambda w, t, e, l, h, f: (e[w], 0, 0)),
            ],
            out_specs=pl.BlockSpec((rt, D_MODEL), lambda w, t, e, l, h, f: (t[w], 0)),
            scratch_shapes=[
                pltpu.VMEM((D_MODEL, D_EXPERT), BF16),
                pltpu.VMEM((D_MODEL, D_EXPERT), BF16),
                pltpu.VMEM((D_EXPERT, D_MODEL), BF16),
                pltpu.SMEM((1,), I32),
            ],
        ),
        out_shape=jax.ShapeDtypeStruct((m, D_MODEL), F32),
        compiler_params=_cparams(("arbitrary",)),
        name="experts",
    )(tile, expert, lo, hi, first, xs, w_gate, w_up, w_down)


def _expert_schedule(counts, m):
    rt = min(MOE_RT, m)
    n_tiles = m // rt
    ends = jnp.cumsum(counts)
    starts = ends - counts
    cuts = jnp.concatenate([jnp.arange(n_tiles, dtype=I32) * rt, starts.astype(I32)])
    nw = cuts.shape[0]
    idx = jnp.arange(nw, dtype=I32)
    before = (cuts[None, :] < cuts[:, None]) | ((cuts[None, :] == cuts[:, None]) & (idx[None, :] < idx[:, None]))
    pos = jnp.sum(before.astype(I32), axis=1)
    bp = jnp.sum(jnp.where(pos[None, :] == idx[:, None], cuts[None, :], 0), axis=1)
    nxt = jnp.concatenate([bp[1:], jnp.array([m], I32)])
    tile = jnp.minimum(bp // rt, n_tiles - 1)
    expert = jnp.minimum(jnp.sum((ends[None, :] <= bp[:, None]).astype(I32), axis=1), N_EXPERTS - 1)
    lo = bp - tile * rt
    hi = nxt - tile * rt
    first = jnp.concatenate([jnp.ones((1,), I32), (tile[1:] != tile[:-1]).astype(I32)])
    return tile.astype(I32), expert, lo.astype(I32), hi.astype(I32), first, starts


def _combine_body(alpha, dest_ref, x1_ref, rw_ref, g2_ref, b2_ref, ys_hbm, o_ref, buf_ref, sem):
    i = pl.program_id(0)
    tm = x1_ref.shape[0]
    base = i * tm * TOP_K

    def copy(r):
        k = lax.bitwise_and(r, 1)
        t = lax.shift_right_logical(r, 1)
        return pltpu.make_async_copy(ys_hbm.at[pl.ds(dest_ref[base + r], 1), :],
                                     buf_ref.at[k, pl.ds(t, 1), :], sem)

    def issue(r, carry):
        copy(r).start()
        return carry

    def drain(r, carry):
        copy(r).wait()
        return carry

    lax.fori_loop(0, tm * TOP_K, issue, 0)
    lax.fori_loop(0, tm * TOP_K, drain, 0)
    w1 = rw_ref[:, 0:1]
    w2 = rw_ref[:, 1:2]
    moe = buf_ref[0] * w1 + buf_ref[1] * w2
    y = alpha * x1_ref[...] + moe
    o_ref[...] = _layer_norm_rows(y, g2_ref[...], b2_ref[...])


def _combine(dest, x1, rw, ln_g, ln_b, ys, alpha):
    n = x1.shape[0]
    tm = min(COMB_TM, n)
    return pl.pallas_call(
        functools.partial(_combine_body, alpha),
        grid_spec=pltpu.PrefetchScalarGridSpec(
            num_scalar_prefetch=1,
            grid=(n // tm,),
            in_specs=[
                pl.BlockSpec((tm, D_MODEL), lambda i, d: (i, 0)),
                pl.BlockSpec((tm, LANES), lambda i, d: (i, 0)),
                pl.BlockSpec((1, D_MODEL), lambda i, d: (0, 0)),
                pl.BlockSpec((1, D_MODEL), lambda i, d: (0, 0)),
                pl.BlockSpec(memory_space=pl.ANY),
            ],
            out_specs=pl.BlockSpec((tm, D_MODEL), lambda i, d: (i, 0)),
            scratch_shapes=[pltpu.VMEM((TOP_K, tm, D_MODEL), F32), pltpu.SemaphoreType.DMA],
        ),
        out_shape=jax.ShapeDtypeStruct((n, D_MODEL), F32),
        compiler_params=_cparams(("arbitrary",)),
        name="combine_ln",
    )(dest, x1, rw, ln_g.reshape(1, -1), ln_b.reshape(1, -1), ys)


def _rotary_tables(seq):
    half = RET_DK // 2
    inv = ROPE_BASE ** (-jnp.arange(half, dtype=F32) / half)
    ang = jnp.arange(seq, dtype=F32)[:, None] * inv[None, :]
    return jnp.cos(ang), jnp.sin(ang)


def _layer(x, depth, w_in, b_merge, conv_w, conv_b, w_rg_r, b_rg_r, w_rg_i, b_rg_i, lru_lambda,
           w_ret_o, w_lru_o, w_out, ln1_g, ln1_b, w_group, b_group, w_exp_router, b_exp_router,
           w_e_gate, w_e_up, w_e_down, ln2_g, ln2_b):
    B, S, D = x.shape
    n = B * S
    alpha = (2.0 * depth) ** 0.25
    x2 = x.reshape(n, D)
    wb = w_in.astype(BF16)
    o_k, o_v, o_g, o_u, o_gl, o_gm = np.cumsum([RET_QK, RET_QK, RET_V, RET_V, LRU_WIDTH, LRU_WIDTH])
    cos, sin = _rotary_tables(S)

    qk = _proj(x2, wb[:, :o_v], "rot", S, (cos, sin))
    v = _proj(x2, wb[:, o_v:o_g], "plain", S)
    sg = _proj(x2, wb[:, o_g:o_u], "silu", S)
    u = _proj(x2, wb[:, o_u:o_gl], "plain", S)
    gl = _proj(x2, wb[:, o_gl:o_gm], "gelu", S)
    gm = _proj(x2, wb[:, o_gm:], "sigb", S, (b_merge.reshape(1, -1),))

    ret = _retention(qk, v, sg, B, S)
    lru = _rglru(u, gl, conv_w, conv_b, w_rg_r, b_rg_r, w_rg_i, b_rg_i, lru_lambda, B, S)

    w_rt = jnp.zeros((D, LANES), F32).at[:, :N_GROUPS].set(w_group).at[:, N_GROUPS:N_GROUPS + N_EXPERTS].set(w_exp_router)
    b_rt = jnp.zeros((1, LANES), F32).at[0, :N_GROUPS].set(b_group).at[0, N_GROUPS:N_GROUPS + N_EXPERTS].set(b_exp_router)
    x1, ri, rw, cnt = _merge(ret, lru, gm, x2, w_ret_o.astype(BF16), w_lru_o.astype(BF16), w_out.astype(BF16),
                             ln1_g, ln1_b, w_rt, b_rt, alpha)

    m = n * TOP_K
    counts = cnt[0, :N_EXPERTS].astype(I32)
    tile, expert, lo, hi, first, starts = _expert_schedule(counts, m)
    dest = (starts.astype(I32)[ri[:, :TOP_K]] + ri[:, TOP_K:2 * TOP_K]).reshape(m)

    xs = _dispatch(dest, x1)
    ys = _experts((tile, expert, lo, hi, first), xs, w_e_gate, w_e_up, w_e_down)
    out = _combine(dest, x1, rw, ln2_g, ln2_b, ys, alpha)
    return out.reshape(B, S, D)


def kernel(x, w_in, b_merge, conv_w, conv_b, w_rg_r, b_rg_r, w_rg_i, b_rg_i, lru_lambda, w_ret_o, w_lru_o, w_out, ln1_g, ln1_b, w_group, b_group, w_exp_router, b_exp_router, w_e_gate, w_e_up, w_e_down, ln2_g, ln2_b):
    depth = w_in.shape[0]
    for l in range(depth):
        x = _layer(x, depth, w_in[l], b_merge[l], conv_w[l], conv_b[l], w_rg_r[l], b_rg_r[l], w_rg_i[l],
                   b_rg_i[l], lru_lambda[l], w_ret_o[l], w_lru_o[l], w_out[l], ln1_g[l], ln1_b[l],
                   w_group[l], b_group[l], w_exp_router[l], b_exp_router[l], w_e_gate[l], w_e_up[l],
                   w_e_down[l], ln2_g[l], ln2_b[l])
    return x
```

```python
import functools

import jax
import jax.numpy as jnp
import numpy as np
from jax import lax
from jax.experimental import pallas as pl
from jax.experimental.pallas import tpu as pltpu

F32 = jnp.float32
BF16 = jnp.bfloat16
I32 = jnp.int32

D_MODEL = 1024
RET_HEADS = 4
RET_DK = 256
RET_DV = 512
RET_QK = RET_HEADS * RET_DK
RET_V = RET_HEADS * RET_DV
ROPE_BASE = 10000.0
LRU_WIDTH = 1536
LRU_BLOCKS = 8
LRU_BLOCK = LRU_WIDTH // LRU_BLOCKS
LRU_PAIR = 2 * LRU_BLOCK
LRU_PAIRS = LRU_BLOCKS // 2
CONV_WIDTH = 4
LRU_C = 8.0
N_GROUPS = 4
EXPERTS_PER_GROUP = 8
N_EXPERTS = N_GROUPS * EXPERTS_PER_GROUP
TOP_K = 2
D_EXPERT = 512
LN_EPS = 1e-5

LANES = 128
SUBLANES = 8
VMEM_LIMIT = 56 * 1024 * 1024

PROJ_TM = 512
PROJ_CHUNK = 512
RET_CHUNK = 256
LRU_T = 256
ROUTE_TM = 512
MOE_RT = 256
SEG_ALIGN = SUBLANES
RUN_BITS = (ROUTE_TM // SEG_ALIGN).bit_length()


def _cparams(sem):
    return pltpu.CompilerParams(dimension_semantics=sem, vmem_limit_bytes=VMEM_LIMIT)


def _proj_body(mode, *refs):
    if mode == "rot":
        x_ref, w_ref, cos_ref, sin_ref, o_ref = refs
    elif mode == "sigb":
        x_ref, w_ref, b_ref, o_ref = refs
    else:
        x_ref, w_ref, o_ref = refs
    xb = x_ref[...].astype(BF16)
    width = w_ref.shape[1]
    ch = RET_DK if mode == "rot" else min(PROJ_CHUNK, width)
    for jc in range(width // ch):
        cols = slice(jc * ch, (jc + 1) * ch)
        acc = jnp.dot(xb, w_ref[:, cols], preferred_element_type=F32)
        if mode == "rot":
            half = RET_DK // 2
            t1, t2 = acc[:, :half], acc[:, half:]
            c, s = cos_ref[...], sin_ref[...]
            scale = RET_DK ** -0.5 if jc >= RET_HEADS else 1.0
            o_ref[:, jc * ch:jc * ch + half] = ((t1 * c - t2 * s) * scale).astype(o_ref.dtype)
            o_ref[:, jc * ch + half:(jc + 1) * ch] = ((t1 * s + t2 * c) * scale).astype(o_ref.dtype)
        elif mode == "silu":
            o_ref[:, cols] = (acc * jax.nn.sigmoid(acc)).astype(o_ref.dtype)
        elif mode == "gelu":
            o_ref[:, cols] = jax.nn.gelu(acc).astype(o_ref.dtype)
        elif mode == "sigb":
            o_ref[:, cols] = jax.nn.sigmoid(acc + b_ref[:, cols]).astype(o_ref.dtype)
        else:
            o_ref[:, cols] = acc.astype(o_ref.dtype)


def _proj(x2, w, mode, seq, extra=(), out_dtype=BF16):
    n, d = x2.shape
    width = w.shape[1]
    tm = min(PROJ_TM, seq)
    in_specs = [pl.BlockSpec((tm, d), lambda i: (i, 0)),
                pl.BlockSpec((d, width), lambda i: (0, 0))]
    if mode == "rot":
        per_seq = seq // tm
        spec = pl.BlockSpec((tm, RET_DK // 2), lambda i: (i % per_seq, 0))
        in_specs += [spec, spec]
    elif mode == "sigb":
        in_specs += [pl.BlockSpec((1, width), lambda i: (0, 0))]
    return pl.pallas_call(
        functools.partial(_proj_body, mode),
        grid=(n // tm,),
        in_specs=in_specs,
        out_specs=pl.BlockSpec((tm, width), lambda i: (i, 0)),
        out_shape=jax.ShapeDtypeStruct((n, width), out_dtype),
        compiler_params=_cparams(("parallel",)),
        name="proj_" + mode,
    )(x2, w, *extra)


def _ret_body(dec_ref, q_ref, k_ref, v_ref, sg_ref, dm_ref, xi_ref, zeta_ref, o_ref, st_ref):
    c = pl.program_id(1)

    @pl.when(c == 0)
    def _():
        st_ref[...] = jnp.zeros_like(st_ref)

    for h in range(RET_HEADS):
        qc = slice(h * RET_DK, (h + 1) * RET_DK)
        vc = slice(h * RET_DV, (h + 1) * RET_DV)
        q, k, v = q_ref[:, qc], k_ref[:, qc], v_ref[:, vc]
        scores = lax.dot_general(q, k, (((1,), (1,)), ((), ())), preferred_element_type=F32) * dm_ref[h]
        inner = jnp.dot(scores.astype(BF16), v, preferred_element_type=F32)
        st = st_ref[h]
        cross = jnp.dot(q, st.astype(BF16), preferred_element_type=F32) * xi_ref[h]
        kz = (k.astype(F32) * zeta_ref[h]).astype(BF16)
        upd = lax.dot_general(kz, v, (((0,), (0,)), ((), ())), preferred_element_type=F32)
        st_ref[h] = st * dec_ref[h] + upd
        o = inner + cross
        mu = jnp.mean(o, axis=-1, keepdims=True)
        oc = o - mu
        var = jnp.mean(oc * oc, axis=-1, keepdims=True)
        on = oc * lax.rsqrt(var + LN_EPS)
        o_ref[:, vc] = (sg_ref[:, vc].astype(F32) * on).astype(o_ref.dtype)


def _retention(qk, v, sg, batch, seq):
    n = qk.shape[0]
    C = min(RET_CHUNK, seq)
    nc = seq // C
    H = RET_HEADS
    log_g = jnp.log1p(-(2.0 ** (-5.0 - jnp.arange(H, dtype=F32))))
    pos = jnp.arange(C, dtype=F32)
    diff = pos[:, None] - pos[None, :]
    causal = diff >= 0
    d_mask = jnp.where(causal[None], jnp.exp(log_g[:, None, None] * jnp.where(causal, diff, 0.0)[None]), 0.0)
    xi = jnp.exp(log_g[:, None] * (pos + 1.0)[None])[:, :, None]
    zeta = jnp.exp(log_g[:, None] * (C - 1.0 - pos)[None])[:, :, None]
    chunk_decay = jnp.exp(log_g * C)
    row = lambda b, c: b * nc + c
    full3 = lambda b, c: (0, 0, 0)
    return pl.pallas_call(
        _ret_body,
        grid=(batch, nc),
        in_specs=[
            pl.BlockSpec(memory_space=pltpu.SMEM),
            pl.BlockSpec((C, RET_QK), lambda b, c: (row(b, c), 0)),
            pl.BlockSpec((C, RET_QK), lambda b, c: (row(b, c), 1)),
            pl.BlockSpec((C, RET_V), lambda b, c: (row(b, c), 0)),
            pl.BlockSpec((C, RET_V), lambda b, c: (row(b, c), 0)),
            pl.BlockSpec((H, C, C), full3),
            pl.BlockSpec((H, C, 1), full3),
            pl.BlockSpec((H, C, 1), full3),
        ],
        out_specs=pl.BlockSpec((C, RET_V), lambda b, c: (row(b, c), 0)),
        out_shape=jax.ShapeDtypeStruct((n, RET_V), BF16),
        scratch_shapes=[pltpu.VMEM((H, RET_DK, RET_DV), F32)],
        compiler_params=_cparams(("parallel", "arbitrary")),
        name="retention",
    )(chunk_decay, qk, qk, v, sg, d_mask, xi, zeta)


def _lru_body(u_ref, g_ref, cw_ref, cb_ref, wr_ref, wi_ref, br_ref, bi_ref, lam_ref,
              o_ref, ubuf_ref, a_ref, b_ref, carry_ref):
    T = u_ref.shape[0]
    G = T // SUBLANES
    P = _seg_pitch(T)
    ncb = LRU_WIDTH // LANES
    t = pl.program_id(1)

    @pl.when(t == 0)
    def _():
        ubuf_ref[0:SUBLANES, :] = jnp.zeros((SUBLANES, LRU_WIDTH), F32)
        carry_ref[...] = jnp.zeros_like(carry_ref)

    ubuf_ref[SUBLANES:SUBLANES + T, :] = u_ref[...].astype(F32)
    cw = cw_ref[...]
    uc = cb_ref[...] + cw[CONV_WIDTH - 1:CONV_WIDTH, :] * ubuf_ref[SUBLANES:SUBLANES + T, :]
    for j in range(CONV_WIDTH - 1):
        back = CONV_WIDTH - 1 - j
        uc = uc + cw[j:j + 1, :] * ubuf_ref[SUBLANES - back:SUBLANES - back + T, :]
    ubuf_ref[0:SUBLANES, :] = ubuf_ref[T:T + SUBLANES, :]

    ucb = uc.astype(BF16)
    neg_c_sp = -LRU_C * jax.nn.softplus(-lam_ref[...])
    for p in range(LRU_PAIRS):
        lo = p * LRU_PAIR
        sl = ucb[:, lo:lo + LRU_PAIR]
        r = jax.nn.sigmoid(jnp.dot(sl, wr_ref[p], preferred_element_type=F32) + br_ref[:, lo:lo + LRU_PAIR])
        i = jax.nn.sigmoid(jnp.dot(sl, wi_ref[p], preferred_element_type=F32) + bi_ref[:, lo:lo + LRU_PAIR])
        log_a = r * neg_c_sp[:, lo:lo + LRU_PAIR]
        a = jnp.exp(log_a)
        inp = jnp.sqrt(-jnp.tanh(log_a) * (a * a + 1.0)) * (i * uc[:, lo:lo + LRU_PAIR])
        for cc in range(LRU_PAIR // LANES):
            cb = p * (LRU_PAIR // LANES) + cc
            for s in range(SUBLANES):
                a_ref[cb, s * P:s * P + G, :] = a[s * G:(s + 1) * G, cc * LANES:(cc + 1) * LANES]
                b_ref[cb, s * P:s * P + G, :] = inp[s * G:(s + 1) * G, cc * LANES:(cc + 1) * LANES]

    def step(j, hp):
        hs, ps = hp
        nh, npr = [], []
        for cb in range(ncb):
            a = a_ref[cb, pl.ds(j, SUBLANES, stride=P), :]
            b = b_ref[cb, pl.ds(j, SUBLANES, stride=P), :]
            hn = a * hs[cb] + b
            pn = a * ps[cb]
            b_ref[cb, pl.ds(j, SUBLANES, stride=P), :] = hn
            a_ref[cb, pl.ds(j, SUBLANES, stride=P), :] = pn
            nh.append(hn)
            npr.append(pn)
        return tuple(nh), tuple(npr)

    zeros = tuple(jnp.zeros((SUBLANES, LANES), F32) for _ in range(ncb))
    ones = tuple(jnp.ones((SUBLANES, LANES), F32) for _ in range(ncb))
    h_end, p_end = lax.fori_loop(0, G, step, (zeros, ones))

    for cb in range(ncb):
        cin = carry_ref[:, cb * LANES:(cb + 1) * LANES]
        for s in range(SUBLANES):
            rows = slice(s * G, (s + 1) * G)
            seg = slice(s * P, s * P + G)
            hseg = b_ref[cb, seg, :] + a_ref[cb, seg, :] * cin
            gate = g_ref[rows, cb * LANES:(cb + 1) * LANES].astype(F32)
            o_ref[rows, cb * LANES:(cb + 1) * LANES] = (gate * hseg).astype(o_ref.dtype)
            cin = h_end[cb][s:s + 1, :] + p_end[cb][s:s + 1, :] * cin
        carry_ref[:, cb * LANES:(cb + 1) * LANES] = cin


def _seg_pitch(t):
    g = t // SUBLANES
    units = -(-g // SUBLANES)
    return SUBLANES * (units + 1 - units % 2)


def _rglru(u, gate, conv_w, conv_b, w_r, b_r, w_i, b_i, lam, batch, seq):
    n = u.shape[0]
    T = min(LRU_T, seq)
    nt = seq // T
    ncb = LRU_WIDTH // LANES

    def pairs(w):
        z = jnp.zeros((LRU_PAIRS, LRU_PAIR, LRU_PAIR), F32)
        w4 = w.reshape(LRU_PAIRS, 2, LRU_BLOCK, LRU_BLOCK)
        z = z.at[:, :LRU_BLOCK, :LRU_BLOCK].set(w4[:, 0])
        z = z.at[:, LRU_BLOCK:, LRU_BLOCK:].set(w4[:, 1])
        return z.astype(BF16)

    row = lambda b, t: (b * nt + t, 0)
    full2 = lambda b, t: (0, 0)
    full3 = lambda b, t: (0, 0, 0)
    return pl.pallas_call(
        _lru_body,
        grid=(batch, nt),
        in_specs=[
            pl.BlockSpec((T, LRU_WIDTH), row),
            pl.BlockSpec((T, LRU_WIDTH), row),
            pl.BlockSpec((CONV_WIDTH, LRU_WIDTH), full2),
            pl.BlockSpec((1, LRU_WIDTH), full2),
            pl.BlockSpec((LRU_PAIRS, LRU_PAIR, LRU_PAIR), full3),
            pl.BlockSpec((LRU_PAIRS, LRU_PAIR, LRU_PAIR), full3),
            pl.BlockSpec((1, LRU_WIDTH), full2),
            pl.BlockSpec((1, LRU_WIDTH), full2),
            pl.BlockSpec((1, LRU_WIDTH), full2),
        ],
        out_specs=pl.BlockSpec((T, LRU_WIDTH), row),
        out_shape=jax.ShapeDtypeStruct((n, LRU_WIDTH), BF16),
        scratch_shapes=[
            pltpu.VMEM((T + SUBLANES, LRU_WIDTH), F32),
            pltpu.VMEM((ncb, SUBLANES * _seg_pitch(T), LANES), F32),
            pltpu.VMEM((ncb, SUBLANES * _seg_pitch(T), LANES), F32),
            pltpu.VMEM((1, LRU_WIDTH), F32),
        ],
        compiler_params=_cparams(("parallel", "arbitrary")),
        name="rglru",
    )(u, gate, conv_w, conv_b.reshape(1, -1), pairs(w_r), pairs(w_i),
      b_r.reshape(1, -1), b_i.reshape(1, -1), lam.reshape(1, -1))


def _layer_norm_rows(y, g, b):
    mu = jnp.mean(y, axis=-1, keepdims=True)
    yc = y - mu
    var = jnp.mean(yc * yc, axis=-1, keepdims=True)
    return yc * lax.rsqrt(var + LN_EPS) * g + b


def _merge_body(alpha, ret_ref, lru_ref, gm_ref, x_ref, wro_ref, wlo_ref, wo_ref, g1_ref, b1_ref,
                wrt_ref, brt_ref, x1_ref, rw_ref, lpc_ref, lpr_ref, cnt_ref):
    pr = jnp.dot(ret_ref[...], wro_ref[...], preferred_element_type=F32)
    pu = jnp.dot(lru_ref[...], wlo_ref[...], preferred_element_type=F32)
    merged = gm_ref[:, :D_MODEL].astype(F32) * pr + gm_ref[:, D_MODEL:].astype(F32) * pu
    y = alpha * x_ref[...] + jnp.dot(merged.astype(BF16), wo_ref[...], preferred_element_type=F32)
    x1 = _layer_norm_rows(y, g1_ref[...], b1_ref[...])
    x1_ref[...] = x1

    tm = x1.shape[0]
    x_hi = x1.astype(BF16)
    x_lo = (x1 - x_hi.astype(F32)).astype(BF16)
    hh = jnp.dot(x_hi, wrt_ref[...], preferred_element_type=F32)
    lh = jnp.dot(x_lo, wrt_ref[:, :LANES], preferred_element_type=F32)
    lg = hh[:, :LANES] + hh[:, LANES:] + lh + brt_ref[...]
    lane = lax.broadcasted_iota(I32, (tm, LANES), 1)
    big = jnp.int32(LANES)
    neg = jnp.float32(-jnp.inf)
    gmask = lane < N_GROUPS
    gl = jnp.where(gmask, lg, neg)
    gmax = jnp.max(gl, axis=-1, keepdims=True)
    g_idx = jnp.min(jnp.where(gmask & (gl == gmax), lane, big), axis=-1, keepdims=True)
    g_w = 1.0 / jnp.sum(jnp.where(gmask, jnp.exp(gl - gmax), 0.0), axis=-1, keepdims=True)
    e_lo = N_GROUPS + EXPERTS_PER_GROUP * g_idx
    emask = (lane >= e_lo) & (lane < e_lo + EXPERTS_PER_GROUP)
    el = jnp.where(emask, lg, neg)
    v1 = jnp.max(el, axis=-1, keepdims=True)
    i1 = jnp.min(jnp.where(emask & (el == v1), lane, big), axis=-1, keepdims=True)
    emask2 = emask & (lane != i1)
    el2 = jnp.where(emask2, lg, neg)
    v2 = jnp.max(el2, axis=-1, keepdims=True)
    i2 = jnp.min(jnp.where(emask2 & (el2 == v2), lane, big), axis=-1, keepdims=True)
    ex = jnp.exp(v2 - v1)
    den = 1.0 + ex
    w1 = g_w / den
    w2 = g_w * ex / den
    e1 = i1 - N_GROUPS
    e2 = i2 - N_GROUPS
    rw_ref[...] = jnp.where(lane == 0, w1, jnp.where(lane == 1, w2, 0.0))

    oh = (lane == e1).astype(F32) + (lane == e2).astype(F32)
    rowi = lax.broadcasted_iota(I32, (tm, tm), 0)
    coli = lax.broadcasted_iota(I32, (tm, tm), 1)
    tri = jnp.where(coli < rowi, 1.0, 0.0).astype(BF16)
    before = jnp.dot(tri, oh.astype(BF16), preferred_element_type=F32)
    cnt = jnp.sum(oh, axis=0, keepdims=True)
    units = jnp.floor((cnt + (SEG_ALIGN - 1.0)) * (1.0 / SEG_ALIGN))
    er = lax.broadcasted_iota(I32, (LANES, LANES), 0)
    ec = lax.broadcasted_iota(I32, (LANES, LANES), 1)
    upper = jnp.where(er < ec, 1.0, 0.0).astype(BF16)
    offs = SEG_ALIGN * jnp.dot(jnp.broadcast_to(units, (SUBLANES, LANES)).astype(BF16), upper,
                               preferred_element_type=F32)[0:1, :]
    pos = before + offs
    lp1 = jnp.sum(jnp.where(lane == e1, pos, 0.0), axis=-1, keepdims=True)
    lp2 = jnp.sum(jnp.where(lane == e2, pos, 0.0), axis=-1, keepdims=True)
    lpc = jnp.where(lane == 0, lp1, jnp.where(lane == 1, lp2, 0.0))
    lpc_ref[...] = lpc
    lpr_ref[...] = lpc.T[0:SUBLANES, :]
    cnt_ref[...] = jnp.broadcast_to(cnt, cnt_ref.shape)


def _merge(ret, lru, gm, x2, w_ret_o, w_lru_o, w_out, ln_g, ln_b, w_rt, b_rt, alpha):
    n = x2.shape[0]
    tm = min(ROUTE_TM, n)
    rowmap = lambda i: (i, 0)
    full = lambda i: (0, 0)
    return pl.pallas_call(
        functools.partial(_merge_body, alpha),
        grid=(n // tm,),
        in_specs=[
            pl.BlockSpec((tm, RET_V), rowmap),
            pl.BlockSpec((tm, LRU_WIDTH), rowmap),
            pl.BlockSpec((tm, 2 * D_MODEL), rowmap),
            pl.BlockSpec((tm, D_MODEL), rowmap),
            pl.BlockSpec((RET_V, D_MODEL), full),
            pl.BlockSpec((LRU_WIDTH, D_MODEL), full),
            pl.BlockSpec((D_MODEL, D_MODEL), full),
            pl.BlockSpec((1, D_MODEL), full),
            pl.BlockSpec((1, D_MODEL), full),
            pl.BlockSpec((D_MODEL, 2 * LANES), full),
            pl.BlockSpec((1, LANES), full),
        ],
        out_specs=[
            pl.BlockSpec((tm, D_MODEL), rowmap),
            pl.BlockSpec((tm, LANES), rowmap),
            pl.BlockSpec((tm, LANES), rowmap),
            pl.BlockSpec((SUBLANES, tm), lambda i: (0, i)),
            pl.BlockSpec((None, SUBLANES, LANES), lambda i: (i, 0, 0)),
        ],
        out_shape=[
            jax.ShapeDtypeStruct((n, D_MODEL), F32),
            jax.ShapeDtypeStruct((n, LANES), F32),
            jax.ShapeDtypeStruct((n, LANES), F32),
            jax.ShapeDtypeStruct((SUBLANES, n), F32),
            jax.ShapeDtypeStruct((n // tm, SUBLANES, LANES), F32),
        ],
        compiler_params=_cparams(("parallel",)),
        name="merge_ln_route",
    )(ret, lru, gm, x2, w_ret_o, w_lru_o, w_out, ln_g.reshape(1, -1), ln_b.reshape(1, -1), w_rt, b_rt)


def _run_copies(units_ref, gdst_ref, tile, loc_ref, glob_hbm, sem, to_global, wait):
    def expert(e, off):
        u = units_ref[tile * N_EXPERTS + e]
        dst = gdst_ref[tile * N_EXPERTS + e]
        for b in range(RUN_BITS):
            rows = SEG_ALIGN << b
            bit = lax.bitwise_and(lax.shift_right_logical(u, b), 1)

            @pl.when(bit == 1)
            def _():
                l = loc_ref.at[pl.ds(pl.multiple_of(off, SEG_ALIGN), rows), :]
                g = glob_hbm.at[pl.ds(pl.multiple_of(dst, SEG_ALIGN), rows), :]
                cp = pltpu.make_async_copy(l, g, sem) if to_global else pltpu.make_async_copy(g, l, sem)
                if wait:
                    cp.wait()
                else:
                    cp.start()

            off = off + bit * rows
            dst = dst + bit * rows
        return off

    lax.fori_loop(0, N_EXPERTS, expert, jnp.int32(0))


def _onehot_rows(lpr_ref, n_rows):
    tm = lpr_ref.shape[1]
    sub = lax.broadcasted_iota(I32, (n_rows, tm), 0)
    lp1 = lpr_ref[0:1, :].astype(I32)
    lp2 = lpr_ref[1:2, :].astype(I32)
    return jnp.where((sub == lp1) | (sub == lp2), 1.0, 0.0).astype(BF16)


def _dispatch_body(units_ref, gdst_ref, total_ref, x1_ref, lpr_ref, xs_hbm, loc_ref, zero_ref, sem):
    i = pl.program_id(0)
    perm = _onehot_rows(lpr_ref, loc_ref.shape[0])
    loc_ref[...] = jnp.dot(perm, x1_ref[...].astype(BF16), preferred_element_type=F32)
    _run_copies(units_ref, gdst_ref, i, loc_ref, xs_hbm, sem, True, False)
    _run_copies(units_ref, gdst_ref, i, loc_ref, xs_hbm, sem, True, True)

    @pl.when(i == pl.num_programs(0) - 1)
    def _():
        zero_ref[...] = jnp.zeros_like(zero_ref)
        _zero_tail(total_ref[0], zero_ref, xs_hbm, sem, False)
        _zero_tail(total_ref[0], zero_ref, xs_hbm, sem, True)


def _zero_tail(total, zero_ref, xs_hbm, sem, wait):
    zr = zero_ref.shape[0]
    shift = zr.bit_length() - 1
    assert zr == 1 << shift and xs_hbm.shape[0] % SEG_ALIGN == 0
    dead = xs_hbm.shape[0] - total
    n_full = lax.shift_right_logical(dead, shift)

    def go(cp):
        if wait:
            cp.wait()
        else:
            cp.start()

    def full(k, carry):
        dst = xs_hbm.at[pl.ds(pl.multiple_of(total + k * zr, SEG_ALIGN), zr), :]
        go(pltpu.make_async_copy(zero_ref, dst, sem))
        return carry

    lax.fori_loop(0, n_full, full, 0)
    pos = total + n_full * zr
    rem = lax.shift_right_logical(dead - n_full * zr, SEG_ALIGN.bit_length() - 1)
    for b in range((zr // SEG_ALIGN).bit_length() - 1):
        rows = SEG_ALIGN << b
        bit = lax.bitwise_and(lax.shift_right_logical(rem, b), 1)

        @pl.when(bit == 1)
        def _():
            dst = xs_hbm.at[pl.ds(pl.multiple_of(pos, SEG_ALIGN), rows), :]
            go(pltpu.make_async_copy(zero_ref.at[pl.ds(0, rows), :], dst, sem))

        pos = pos + bit * rows


def _local_rows(tm):
    return TOP_K * tm + N_EXPERTS * SEG_ALIGN


def _dispatch(units, gdst, total, x1, lpr, m_max):
    n = x1.shape[0]
    tm = min(ROUTE_TM, n)
    return pl.pallas_call(
        _dispatch_body,
        grid_spec=pltpu.PrefetchScalarGridSpec(
            num_scalar_prefetch=3,
            grid=(n // tm,),
            in_specs=[pl.BlockSpec((tm, D_MODEL), lambda i, u, g, t: (i, 0)),
                      pl.BlockSpec((SUBLANES, tm), lambda i, u, g, t: (0, i))],
            out_specs=pl.BlockSpec(memory_space=pl.ANY),
            scratch_shapes=[pltpu.VMEM((_local_rows(tm), D_MODEL), F32),
                            pltpu.VMEM((MOE_RT, D_MODEL), F32), pltpu.SemaphoreType.DMA],
        ),
        out_shape=jax.ShapeDtypeStruct((m_max, D_MODEL), F32),
        compiler_params=_cparams(("arbitrary",)),
        name="dispatch",
    )(units, gdst, total, x1, lpr)


def _expert_body(tile_ref, exp_ref, lo_ref, hi_ref, first_ref, xs_ref, wg_ref, wu_ref, wd_ref, y_ref,
                 wgb_ref, wub_ref, wdb_ref, cur_ref):
    w = pl.program_id(0)
    e = exp_ref[w]
    lo = lo_ref[w]
    hi = hi_ref[w]

    @pl.when(w == 0)
    def _():
        cur_ref[0] = -1

    @pl.when(first_ref[w] == 1)
    def _():
        y_ref[...] = jnp.zeros_like(y_ref)

    @pl.when(hi > lo)
    def _():
        @pl.when(cur_ref[0] != e)
        def _():
            wgb_ref[...] = wg_ref[...].astype(BF16)
            wub_ref[...] = wu_ref[...].astype(BF16)
            wdb_ref[...] = wd_ref[...].astype(BF16)
            cur_ref[0] = e

        rt = xs_ref.shape[0]
        rowi = lax.broadcasted_iota(I32, (rt, 1), 0)
        live = (rowi >= lo) & (rowi < hi)
        xb = jnp.where(live, xs_ref[...], 0.0).astype(BF16)
        hg = jnp.dot(xb, wgb_ref[...], preferred_element_type=F32)
        hu = jnp.dot(xb, wub_ref[...], preferred_element_type=F32)
        hm = (hg * jax.nn.sigmoid(hg) * hu).astype(BF16)
        y_ref[...] += jnp.dot(hm, wdb_ref[...], preferred_element_type=F32)


def _experts(sched, xs, w_gate, w_up, w_down):
    m = xs.shape[0]
    tile, expert, lo, hi, first = sched
    nw = tile.shape[0]
    rt = min(MOE_RT, m)
    return pl.pallas_call(
        _expert_body,
        grid_spec=pltpu.PrefetchScalarGridSpec(
            num_scalar_prefetch=5,
            grid=(nw,),
            in_specs=[
                pl.BlockSpec((rt, D_MODEL), lambda w, t, e, l, h, f: (t[w], 0)),
                pl.BlockSpec((None, D_MODEL, D_EXPERT), lambda w, t, e, l, h, f: (e[w], 0, 0)),
                pl.BlockSpec((None, D_MODEL, D_EXPERT), lambda w, t, e, l, h, f: (e[w], 0, 0)),
                pl.BlockSpec((None, D_EXPERT, D_MODEL), lambda w, t, e, l, h, f: (e[w], 0, 0)),
            ],
            out_specs=pl.BlockSpec((rt, D_MODEL), lambda w, t, e, l, h, f: (t[w], 0)),
            scratch_shapes=[
                pltpu.VMEM((D_MODEL, D_EXPERT), BF16),
                pltpu.VMEM((D_MODEL, D_EXPERT), BF16),
                pltpu.VMEM((D_EXPERT, D_MODEL), BF16),
                pltpu.SMEM((1,), I32),
            ],
        ),
        out_shape=jax.ShapeDtypeStruct((m, D_MODEL), F32),
        compiler_params=_cparams(("arbitrary",)),
        name="experts",
    )(tile, expert, lo, hi, first, xs, w_gate, w_up, w_down)


def _expert_schedule(sizes, m_max):
    rt = min(MOE_RT, m_max)
    n_tiles = m_max // rt
    ends = jnp.cumsum(sizes)
    starts = ends - sizes
    total = ends[-1]
    cuts = jnp.concatenate([jnp.arange(n_tiles, dtype=I32) * rt, starts.astype(I32)])
    nw = cuts.shape[0]
    idx = jnp.arange(nw, dtype=I32)
    before = (cuts[None, :] < cuts[:, None]) | ((cuts[None, :] == cuts[:, None]) & (idx[None, :] < idx[:, None]))
    pos = jnp.sum(before.astype(I32), axis=1)
    bp = jnp.sum(jnp.where(pos[None, :] == idx[:, None], cuts[None, :], 0), axis=1)
    nxt = jnp.concatenate([bp[1:], jnp.array([m_max], I32)])
    tile = jnp.minimum(bp // rt, n_tiles - 1)
    expert = jnp.minimum(jnp.sum((ends[None, :] <= bp[:, None]).astype(I32), axis=1), N_EXPERTS - 1)
    lo = bp - tile * rt
    hi = jnp.maximum(jnp.minimum(nxt, total) - tile * rt, lo)
    first = jnp.concatenate([jnp.ones((1,), I32), (tile[1:] != tile[:-1]).astype(I32)])
    return tile.astype(I32), expert.astype(I32), lo.astype(I32), hi.astype(I32), first


def _combine_body(alpha, units_ref, gdst_ref, x1_ref, rw_ref, lpc_ref, g2_ref, b2_ref, ys_hbm, o_ref,
                  loc_ref, sem):
    i = pl.program_id(0)

    @pl.when(i == 0)
    def _():
        loc_ref[...] = jnp.zeros_like(loc_ref)

    _run_copies(units_ref, gdst_ref, i, loc_ref, ys_hbm, sem, False, False)
    _run_copies(units_ref, gdst_ref, i, loc_ref, ys_hbm, sem, False, True)
    tm = x1_ref.shape[0]
    n_rows = loc_ref.shape[0]
    yb = loc_ref[...].astype(BF16)
    lanes = lax.broadcasted_iota(I32, (tm, n_rows), 1)
    sel1 = jnp.where(lanes == lpc_ref[:, 0:1].astype(I32), 1.0, 0.0).astype(BF16)
    sel2 = jnp.where(lanes == lpc_ref[:, 1:2].astype(I32), 1.0, 0.0).astype(BF16)
    moe = (rw_ref[:, 0:1] * jnp.dot(sel1, yb, preferred_element_type=F32)
           + rw_ref[:, 1:2] * jnp.dot(sel2, yb, preferred_element_type=F32))
    y = alpha * x1_ref[...] + moe
    o_ref[...] = _layer_norm_rows(y, g2_ref[...], b2_ref[...])


def _combine(units, gdst, x1, rw, lpc, ln_g, ln_b, ys, alpha):
    n = x1.shape[0]
    tm = min(ROUTE_TM, n)
    rowmap = lambda i, u, g: (i, 0)
    full = lambda i, u, g: (0, 0)
    return pl.pallas_call(
        functools.partial(_combine_body, alpha),
        grid_spec=pltpu.PrefetchScalarGridSpec(
            num_scalar_prefetch=2,
            grid=(n // tm,),
            in_specs=[
                pl.BlockSpec((tm, D_MODEL), rowmap),
                pl.BlockSpec((tm, LANES), rowmap),
                pl.BlockSpec((tm, LANES), rowmap),
                pl.BlockSpec((1, D_MODEL), full),
                pl.BlockSpec((1, D_MODEL), full),
                pl.BlockSpec(memory_space=pl.ANY),
            ],
            out_specs=pl.BlockSpec((tm, D_MODEL), rowmap),
            scratch_shapes=[pltpu.VMEM((_local_rows(tm), D_MODEL), F32), pltpu.SemaphoreType.DMA],
        ),
        out_shape=jax.ShapeDtypeStruct((n, D_MODEL), F32),
        compiler_params=_cparams(("arbitrary",)),
        name="combine_ln",
    )(units, gdst, x1, rw, lpc, ln_g.reshape(1, -1), ln_b.reshape(1, -1), ys)


def _rotary_tables(seq):
    half = RET_DK // 2
    inv = ROPE_BASE ** (-jnp.arange(half, dtype=F32) / half)
    ang = jnp.arange(seq, dtype=F32)[:, None] * inv[None, :]
    return jnp.cos(ang), jnp.sin(ang)


def _router_weights(w_group, b_group, w_exp_router, b_exp_router):
    d = w_group.shape[0]
    w = jnp.zeros((d, LANES), F32).at[:, :N_GROUPS].set(w_group).at[:, N_GROUPS:N_GROUPS + N_EXPERTS].set(w_exp_router)
    b = jnp.zeros((1, LANES), F32).at[0, :N_GROUPS].set(b_group).at[0, N_GROUPS:N_GROUPS + N_EXPERTS].set(b_exp_router)
    w_hi = w.astype(BF16)
    w_lo = (w - w_hi.astype(F32)).astype(BF16)
    return jnp.concatenate([w_hi, w_lo], axis=1), b


def _layer(x, depth, w_in, b_merge, conv_w, conv_b, w_rg_r, b_rg_r, w_rg_i, b_rg_i, lru_lambda,
           w_ret_o, w_lru_o, w_out, ln1_g, ln1_b, w_group, b_group, w_exp_router, b_exp_router,
           w_e_gate, w_e_up, w_e_down, ln2_g, ln2_b):
    B, S, D = x.shape
    n = B * S
    alpha = (2.0 * depth) ** 0.25
    x2 = x.reshape(n, D)
    wb = w_in.astype(BF16)
    o_k, o_v, o_g, o_u, o_gl, o_gm = np.cumsum([RET_QK, RET_QK, RET_V, RET_V, LRU_WIDTH, LRU_WIDTH])
    cos, sin = _rotary_tables(S)

    qk = _proj(x2, wb[:, :o_v], "rot", S, (cos, sin))
    v = _proj(x2, wb[:, o_v:o_g], "plain", S)
    sg = _proj(x2, wb[:, o_g:o_u], "silu", S)
    u = _proj(x2, wb[:, o_u:o_gl], "plain", S)
    gl = _proj(x2, wb[:, o_gl:o_gm], "gelu", S)
    gm = _proj(x2, wb[:, o_gm:], "sigb", S, (b_merge.reshape(1, -1),))

    ret = _retention(qk, v, sg, B, S)
    lru = _rglru(u, gl, conv_w, conv_b, w_rg_r, b_rg_r, w_rg_i, b_rg_i, lru_lambda, B, S)

    w_rt, b_rt = _router_weights(w_group, b_group, w_exp_router, b_exp_router)
    x1, rw, lpc, lpr, tcnt = _merge(ret, lru, gm, x2, w_ret_o.astype(BF16), w_lru_o.astype(BF16),
                                    w_out.astype(BF16), ln1_g, ln1_b, w_rt, b_rt, alpha)

    tm = min(ROUTE_TM, n)
    n_t = n // tm
    cnt = tcnt[:, 0, :N_EXPERTS].astype(I32)
    run = (cnt + (SEG_ALIGN - 1)) // SEG_ALIGN * SEG_ALIGN
    sizes = jnp.sum(run, axis=0)
    e_start = jnp.cumsum(sizes) - sizes
    gdst = (e_start[None, :] + jnp.cumsum(run, axis=0) - run).reshape(-1).astype(I32)
    units = (run // SEG_ALIGN).reshape(-1).astype(I32)
    m_max = n_t * _local_rows(tm)
    total = jnp.sum(sizes).reshape(1).astype(I32)

    xs = _dispatch(units, gdst, total, x1, lpr, m_max)
    ys = _experts(_expert_schedule(sizes, m_max), xs, w_e_gate, w_e_up, w_e_down)
    out = _combine(units, gdst, x1, rw, lpc, ln2_g, ln2_b, ys, alpha)
    return out.reshape(B, S, D)


def kernel(x, w_in, b_merge, conv_w, conv_b, w_rg_r, b_rg_r, w_rg_i, b_rg_i, lru_lambda, w_ret_o, w_lru_o, w_out, ln1_g, ln1_b, w_group, b_group, w_exp_router, b_exp_router, w_e_gate, w_e_up, w_e_down, ln2_g, ln2_b):
    depth = w_in.shape[0]
    for l in range(depth):
        x = _layer(x, depth, w_in[l], b_merge[l], conv_w[l], conv_b[l], w_rg_r[l], b_rg_r[l], w_rg_i[l],
                   b_rg_i[l], lru_lambda[l], w_ret_o[l], w_lru_o[l], w_out[l], ln1_g[l], ln1_b[l],
                   w_group[l], b_group[l], w_exp_router[l], b_exp_router[l], w_e_gate[l], w_e_up[l],
                   w_e_down[l], ln2_g[l], ln2_b[l])
    return x
```

```python
import functools

import jax
import jax.numpy as jnp
import numpy as np
from jax import lax
from jax.experimental import pallas as pl
from jax.experimental.pallas import tpu as pltpu

F32 = jnp.float32
BF16 = jnp.bfloat16
I32 = jnp.int32

D_MODEL = 1024
RET_HEADS = 4
RET_DK = 256
RET_DV = 512
RET_QK = RET_HEADS * RET_DK
RET_V = RET_HEADS * RET_DV
ROPE_BASE = 10000.0
LRU_WIDTH = 1536
LRU_BLOCKS = 8
LRU_BLOCK = LRU_WIDTH // LRU_BLOCKS
LRU_PAIR = 2 * LRU_BLOCK
LRU_PAIRS = LRU_BLOCKS // 2
CONV_WIDTH = 4
LRU_C = 8.0
N_GROUPS = 4
EXPERTS_PER_GROUP = 8
N_EXPERTS = N_GROUPS * EXPERTS_PER_GROUP
TOP_K = 2
D_EXPERT = 512
LN_EPS = 1e-5

LANES = 128
SUBLANES = 8
VMEM_LIMIT = 56 * 1024 * 1024

PROJ_TM = 512
PROJ_CHUNK = 512
RET_CHUNK = 256
LRU_T = 256
ROUTE_TM = 512
MOE_RT = 512
SEG_ALIGN = SUBLANES
RUN_BITS = (ROUTE_TM // SEG_ALIGN).bit_length()


def _cparams(sem):
    return pltpu.CompilerParams(dimension_semantics=sem, vmem_limit_bytes=VMEM_LIMIT)


def _proj_body(mode, *refs):
    if mode == "rot":
        x_ref, w_ref, cos_ref, sin_ref, o_ref = refs
    elif mode == "sigb":
        x_ref, w_ref, b_ref, o_ref = refs
    else:
        x_ref, w_ref, o_ref = refs
    xb = x_ref[...].astype(BF16)
    width = w_ref.shape[1]
    ch = RET_DK if mode == "rot" else min(PROJ_CHUNK, width)
    for jc in range(width // ch):
        cols = slice(jc * ch, (jc + 1) * ch)
        acc = jnp.dot(xb, w_ref[:, cols], preferred_element_type=F32)
        if mode == "rot":
            half = RET_DK // 2
            t1, t2 = acc[:, :half], acc[:, half:]
            c, s = cos_ref[...], sin_ref[...]
            scale = RET_DK ** -0.5 if jc >= RET_HEADS else 1.0
            o_ref[:, jc * ch:jc * ch + half] = ((t1 * c - t2 * s) * scale).astype(o_ref.dtype)
            o_ref[:, jc * ch + half:(jc + 1) * ch] = ((t1 * s + t2 * c) * scale).astype(o_ref.dtype)
        elif mode == "silu":
            o_ref[:, cols] = (acc * jax.nn.sigmoid(acc)).astype(o_ref.dtype)
        elif mode == "gelu":
            o_ref[:, cols] = jax.nn.gelu(acc).astype(o_ref.dtype)
        elif mode == "sigb":
            o_ref[:, cols] = jax.nn.sigmoid(acc + b_ref[:, cols]).astype(o_ref.dtype)
        else:
            o_ref[:, cols] = acc.astype(o_ref.dtype)


def _proj(x2, w, mode, seq, extra=(), out_dtype=BF16):
    n, d = x2.shape
    width = w.shape[1]
    tm = min(PROJ_TM, seq)
    in_specs = [pl.BlockSpec((tm, d), lambda i: (i, 0)),
                pl.BlockSpec((d, width), lambda i: (0, 0))]
    if mode == "rot":
        per_seq = seq // tm
        spec = pl.BlockSpec((tm, RET_DK // 2), lambda i: (i % per_seq, 0))
        in_specs += [spec, spec]
    elif mode == "sigb":
        in_specs += [pl.BlockSpec((1, width), lambda i: (0, 0))]
    return pl.pallas_call(
        functools.partial(_proj_body, mode),
        grid=(n // tm,),
        in_specs=in_specs,
        out_specs=pl.BlockSpec((tm, width), lambda i: (i, 0)),
        out_shape=jax.ShapeDtypeStruct((n, width), out_dtype),
        compiler_params=_cparams(("parallel",)),
        name="proj_" + mode,
    )(x2, w, *extra)


def _ret_body(dec_ref, q_ref, k_ref, v_ref, sg_ref, dm_ref, xi_ref, zeta_ref, o_ref, st_ref):
    c = pl.program_id(1)

    @pl.when(c == 0)
    def _():
        st_ref[...] = jnp.zeros_like(st_ref)

    for h in range(RET_HEADS):
        qc = slice(h * RET_DK, (h + 1) * RET_DK)
        vc = slice(h * RET_DV, (h + 1) * RET_DV)
        q, k, v = q_ref[:, qc], k_ref[:, qc], v_ref[:, vc]
        scores = lax.dot_general(q, k, (((1,), (1,)), ((), ())), preferred_element_type=F32) * dm_ref[h]
        inner = jnp.dot(scores.astype(BF16), v, preferred_element_type=F32)
        st = st_ref[h]
        cross = jnp.dot(q, st.astype(BF16), preferred_element_type=F32) * xi_ref[h]
        kz = (k.astype(F32) * zeta_ref[h]).astype(BF16)
        upd = lax.dot_general(kz, v, (((0,), (0,)), ((), ())), preferred_element_type=F32)
        st_ref[h] = st * dec_ref[h] + upd
        o = inner + cross
        mu = jnp.mean(o, axis=-1, keepdims=True)
        oc = o - mu
        var = jnp.mean(oc * oc, axis=-1, keepdims=True)
        on = oc * lax.rsqrt(var + LN_EPS)
        o_ref[:, vc] = (sg_ref[:, vc].astype(F32) * on).astype(o_ref.dtype)


def _retention(qk, v, sg, batch, seq):
    n = qk.shape[0]
    C = min(RET_CHUNK, seq)
    nc = seq // C
    H = RET_HEADS
    log_g = jnp.log1p(-(2.0 ** (-5.0 - jnp.arange(H, dtype=F32))))
    pos = jnp.arange(C, dtype=F32)
    diff = pos[:, None] - pos[None, :]
    causal = diff >= 0
    d_mask = jnp.where(causal[None], jnp.exp(log_g[:, None, None] * jnp.where(causal, diff, 0.0)[None]), 0.0)
    xi = jnp.exp(log_g[:, None] * (pos + 1.0)[None])[:, :, None]
    zeta = jnp.exp(log_g[:, None] * (C - 1.0 - pos)[None])[:, :, None]
    chunk_decay = jnp.exp(log_g * C)
    row = lambda b, c: b * nc + c
    full3 = lambda b, c: (0, 0, 0)
    return pl.pallas_call(
        _ret_body,
        grid=(batch, nc),
        in_specs=[
            pl.BlockSpec(memory_space=pltpu.SMEM),
            pl.BlockSpec((C, RET_QK), lambda b, c: (row(b, c), 0)),
            pl.BlockSpec((C, RET_QK), lambda b, c: (row(b, c), 1)),
            pl.BlockSpec((C, RET_V), lambda b, c: (row(b, c), 0)),
            pl.BlockSpec((C, RET_V), lambda b, c: (row(b, c), 0)),
            pl.BlockSpec((H, C, C), full3),
            pl.BlockSpec((H, C, 1), full3),
            pl.BlockSpec((H, C, 1), full3),
        ],
        out_specs=pl.BlockSpec((C, RET_V), lambda b, c: (row(b, c), 0)),
        out_shape=jax.ShapeDtypeStruct((n, RET_V), BF16),
        scratch_shapes=[pltpu.VMEM((H, RET_DK, RET_DV), F32)],
        compiler_params=_cparams(("parallel", "arbitrary")),
        name="retention",
    )(chunk_decay, qk, qk, v, sg, d_mask, xi, zeta)


def _lru_body(u_ref, g_ref, cw_ref, cb_ref, wr_ref, wi_ref, br_ref, bi_ref, lam_ref,
              o_ref, prev_ref, a_ref, b_ref, carry_ref):
    T = u_ref.shape[0]
    G = T // SUBLANES
    P = _seg_pitch(T)
    ncb = LRU_WIDTH // LANES
    t = pl.program_id(1)

    @pl.when(t == 0)
    def _():
        prev_ref[...] = jnp.zeros_like(prev_ref)
        carry_ref[...] = jnp.zeros_like(carry_ref)

    ub = u_ref[...]
    cw = cw_ref[...]
    rowi = lax.broadcasted_iota(I32, (T, T), 0)
    coli = lax.broadcasted_iota(I32, (T, T), 1)
    row8 = lax.broadcasted_iota(I32, (SUBLANES, 1), 0)
    prev = prev_ref[...]
    uc = cb_ref[...] + cw[CONV_WIDTH - 1:CONV_WIDTH, :] * ub.astype(F32)
    head = jnp.zeros((SUBLANES, LRU_WIDTH), F32)
    for back in range(1, CONV_WIDTH):
        tap = cw[CONV_WIDTH - 1 - back:CONV_WIDTH - back, :]
        shift = jnp.where(rowi - coli == back, 1.0, 0.0).astype(BF16)
        uc = uc + tap * jnp.dot(shift, ub, preferred_element_type=F32)
        head = head + tap * jnp.where(row8 < back, pltpu.roll(prev, back, 0), 0.0)
    uc = jnp.concatenate([uc[:SUBLANES] + head, uc[SUBLANES:]], axis=0)
    prev_ref[...] = u_ref[T - 2 * SUBLANES:, :].astype(F32)[SUBLANES:]

    ucb = uc.astype(BF16)
    neg_c_sp = -LRU_C * jax.nn.softplus(-lam_ref[...])
    for p in range(LRU_PAIRS):
        lo = p * LRU_PAIR
        sl = ucb[:, lo:lo + LRU_PAIR]
        r = jax.nn.sigmoid(jnp.dot(sl, wr_ref[p], preferred_element_type=F32) + br_ref[:, lo:lo + LRU_PAIR])
        i = jax.nn.sigmoid(jnp.dot(sl, wi_ref[p], preferred_element_type=F32) + bi_ref[:, lo:lo + LRU_PAIR])
        log_a = r * neg_c_sp[:, lo:lo + LRU_PAIR]
        a = jnp.exp(log_a)
        inp = jnp.sqrt(-jnp.tanh(log_a) * (a * a + 1.0)) * (i * uc[:, lo:lo + LRU_PAIR])
        for cc in range(LRU_PAIR // LANES):
            cb = p * (LRU_PAIR // LANES) + cc
            for s in range(SUBLANES):
                a_ref[cb, s * P:s * P + G, :] = a[s * G:(s + 1) * G, cc * LANES:(cc + 1) * LANES]
                b_ref[cb, s * P:s * P + G, :] = inp[s * G:(s + 1) * G, cc * LANES:(cc + 1) * LANES]

    def step(j, hp):
        hs, ps = hp
        nh, npr = [], []
        for cb in range(ncb):
            a = a_ref[cb, pl.ds(j, SUBLANES, stride=P), :]
            b = b_ref[cb, pl.ds(j, SUBLANES, stride=P), :]
            hn = a * hs[cb] + b
            pn = a * ps[cb]
            b_ref[cb, pl.ds(j, SUBLANES, stride=P), :] = hn
            a_ref[cb, pl.ds(j, SUBLANES, stride=P), :] = pn
            nh.append(hn)
            npr.append(pn)
        return tuple(nh), tuple(npr)

    zeros = tuple(jnp.zeros((SUBLANES, LANES), F32) for _ in range(ncb))
    ones = tuple(jnp.ones((SUBLANES, LANES), F32) for _ in range(ncb))
    h_end, p_end = lax.fori_loop(0, G, step, (zeros, ones))

    for cb in range(ncb):
        cin = carry_ref[:, cb * LANES:(cb + 1) * LANES]
        for s in range(SUBLANES):
            rows = slice(s * G, (s + 1) * G)
            seg = slice(s * P, s * P + G)
            hseg = b_ref[cb, seg, :] + a_ref[cb, seg, :] * cin
            gate = g_ref[rows, cb * LANES:(cb + 1) * LANES].astype(F32)
            o_ref[rows, cb * LANES:(cb + 1) * LANES] = (gate * hseg).astype(o_ref.dtype)
            cin = h_end[cb][s:s + 1, :] + p_end[cb][s:s + 1, :] * cin
        carry_ref[:, cb * LANES:(cb + 1) * LANES] = cin


def _seg_pitch(t):
    g = t // SUBLANES
    units = -(-g // SUBLANES)
    return SUBLANES * (units + 1 - units % 2)


def _rglru(u, gate, conv_w, conv_b, w_r, b_r, w_i, b_i, lam, batch, seq):
    n = u.shape[0]
    T = min(LRU_T, seq)
    nt = seq // T
    ncb = LRU_WIDTH // LANES

    def pairs(w):
        z = jnp.zeros((LRU_PAIRS, LRU_PAIR, LRU_PAIR), F32)
        w4 = w.reshape(LRU_PAIRS, 2, LRU_BLOCK, LRU_BLOCK)
        z = z.at[:, :LRU_BLOCK, :LRU_BLOCK].set(w4[:, 0])
        z = z.at[:, LRU_BLOCK:, LRU_BLOCK:].set(w4[:, 1])
        return z.astype(BF16)

    row = lambda b, t: (b * nt + t, 0)
    full2 = lambda b, t: (0, 0)
    full3 = lambda b, t: (0, 0, 0)
    return pl.pallas_call(
        _lru_body,
        grid=(batch, nt),
        in_specs=[
            pl.BlockSpec((T, LRU_WIDTH), row),
            pl.BlockSpec((T, LRU_WIDTH), row),
            pl.BlockSpec((CONV_WIDTH, LRU_WIDTH), full2),
            pl.BlockSpec((1, LRU_WIDTH), full2),
            pl.BlockSpec((LRU_PAIRS, LRU_PAIR, LRU_PAIR), full3),
            pl.BlockSpec((LRU_PAIRS, LRU_PAIR, LRU_PAIR), full3),
            pl.BlockSpec((1, LRU_WIDTH), full2),
            pl.BlockSpec((1, LRU_WIDTH), full2),
            pl.BlockSpec((1, LRU_WIDTH), full2),
        ],
        out_specs=pl.BlockSpec((T, LRU_WIDTH), row),
        out_shape=jax.ShapeDtypeStruct((n, LRU_WIDTH), BF16),
        scratch_shapes=[
            pltpu.VMEM((SUBLANES, LRU_WIDTH), F32),
            pltpu.VMEM((ncb, SUBLANES * _seg_pitch(T), LANES), F32),
            pltpu.VMEM((ncb, SUBLANES * _seg_pitch(T), LANES), F32),
            pltpu.VMEM((1, LRU_WIDTH), F32),
        ],
        compiler_params=_cparams(("parallel", "arbitrary")),
        name="rglru",
    )(u, gate, conv_w, conv_b.reshape(1, -1), pairs(w_r), pairs(w_i),
      b_r.reshape(1, -1), b_i.reshape(1, -1), lam.reshape(1, -1))


def _layer_norm_rows(y, g, b):
    mu = jnp.mean(y, axis=-1, keepdims=True)
    yc = y - mu
    var = jnp.mean(yc * yc, axis=-1, keepdims=True)
    return yc * lax.rsqrt(var + LN_EPS) * g + b


def _merge_body(alpha, ret_ref, lru_ref, gm_ref, x_ref, wro_ref, wlo_ref, wo_ref, g1_ref, b1_ref,
                wrt_ref, brt_ref, x1_ref, rw_ref, lpc_ref, lpr_ref, cnt_ref):
    pr = jnp.dot(ret_ref[...], wro_ref[...], preferred_element_type=F32)
    pu = jnp.dot(lru_ref[...], wlo_ref[...], preferred_element_type=F32)
    merged = gm_ref[:, :D_MODEL].astype(F32) * pr + gm_ref[:, D_MODEL:].astype(F32) * pu
    y = alpha * x_ref[...] + jnp.dot(merged.astype(BF16), wo_ref[...], preferred_element_type=F32)
    x1 = _layer_norm_rows(y, g1_ref[...], b1_ref[...])
    x1_ref[...] = x1

    tm = x1.shape[0]
    x_hi = x1.astype(BF16)
    x_lo = (x1 - x_hi.astype(F32)).astype(BF16)
    hh = jnp.dot(x_hi, wrt_ref[...], preferred_element_type=F32)
    lh = jnp.dot(x_lo, wrt_ref[:, :LANES], preferred_element_type=F32)
    lg = hh[:, :LANES] + hh[:, LANES:] + lh + brt_ref[...]
    lane = lax.broadcasted_iota(I32, (tm, LANES), 1)
    big = jnp.int32(LANES)
    neg = jnp.float32(-jnp.inf)
    gmask = lane < N_GROUPS
    gl = jnp.where(gmask, lg, neg)
    gmax = jnp.max(gl, axis=-1, keepdims=True)
    g_idx = jnp.min(jnp.where(gmask & (gl == gmax), lane, big), axis=-1, keepdims=True)
    g_w = 1.0 / jnp.sum(jnp.where(gmask, jnp.exp(gl - gmax), 0.0), axis=-1, keepdims=True)
    e_lo = N_GROUPS + EXPERTS_PER_GROUP * g_idx
    emask = (lane >= e_lo) & (lane < e_lo + EXPERTS_PER_GROUP)
    el = jnp.where(emask, lg, neg)
    v1 = jnp.max(el, axis=-1, keepdims=True)
    i1 = jnp.min(jnp.where(emask & (el == v1), lane, big), axis=-1, keepdims=True)
    emask2 = emask & (lane != i1)
    el2 = jnp.where(emask2, lg, neg)
    v2 = jnp.max(el2, axis=-1, keepdims=True)
    i2 = jnp.min(jnp.where(emask2 & (el2 == v2), lane, big), axis=-1, keepdims=True)
    ex = jnp.exp(v2 - v1)
    den = 1.0 + ex
    w1 = g_w / den
    w2 = g_w * ex / den
    e1 = i1 - N_GROUPS
    e2 = i2 - N_GROUPS
    rw_ref[...] = jnp.where(lane == 0, w1, jnp.where(lane == 1, w2, 0.0))

    oh = (lane == e1).astype(F32) + (lane == e2).astype(F32)
    rowi = lax.broadcasted_iota(I32, (tm, tm), 0)
    coli = lax.broadcasted_iota(I32, (tm, tm), 1)
    tri = jnp.where(coli < rowi, 1.0, 0.0).astype(BF16)
    before = jnp.dot(tri, oh.astype(BF16), preferred_element_type=F32)
    cnt = jnp.sum(oh, axis=0, keepdims=True)
    units = jnp.floor((cnt + (SEG_ALIGN - 1.0)) * (1.0 / SEG_ALIGN))
    er = lax.broadcasted_iota(I32, (LANES, LANES), 0)
    ec = lax.broadcasted_iota(I32, (LANES, LANES), 1)
    upper = jnp.where(er < ec, 1.0, 0.0).astype(BF16)
    offs = SEG_ALIGN * jnp.dot(jnp.broadcast_to(units, (SUBLANES, LANES)).astype(BF16), upper,
                               preferred_element_type=F32)[0:1, :]
    pos = before + offs
    lp1 = jnp.sum(jnp.where(lane == e1, pos, 0.0), axis=-1, keepdims=True)
    lp2 = jnp.sum(jnp.where(lane == e2, pos, 0.0), axis=-1, keepdims=True)
    lpc = jnp.where(lane == 0, lp1, jnp.where(lane == 1, lp2, 0.0))
    lpc_ref[...] = lpc
    lpr_ref[...] = lpc.T[0:SUBLANES, :]
    cnt_ref[...] = jnp.broadcast_to(cnt, cnt_ref.shape)


def _merge(ret, lru, gm, x2, w_ret_o, w_lru_o, w_out, ln_g, ln_b, w_rt, b_rt, alpha):
    n = x2.shape[0]
    tm = min(ROUTE_TM, n)
    rowmap = lambda i: (i, 0)
    full = lambda i: (0, 0)
    return pl.pallas_call(
        functools.partial(_merge_body, alpha),
        grid=(n // tm,),
        in_specs=[
            pl.BlockSpec((tm, RET_V), rowmap),
            pl.BlockSpec((tm, LRU_WIDTH), rowmap),
            pl.BlockSpec((tm, 2 * D_MODEL), rowmap),
            pl.BlockSpec((tm, D_MODEL), rowmap),
            pl.BlockSpec((RET_V, D_MODEL), full),
            pl.BlockSpec((LRU_WIDTH, D_MODEL), full),
            pl.BlockSpec((D_MODEL, D_MODEL), full),
            pl.BlockSpec((1, D_MODEL), full),
            pl.BlockSpec((1, D_MODEL), full),
            pl.BlockSpec((D_MODEL, 2 * LANES), full),
            pl.BlockSpec((1, LANES), full),
        ],
        out_specs=[
            pl.BlockSpec((tm, D_MODEL), rowmap),
            pl.BlockSpec((tm, LANES), rowmap),
            pl.BlockSpec((tm, LANES), rowmap),
            pl.BlockSpec((SUBLANES, tm), lambda i: (0, i)),
            pl.BlockSpec((None, SUBLANES, LANES), lambda i: (i, 0, 0)),
        ],
        out_shape=[
            jax.ShapeDtypeStruct((n, D_MODEL), F32),
            jax.ShapeDtypeStruct((n, LANES), F32),
            jax.ShapeDtypeStruct((n, LANES), F32),
            jax.ShapeDtypeStruct((SUBLANES, n), F32),
            jax.ShapeDtypeStruct((n // tm, SUBLANES, LANES), F32),
        ],
        compiler_params=_cparams(("parallel",)),
        name="merge_ln_route",
    )(ret, lru, gm, x2, w_ret_o, w_lru_o, w_out, ln_g.reshape(1, -1), ln_b.reshape(1, -1), w_rt, b_rt)


def _run_lists(units, run_off, gdst):
    k = jnp.arange(N_EXPERTS, dtype=I32)
    cnts, offs, dsts = [], [], []
    for b in range(RUN_BITS):
        bit = (units >> b) & 1
        low = (units & ((1 << b) - 1)) * SEG_ALIGN
        pos = jnp.cumsum(bit, axis=1) - bit
        hit = (bit[:, None, :] == 1) & (pos[:, None, :] == k[None, :, None])
        offs.append(jnp.sum(jnp.where(hit, (run_off + low)[:, None, :], 0), axis=2))
        dsts.append(jnp.sum(jnp.where(hit, (gdst + low)[:, None, :], 0), axis=2))
        cnts.append(jnp.sum(bit, axis=1))
    flat = lambda parts: jnp.stack(parts, axis=1).reshape(-1).astype(I32)
    return flat(cnts), flat(offs), flat(dsts)


def _run_copies(lists, tile, loc_ref, glob_hbm, sem, to_global, wait):
    cnt_ref, off_ref, dst_ref = lists
    for b in range(RUN_BITS):
        rows = SEG_ALIGN << b
        base = tile * RUN_BITS + b

        def piece(k, carry):
            off = pl.multiple_of(off_ref[base * N_EXPERTS + k], SEG_ALIGN)
            dst = pl.multiple_of(dst_ref[base * N_EXPERTS + k], SEG_ALIGN)
            l = loc_ref.at[pl.ds(off, rows), :]
            g = glob_hbm.at[pl.ds(dst, rows), :]
            cp = pltpu.make_async_copy(l, g, sem) if to_global else pltpu.make_async_copy(g, l, sem)
            if wait:
                cp.wait()
            else:
                cp.start()
            return carry

        lax.fori_loop(0, cnt_ref[base], piece, 0)


def _zero_rows(start, units, max_units, zero_ref, xs_hbm, sem, wait):
    pos = start
    for b in range((max_units - 1).bit_length()):
        rows = SEG_ALIGN << b
        bit = lax.bitwise_and(lax.shift_right_logical(units, b), 1)

        @pl.when(bit == 1)
        def _():
            dst = xs_hbm.at[pl.ds(pl.multiple_of(pos, SEG_ALIGN), rows), :]
            cp = pltpu.make_async_copy(zero_ref.at[pl.ds(0, rows), :], dst, sem)
            if wait:
                cp.wait()
            else:
                cp.start()

        pos = pos + bit * rows


def _onehot_rows(lpr_ref, n_rows):
    tm = lpr_ref.shape[1]
    sub = lax.broadcasted_iota(I32, (n_rows, tm), 0)
    lp1 = lpr_ref[0:1, :].astype(I32)
    lp2 = lpr_ref[1:2, :].astype(I32)
    return jnp.where((sub == lp1) | (sub == lp2), 1.0, 0.0).astype(BF16)


def _dispatch_body(cnt_ref, off_ref, dst_ref, pad_start_ref, pad_units_ref, total_ref, x1_ref, lpr_ref,
                   xs_hbm, loc_ref, zero_ref, sems, zsem):
    i = pl.program_id(0)
    last = pl.num_programs(0) - 1
    slot = lax.rem(i, 2)
    lists = (cnt_ref, off_ref, dst_ref)
    perm = _onehot_rows(lpr_ref, loc_ref.shape[1])
    loc_ref[slot] = jnp.dot(perm, x1_ref[...].astype(BF16), preferred_element_type=F32)
    _run_copies(lists, i, loc_ref.at[slot], xs_hbm, sems.at[slot], True, False)

    @pl.when(i > 0)
    def _():
        _run_copies(lists, i - 1, loc_ref.at[1 - slot], xs_hbm, sems.at[1 - slot], True, True)

    @pl.when(i == last)
    def _():
        zero_ref[...] = jnp.zeros_like(zero_ref)
        max_units = zero_ref.shape[0] // SEG_ALIGN
        for wait in (False, True):
            def region(e, carry):
                _zero_rows(pad_start_ref[e], pad_units_ref[e], max_units, zero_ref, xs_hbm, zsem, wait)
                return carry
            lax.fori_loop(0, N_EXPERTS, region, 0)
            _zero_tail(total_ref[0], zero_ref, xs_hbm, zsem, wait)
        _run_copies(lists, i, loc_ref.at[slot], xs_hbm, sems.at[slot], True, True)


def _zero_tail(total, zero_ref, xs_hbm, sem, wait):
    zr = zero_ref.shape[0]
    shift = zr.bit_length() - 1
    assert zr == 1 << shift and xs_hbm.shape[0] % SEG_ALIGN == 0
    dead = xs_hbm.shape[0] - total
    n_full = lax.shift_right_logical(dead, shift)

    def full(k, carry):
        dst = xs_hbm.at[pl.ds(pl.multiple_of(total + k * zr, SEG_ALIGN), zr), :]
        cp = pltpu.make_async_copy(zero_ref, dst, sem)
        if wait:
            cp.wait()
        else:
            cp.start()
        return carry

    lax.fori_loop(0, n_full, full, 0)
    rem = lax.shift_right_logical(dead - n_full * zr, SEG_ALIGN.bit_length() - 1)
    _zero_rows(total + n_full * zr, rem, zr // SEG_ALIGN, zero_ref, xs_hbm, sem, wait)


def _local_rows(tm):
    return TOP_K * tm + N_EXPERTS * SEG_ALIGN


def _dispatch(lists, pad_start, pad_units, total, x1, lpr, m_max):
    n = x1.shape[0]
    tm = min(ROUTE_TM, n)
    return pl.pallas_call(
        _dispatch_body,
        grid_spec=pltpu.PrefetchScalarGridSpec(
            num_scalar_prefetch=6,
            grid=(n // tm,),
            in_specs=[pl.BlockSpec((tm, D_MODEL), lambda i, *_: (i, 0)),
                      pl.BlockSpec((SUBLANES, tm), lambda i, *_: (0, i))],
            out_specs=pl.BlockSpec(memory_space=pl.ANY),
            scratch_shapes=[pltpu.VMEM((2, _local_rows(tm), D_MODEL), F32),
                            pltpu.VMEM((MOE_RT, D_MODEL), F32),
                            pltpu.SemaphoreType.DMA((2,)), pltpu.SemaphoreType.DMA],
        ),
        out_shape=jax.ShapeDtypeStruct((m_max, D_MODEL), F32),
        compiler_params=_cparams(("arbitrary",)),
        name="dispatch",
    )(*lists, pad_start, pad_units, total, x1, lpr)


def _expert_body(exp_ref, live_ref, xs_ref, wg_ref, wu_ref, wd_ref, y_ref, wgb_ref, wub_ref, wdb_ref, cur_ref):
    g = pl.program_id(0)
    e = exp_ref[g]

    @pl.when(g == 0)
    def _():
        cur_ref[0] = -1

    @pl.when(g < live_ref[0])
    def _():
        @pl.when(cur_ref[0] != e)
        def _():
            wgb_ref[...] = wg_ref[...].astype(BF16)
            wub_ref[...] = wu_ref[...].astype(BF16)
            wdb_ref[...] = wd_ref[...].astype(BF16)
            cur_ref[0] = e

        xb = xs_ref[...].astype(BF16)
        hg = jnp.dot(xb, wgb_ref[...], preferred_element_type=F32)
        hu = jnp.dot(xb, wub_ref[...], preferred_element_type=F32)
        hm = (hg * jax.nn.sigmoid(hg) * hu).astype(BF16)
        y_ref[...] = jnp.dot(hm, wdb_ref[...], preferred_element_type=F32)

    @pl.when(g >= live_ref[0])
    def _():
        y_ref[...] = jnp.zeros_like(y_ref)


def _experts(tile_expert, n_live, xs, w_gate, w_up, w_down):
    m = xs.shape[0]
    rt = min(MOE_RT, m)
    return pl.pallas_call(
        _expert_body,
        grid_spec=pltpu.PrefetchScalarGridSpec(
            num_scalar_prefetch=2,
            grid=(m // rt,),
            in_specs=[
                pl.BlockSpec((rt, D_MODEL), lambda g, e, n: (g, 0)),
                pl.BlockSpec((None, D_MODEL, D_EXPERT), lambda g, e, n: (e[g], 0, 0)),
                pl.BlockSpec((None, D_MODEL, D_EXPERT), lambda g, e, n: (e[g], 0, 0)),
                pl.BlockSpec((None, D_EXPERT, D_MODEL), lambda g, e, n: (e[g], 0, 0)),
            ],
            out_specs=pl.BlockSpec((rt, D_MODEL), lambda g, e, n: (g, 0)),
            scratch_shapes=[
                pltpu.VMEM((D_MODEL, D_EXPERT), BF16),
                pltpu.VMEM((D_MODEL, D_EXPERT), BF16),
                pltpu.VMEM((D_EXPERT, D_MODEL), BF16),
                pltpu.SMEM((1,), I32),
            ],
        ),
        out_shape=jax.ShapeDtypeStruct((m, D_MODEL), F32),
        compiler_params=_cparams(("arbitrary",)),
        name="experts",
    )(tile_expert, n_live, xs, w_gate, w_up, w_down)


def _combine_body(alpha, cnt_ref, off_ref, dst_ref, x1_ref, rw_ref, lpc_ref, g2_ref, b2_ref, ys_hbm, o_ref,
                  loc_ref, sems):
    i = pl.program_id(0)
    slot = lax.rem(i, 2)
    lists = (cnt_ref, off_ref, dst_ref)

    @pl.when(i == 0)
    def _():
        loc_ref[...] = jnp.zeros_like(loc_ref)
        _run_copies(lists, i, loc_ref.at[slot], ys_hbm, sems.at[slot], False, False)

    @pl.when(i + 1 < pl.num_programs(0))
    def _():
        _run_copies(lists, i + 1, loc_ref.at[1 - slot], ys_hbm, sems.at[1 - slot], False, False)

    _run_copies(lists, i, loc_ref.at[slot], ys_hbm, sems.at[slot], False, True)
    tm = x1_ref.shape[0]
    n_rows = loc_ref.shape[1]
    yb = loc_ref[slot].astype(BF16)
    lanes = lax.broadcasted_iota(I32, (tm, n_rows), 1)
    sel = (jnp.where(lanes == lpc_ref[:, 0:1].astype(I32), rw_ref[:, 0:1], 0.0)
           + jnp.where(lanes == lpc_ref[:, 1:2].astype(I32), rw_ref[:, 1:2], 0.0))
    moe = jnp.dot(sel.astype(BF16), yb, preferred_element_type=F32)
    y = alpha * x1_ref[...] + moe
    o_ref[...] = _layer_norm_rows(y, g2_ref[...], b2_ref[...])


def _combine(lists, x1, rw, lpc, ln_g, ln_b, ys, alpha):
    n = x1.shape[0]
    tm = min(ROUTE_TM, n)
    rowmap = lambda i, *_: (i, 0)
    full = lambda i, *_: (0, 0)
    return pl.pallas_call(
        functools.partial(_combine_body, alpha),
        grid_spec=pltpu.PrefetchScalarGridSpec(
            num_scalar_prefetch=3,
            grid=(n // tm,),
            in_specs=[
                pl.BlockSpec((tm, D_MODEL), rowmap),
                pl.BlockSpec((tm, LANES), rowmap),
                pl.BlockSpec((tm, LANES), rowmap),
                pl.BlockSpec((1, D_MODEL), full),
                pl.BlockSpec((1, D_MODEL), full),
                pl.BlockSpec(memory_space=pl.ANY),
            ],
            out_specs=pl.BlockSpec((tm, D_MODEL), rowmap),
            scratch_shapes=[pltpu.VMEM((2, _local_rows(tm), D_MODEL), F32), pltpu.SemaphoreType.DMA((2,))],
        ),
        out_shape=jax.ShapeDtypeStruct((n, D_MODEL), F32),
        compiler_params=_cparams(("arbitrary",)),
        name="combine_ln",
    )(*lists, x1, rw, lpc, ln_g.reshape(1, -1), ln_b.reshape(1, -1), ys)


def _rotary_tables(seq):
    half = RET_DK // 2
    inv = ROPE_BASE ** (-jnp.arange(half, dtype=F32) / half)
    ang = jnp.arange(seq, dtype=F32)[:, None] * inv[None, :]
    return jnp.cos(ang), jnp.sin(ang)


def _router_weights(w_group, b_group, w_exp_router, b_exp_router):
    d = w_group.shape[0]
    w = jnp.zeros((d, LANES), F32).at[:, :N_GROUPS].set(w_group).at[:, N_GROUPS:N_GROUPS + N_EXPERTS].set(w_exp_router)
    b = jnp.zeros((1, LANES), F32).at[0, :N_GROUPS].set(b_group).at[0, N_GROUPS:N_GROUPS + N_EXPERTS].set(b_exp_router)
    w_hi = w.astype(BF16)
    w_lo = (w - w_hi.astype(F32)).astype(BF16)
    return jnp.concatenate([w_hi, w_lo], axis=1), b


def _layer(x, depth, w_in, b_merge, conv_w, conv_b, w_rg_r, b_rg_r, w_rg_i, b_rg_i, lru_lambda,
           w_ret_o, w_lru_o, w_out, ln1_g, ln1_b, w_group, b_group, w_exp_router, b_exp_router,
           w_e_gate, w_e_up, w_e_down, ln2_g, ln2_b):
    B, S, D = x.shape
    n = B * S
    alpha = (2.0 * depth) ** 0.25
    x2 = x.reshape(n, D)
    wb = w_in.astype(BF16)
    o_k, o_v, o_g, o_u, o_gl, o_gm = np.cumsum([RET_QK, RET_QK, RET_V, RET_V, LRU_WIDTH, LRU_WIDTH])
    cos, sin = _rotary_tables(S)

    qk = _proj(x2, wb[:, :o_v], "rot", S, (cos, sin))
    v = _proj(x2, wb[:, o_v:o_g], "plain", S)
    sg = _proj(x2, wb[:, o_g:o_u], "silu", S)
    u = _proj(x2, wb[:, o_u:o_gl], "plain", S)
    gl = _proj(x2, wb[:, o_gl:o_gm], "gelu", S)
    gm = _proj(x2, wb[:, o_gm:], "sigb", S, (b_merge.reshape(1, -1),))

    ret = _retention(qk, v, sg, B, S)
    lru = _rglru(u, gl, conv_w, conv_b, w_rg_r, b_rg_r, w_rg_i, b_rg_i, lru_lambda, B, S)

    w_rt, b_rt = _router_weights(w_group, b_group, w_exp_router, b_exp_router)
    x1, rw, lpc, lpr, tcnt = _merge(ret, lru, gm, x2, w_ret_o.astype(BF16), w_lru_o.astype(BF16),
                                    w_out.astype(BF16), ln1_g, ln1_b, w_rt, b_rt, alpha)

    tm = min(ROUTE_TM, n)
    n_t = n // tm
    m_max = n_t * _local_rows(tm) + N_EXPERTS * MOE_RT
    rt = min(MOE_RT, m_max)
    cnt = tcnt[:, 0, :N_EXPERTS].astype(I32)
    units = (cnt + (SEG_ALIGN - 1)) // SEG_ALIGN
    run = units * SEG_ALIGN
    sizes = jnp.sum(run, axis=0)
    region = (sizes + (rt - 1)) // rt * rt
    e_end = jnp.cumsum(region)
    e_start = e_end - region
    gdst = e_start[None, :] + jnp.cumsum(run, axis=0) - run
    run_off = jnp.cumsum(run, axis=1) - run
    lists = _run_lists(units, run_off, gdst)
    pad_start = (e_start + sizes).astype(I32)
    pad_units = ((region - sizes) // SEG_ALIGN).astype(I32)
    total = e_end[-1:].astype(I32)
    tile_start = jnp.arange(m_max // rt, dtype=I32) * rt
    n_live = total // rt
    tile_expert = jnp.sum((e_end[None, :] <= jnp.minimum(tile_start, total - rt)[:, None]).astype(I32), axis=1)

    xs = _dispatch(lists, pad_start, pad_units, total, x1, lpr, m_max)
    ys = _experts(tile_expert, n_live, xs, w_e_gate, w_e_up, w_e_down)
    out = _combine(lists, x1, rw, lpc, ln2_g, ln2_b, ys, alpha)
    return out.reshape(B, S, D)


def kernel(x, w_in, b_merge, conv_w, conv_b, w_rg_r, b_rg_r, w_rg_i, b_rg_i, lru_lambda, w_ret_o, w_lru_o, w_out, ln1_g, ln1_b, w_group, b_group, w_exp_router, b_exp_router, w_e_gate, w_e_up, w_e_down, ln2_g, ln2_b):
    depth = w_in.shape[0]
    for l in range(depth):
        x = _layer(x, depth, w_in[l], b_merge[l], conv_w[l], conv_b[l], w_rg_r[l], b_rg_r[l], w_rg_i[l],
                   b_rg_i[l], lru_lambda[l], w_ret_o[l], w_lru_o[l], w_out[l], ln1_g[l], ln1_b[l],
                   w_group[l], b_group[l], w_exp_router[l], b_exp_router[l], w_e_gate[l], w_e_up[l],
                   w_e_down[l], ln2_g[l], ln2_b[l])
    return x
```

```python
import functools

import jax
import jax.numpy as jnp
import numpy as np
from jax import lax
from jax.experimental import pallas as pl
from jax.experimental.pallas import tpu as pltpu

F32 = jnp.float32
BF16 = jnp.bfloat16
I32 = jnp.int32

D_MODEL = 1024
RET_HEADS = 4
RET_DK = 256
RET_DV = 512
RET_QK = RET_HEADS * RET_DK
RET_V = RET_HEADS * RET_DV
ROPE_BASE = 10000.0
LRU_WIDTH = 1536
LRU_BLOCKS = 8
LRU_BLOCK = LRU_WIDTH // LRU_BLOCKS
LRU_PAIR = 2 * LRU_BLOCK
LRU_PAIRS = LRU_BLOCKS // 2
CONV_WIDTH = 4
LRU_C = 8.0
N_GROUPS = 4
EXPERTS_PER_GROUP = 8
N_EXPERTS = N_GROUPS * EXPERTS_PER_GROUP
TOP_K = 2
D_EXPERT = 512
LN_EPS = 1e-5

LANES = 128
SUBLANES = 8
VMEM_LIMIT = 56 * 1024 * 1024

PROJ_TM = 512
PROJ_CHUNK = 512
RET_CHUNK = 256
LRU_T = 512
ROUTE_TM = 512
MOE_RT = 512
SEG_ALIGN = SUBLANES
RUN_BITS = (ROUTE_TM // SEG_ALIGN).bit_length()


def _cparams(sem):
    return pltpu.CompilerParams(dimension_semantics=sem, vmem_limit_bytes=VMEM_LIMIT)


PROJ_SEGMENTS = (("rot", 2 * RET_QK), ("plain", RET_V), ("silu", RET_V),
                 ("plain", LRU_WIDTH), ("gelu", LRU_WIDTH), ("sigb", 2 * D_MODEL))


def _proj_body(x_ref, w_ref, cos_ref, sin_ref, b_ref, *o_refs):
    xb = x_ref[...].astype(BF16)
    col = 0
    for (mode, width), o_ref in zip(PROJ_SEGMENTS, o_refs):
        ch = RET_DK if mode == "rot" else PROJ_CHUNK
        for jc in range(width // ch):
            cols = slice(jc * ch, (jc + 1) * ch)
            acc = jnp.dot(xb, w_ref[:, col + jc * ch:col + (jc + 1) * ch], preferred_element_type=F32)
            if mode == "rot":
                half = RET_DK // 2
                t1, t2 = acc[:, :half], acc[:, half:]
                c, s = cos_ref[...], sin_ref[...]
                scale = RET_DK ** -0.5 if jc >= RET_HEADS else 1.0
                o_ref[:, jc * ch:jc * ch + half] = ((t1 * c - t2 * s) * scale).astype(o_ref.dtype)
                o_ref[:, jc * ch + half:(jc + 1) * ch] = ((t1 * s + t2 * c) * scale).astype(o_ref.dtype)
            elif mode == "silu":
                o_ref[:, cols] = (acc * jax.nn.sigmoid(acc)).astype(o_ref.dtype)
            elif mode == "gelu":
                o_ref[:, cols] = jax.nn.gelu(acc).astype(o_ref.dtype)
            elif mode == "sigb":
                o_ref[:, cols] = jax.nn.sigmoid(acc + b_ref[:, cols]).astype(o_ref.dtype)
            else:
                o_ref[:, cols] = acc.astype(o_ref.dtype)
        col += width


def _proj(x2, w, cos, sin, b_merge, seq):
    n, d = x2.shape
    tm = min(PROJ_TM, seq)
    per_seq = seq // tm
    widths = [width for _, width in PROJ_SEGMENTS]
    assert sum(widths) == w.shape[1]
    rot = pl.BlockSpec((tm, RET_DK // 2), lambda i: (i % per_seq, 0))
    return pl.pallas_call(
        _proj_body,
        grid=(n // tm,),
        in_specs=[pl.BlockSpec((tm, d), lambda i: (i, 0)),
                  pl.BlockSpec(w.shape, lambda i: (0, 0), pipeline_mode=pl.Buffered(1)),
                  rot, rot,
                  pl.BlockSpec((1, b_merge.shape[1]), lambda i: (0, 0))],
        out_specs=[pl.BlockSpec((tm, width), lambda i: (i, 0)) for width in widths],
        out_shape=[jax.ShapeDtypeStruct((n, width), BF16) for width in widths],
        compiler_params=_cparams(("parallel",)),
        name="proj",
    )(x2, w, cos, sin, b_merge)


def _ret_body(dec_ref, q_ref, k_ref, v_ref, sg_ref, dm_ref, xi_ref, zeta_ref, o_ref, st_ref):
    c = pl.program_id(1)

    @pl.when(c == 0)
    def _():
        st_ref[...] = jnp.zeros_like(st_ref)

    for h in range(RET_HEADS):
        qc = slice(h * RET_DK, (h + 1) * RET_DK)
        vc = slice(h * RET_DV, (h + 1) * RET_DV)
        q, k, v = q_ref[:, qc], k_ref[:, qc], v_ref[:, vc]
        scores = lax.dot_general(q, k, (((1,), (1,)), ((), ())), preferred_element_type=F32) * dm_ref[h]
        inner = jnp.dot(scores.astype(BF16), v, preferred_element_type=F32)
        st = st_ref[h]
        cross = jnp.dot(q, st.astype(BF16), preferred_element_type=F32) * xi_ref[h]
        kz = (k.astype(F32) * zeta_ref[h]).astype(BF16)
        upd = lax.dot_general(kz, v, (((0,), (0,)), ((), ())), preferred_element_type=F32)
        st_ref[h] = st * dec_ref[h] + upd
        o = inner + cross
        mu = jnp.mean(o, axis=-1, keepdims=True)
        oc = o - mu
        var = jnp.mean(oc * oc, axis=-1, keepdims=True)
        on = oc * lax.rsqrt(var + LN_EPS)
        o_ref[:, vc] = (sg_ref[:, vc].astype(F32) * on).astype(o_ref.dtype)


def _retention(qk, v, sg, batch, seq):
    n = qk.shape[0]
    C = min(RET_CHUNK, seq)
    nc = seq // C
    H = RET_HEADS
    log_g = jnp.log1p(-(2.0 ** (-5.0 - jnp.arange(H, dtype=F32))))
    pos = jnp.arange(C, dtype=F32)
    diff = pos[:, None] - pos[None, :]
    causal = diff >= 0
    d_mask = jnp.where(causal[None], jnp.exp(log_g[:, None, None] * jnp.where(causal, diff, 0.0)[None]), 0.0)
    xi = jnp.exp(log_g[:, None] * (pos + 1.0)[None])[:, :, None]
    zeta = jnp.exp(log_g[:, None] * (C - 1.0 - pos)[None])[:, :, None]
    chunk_decay = jnp.exp(log_g * C)
    row = lambda b, c: b * nc + c
    full3 = lambda b, c: (0, 0, 0)
    return pl.pallas_call(
        _ret_body,
        grid=(batch, nc),
        in_specs=[
            pl.BlockSpec(memory_space=pltpu.SMEM),
            pl.BlockSpec((C, RET_QK), lambda b, c: (row(b, c), 0)),
            pl.BlockSpec((C, RET_QK), lambda b, c: (row(b, c), 1)),
            pl.BlockSpec((C, RET_V), lambda b, c: (row(b, c), 0)),
            pl.BlockSpec((C, RET_V), lambda b, c: (row(b, c), 0)),
            pl.BlockSpec((H, C, C), full3),
            pl.BlockSpec((H, C, 1), full3),
            pl.BlockSpec((H, C, 1), full3),
        ],
        out_specs=pl.BlockSpec((C, RET_V), lambda b, c: (row(b, c), 0)),
        out_shape=jax.ShapeDtypeStruct((n, RET_V), BF16),
        scratch_shapes=[pltpu.VMEM((H, RET_DK, RET_DV), F32)],
        compiler_params=_cparams(("parallel", "arbitrary")),
        name="retention",
    )(chunk_decay, qk, qk, v, sg, d_mask, xi, zeta)


def _lru_body(u_ref, g_ref, cw_ref, cb_ref, wr_ref, wi_ref, br_ref, bi_ref, lam_ref,
              o_ref, ubuf_ref, a_ref, b_ref, carry_ref):
    T = u_ref.shape[0]
    G = T // SUBLANES
    P = _seg_pitch(T)
    ncb = LRU_WIDTH // LANES
    t = pl.program_id(1)

    @pl.when(t == 0)
    def _():
        ubuf_ref[0:SUBLANES, :] = jnp.zeros((SUBLANES, LRU_WIDTH), F32)
        carry_ref[...] = jnp.zeros_like(carry_ref)

    ubuf_ref[SUBLANES:SUBLANES + T, :] = u_ref[...].astype(F32)
    cw = cw_ref[...]
    uc = cb_ref[...] + cw[CONV_WIDTH - 1:CONV_WIDTH, :] * ubuf_ref[SUBLANES:SUBLANES + T, :]
    for j in range(CONV_WIDTH - 1):
        back = CONV_WIDTH - 1 - j
        uc = uc + cw[j:j + 1, :] * ubuf_ref[SUBLANES - back:SUBLANES - back + T, :]
    ubuf_ref[0:SUBLANES, :] = ubuf_ref[T:T + SUBLANES, :]

    ucb = uc.astype(BF16)
    neg_c_sp = -LRU_C * jax.nn.softplus(-lam_ref[...])
    for p in range(LRU_PAIRS):
        lo = p * LRU_PAIR
        sl = ucb[:, lo:lo + LRU_PAIR]
        r = jax.nn.sigmoid(jnp.dot(sl, wr_ref[p], preferred_element_type=F32) + br_ref[:, lo:lo + LRU_PAIR])
        i = jax.nn.sigmoid(jnp.dot(sl, wi_ref[p], preferred_element_type=F32) + bi_ref[:, lo:lo + LRU_PAIR])
        log_a = r * neg_c_sp[:, lo:lo + LRU_PAIR]
        a = jnp.exp(log_a)
        inp = jnp.sqrt(-jnp.tanh(log_a) * (a * a + 1.0)) * (i * uc[:, lo:lo + LRU_PAIR])
        for cc in range(LRU_PAIR // LANES):
            cb = p * (LRU_PAIR // LANES) + cc
            for s in range(SUBLANES):
                a_ref[cb, s * P:s * P + G, :] = a[s * G:(s + 1) * G, cc * LANES:(cc + 1) * LANES]
                b_ref[cb, s * P:s * P + G, :] = inp[s * G:(s + 1) * G, cc * LANES:(cc + 1) * LANES]

    def ends(j, hp):
        hs, ps = hp
        nh, npr = [], []
        for cb in range(ncb):
            a = a_ref[cb, pl.ds(j, SUBLANES, stride=P), :]
            b = b_ref[cb, pl.ds(j, SUBLANES, stride=P), :]
            nh.append(a * hs[cb] + b)
            npr.append(a * ps[cb])
        return tuple(nh), tuple(npr)

    zeros = tuple(jnp.zeros((SUBLANES, LANES), F32) for _ in range(ncb))
    ones = tuple(jnp.ones((SUBLANES, LANES), F32) for _ in range(ncb))
    h_end, p_end = lax.fori_loop(0, G, ends, (zeros, ones))

    starts = []
    for cb in range(ncb):
        cin = carry_ref[:, cb * LANES:(cb + 1) * LANES]
        rows_in = []
        for s in range(SUBLANES):
            rows_in.append(cin)
            cin = h_end[cb][s:s + 1, :] + p_end[cb][s:s + 1, :] * cin
        carry_ref[:, cb * LANES:(cb + 1) * LANES] = cin
        starts.append(jnp.concatenate(rows_in, axis=0))

    def states(j, hs):
        nh = []
        for cb in range(ncb):
            a = a_ref[cb, pl.ds(j, SUBLANES, stride=P), :]
            b = b_ref[cb, pl.ds(j, SUBLANES, stride=P), :]
            hn = a * hs[cb] + b
            b_ref[cb, pl.ds(j, SUBLANES, stride=P), :] = hn
            nh.append(hn)
        return tuple(nh)

    lax.fori_loop(0, G, states, tuple(starts))

    for cb in range(ncb):
        for s in range(SUBLANES):
            rows = slice(s * G, (s + 1) * G)
            gate = g_ref[rows, cb * LANES:(cb + 1) * LANES].astype(F32)
            o_ref[rows, cb * LANES:(cb + 1) * LANES] = (gate * b_ref[cb, s * P:s * P + G, :]).astype(o_ref.dtype)


def _seg_pitch(t):
    g = t // SUBLANES
    units = -(-g // SUBLANES)
    return SUBLANES * (units + 1 - units % 2)


def _rglru(u, gate, conv_w, conv_b, w_r, b_r, w_i, b_i, lam, batch, seq):
    n = u.shape[0]
    T = min(LRU_T, seq)
    nt = seq // T
    ncb = LRU_WIDTH // LANES

    def pairs(w):
        z = jnp.zeros((LRU_PAIRS, LRU_PAIR, LRU_PAIR), F32)
        w4 = w.reshape(LRU_PAIRS, 2, LRU_BLOCK, LRU_BLOCK)
        z = z.at[:, :LRU_BLOCK, :LRU_BLOCK].set(w4[:, 0])
        z = z.at[:, LRU_BLOCK:, LRU_BLOCK:].set(w4[:, 1])
        return z.astype(BF16)

    row = lambda b, t: (b * nt + t, 0)
    full2 = lambda b, t: (0, 0)
    full3 = lambda b, t: (0, 0, 0)
    return pl.pallas_call(
        _lru_body,
        grid=(batch, nt),
        in_specs=[
            pl.BlockSpec((T, LRU_WIDTH), row),
            pl.BlockSpec((T, LRU_WIDTH), row),
            pl.BlockSpec((CONV_WIDTH, LRU_WIDTH), full2),
            pl.BlockSpec((1, LRU_WIDTH), full2),
            pl.BlockSpec((LRU_PAIRS, LRU_PAIR, LRU_PAIR), full3),
            pl.BlockSpec((LRU_PAIRS, LRU_PAIR, LRU_PAIR), full3),
            pl.BlockSpec((1, LRU_WIDTH), full2),
            pl.BlockSpec((1, LRU_WIDTH), full2),
            pl.BlockSpec((1, LRU_WIDTH), full2),
        ],
        out_specs=pl.BlockSpec((T, LRU_WIDTH), row),
        out_shape=jax.ShapeDtypeStruct((n, LRU_WIDTH), BF16),
        scratch_shapes=[
            pltpu.VMEM((T + SUBLANES, LRU_WIDTH), F32),
            pltpu.VMEM((ncb, SUBLANES * _seg_pitch(T), LANES), F32),
            pltpu.VMEM((ncb, SUBLANES * _seg_pitch(T), LANES), F32),
            pltpu.VMEM((1, LRU_WIDTH), F32),
        ],
        compiler_params=_cparams(("parallel", "arbitrary")),
        name="rglru",
    )(u, gate, conv_w, conv_b.reshape(1, -1), pairs(w_r), pairs(w_i),
      b_r.reshape(1, -1), b_i.reshape(1, -1), lam.reshape(1, -1))


def _layer_norm_rows(y, g, b):
    mu = jnp.mean(y, axis=-1, keepdims=True)
    yc = y - mu
    var = jnp.mean(yc * yc, axis=-1, keepdims=True)
    return yc * lax.rsqrt(var + LN_EPS) * g + b


def _merge_body(alpha, ret_ref, lru_ref, gm_ref, x_ref, wro_ref, wlo_ref, wo_ref, g1_ref, b1_ref,
                wrt_ref, brt_ref, x1_ref, rw_ref, lpc_ref, lpr_ref, cnt_ref):
    pr = jnp.dot(ret_ref[...], wro_ref[...], preferred_element_type=F32)
    pu = jnp.dot(lru_ref[...], wlo_ref[...], preferred_element_type=F32)
    merged = gm_ref[:, :D_MODEL].astype(F32) * pr + gm_ref[:, D_MODEL:].astype(F32) * pu
    y = alpha * x_ref[...] + jnp.dot(merged.astype(BF16), wo_ref[...], preferred_element_type=F32)
    x1 = _layer_norm_rows(y, g1_ref[...], b1_ref[...])
    x1_ref[...] = x1

    tm = x1.shape[0]
    x_hi = x1.astype(BF16)
    x_lo = (x1 - x_hi.astype(F32)).astype(BF16)
    hh = jnp.dot(x_hi, wrt_ref[...], preferred_element_type=F32)
    lh = jnp.dot(x_lo, wrt_ref[:, :LANES], preferred_element_type=F32)
    lg = hh[:, :LANES] + hh[:, LANES:] + lh + brt_ref[...]
    lane = lax.broadcasted_iota(I32, (tm, LANES), 1)
    big = jnp.int32(LANES)
    neg = jnp.float32(-jnp.inf)
    gmask = lane < N_GROUPS
    gl = jnp.where(gmask, lg, neg)
    gmax = jnp.max(gl, axis=-1, keepdims=True)
    g_idx = jnp.min(jnp.where(gmask & (gl == gmax), lane, big), axis=-1, keepdims=True)
    g_w = 1.0 / jnp.sum(jnp.where(gmask, jnp.exp(gl - gmax), 0.0), axis=-1, keepdims=True)
    e_lo = N_GROUPS + EXPERTS_PER_GROUP * g_idx
    emask = (lane >= e_lo) & (lane < e_lo + EXPERTS_PER_GROUP)
    el = jnp.where(emask, lg, neg)
    v1 = jnp.max(el, axis=-1, keepdims=True)
    i1 = jnp.min(jnp.where(emask & (el == v1), lane, big), axis=-1, keepdims=True)
    emask2 = emask & (lane != i1)
    el2 = jnp.where(emask2, lg, neg)
    v2 = jnp.max(el2, axis=-1, keepdims=True)
    i2 = jnp.min(jnp.where(emask2 & (el2 == v2), lane, big), axis=-1, keepdims=True)
    ex = jnp.exp(v2 - v1)
    den = 1.0 + ex
    w1 = g_w / den
    w2 = g_w * ex / den
    e1 = i1 - N_GROUPS
    e2 = i2 - N_GROUPS
    rw_ref[...] = jnp.where(lane == 0, w1, jnp.where(lane == 1, w2, 0.0))

    oh = (lane == e1).astype(F32) + (lane == e2).astype(F32)
    rowi = lax.broadcasted_iota(I32, (tm, tm), 0)
    coli = lax.broadcasted_iota(I32, (tm, tm), 1)
    tri = jnp.where(coli < rowi, 1.0, 0.0).astype(BF16)
    before = jnp.dot(tri, oh.astype(BF16), preferred_element_type=F32)
    cnt = jnp.sum(oh, axis=0, keepdims=True)
    units = jnp.floor((cnt + (SEG_ALIGN - 1.0)) * (1.0 / SEG_ALIGN))
    er = lax.broadcasted_iota(I32, (LANES, LANES), 0)
    ec = lax.broadcasted_iota(I32, (LANES, LANES), 1)
    upper = jnp.where(er < ec, 1.0, 0.0).astype(BF16)
    offs = SEG_ALIGN * jnp.dot(jnp.broadcast_to(units, (SUBLANES, LANES)).astype(BF16), upper,
                               preferred_element_type=F32)[0:1, :]
    pos = before + offs
    lp1 = jnp.sum(jnp.where(lane == e1, pos, 0.0), axis=-1, keepdims=True)
    lp2 = jnp.sum(jnp.where(lane == e2, pos, 0.0), axis=-1, keepdims=True)
    lpc = jnp.where(lane == 0, lp1, jnp.where(lane == 1, lp2, 0.0))
    lpc_ref[...] = lpc
    lpr_ref[...] = lpc.T[0:SUBLANES, :]
    cnt_ref[...] = jnp.broadcast_to(cnt, cnt_ref.shape)


def _merge(ret, lru, gm, x2, w_ret_o, w_lru_o, w_out, ln_g, ln_b, w_rt, b_rt, alpha):
    n = x2.shape[0]
    tm = min(ROUTE_TM, n)
    rowmap = lambda i: (i, 0)
    full = lambda i: (0, 0)
    return pl.pallas_call(
        functools.partial(_merge_body, alpha),
        grid=(n // tm,),
        in_specs=[
            pl.BlockSpec((tm, RET_V), rowmap),
            pl.BlockSpec((tm, LRU_WIDTH), rowmap),
            pl.BlockSpec((tm, 2 * D_MODEL), rowmap),
            pl.BlockSpec((tm, D_MODEL), rowmap),
            pl.BlockSpec((RET_V, D_MODEL), full),
            pl.BlockSpec((LRU_WIDTH, D_MODEL), full),
            pl.BlockSpec((D_MODEL, D_MODEL), full),
            pl.BlockSpec((1, D_MODEL), full),
            pl.BlockSpec((1, D_MODEL), full),
            pl.BlockSpec((D_MODEL, 2 * LANES), full),
            pl.BlockSpec((1, LANES), full),
        ],
        out_specs=[
            pl.BlockSpec((tm, D_MODEL), rowmap),
            pl.BlockSpec((tm, LANES), rowmap),
            pl.BlockSpec((tm, LANES), rowmap),
            pl.BlockSpec((SUBLANES, tm), lambda i: (0, i)),
            pl.BlockSpec((None, SUBLANES, LANES), lambda i: (i, 0, 0)),
        ],
        out_shape=[
            jax.ShapeDtypeStruct((n, D_MODEL), F32),
            jax.ShapeDtypeStruct((n, LANES), F32),
            jax.ShapeDtypeStruct((n, LANES), F32),
            jax.ShapeDtypeStruct((SUBLANES, n), F32),
            jax.ShapeDtypeStruct((n // tm, SUBLANES, LANES), F32),
        ],
        compiler_params=_cparams(("parallel",)),
        name="merge_ln_route",
    )(ret, lru, gm, x2, w_ret_o, w_lru_o, w_out, ln_g.reshape(1, -1), ln_b.reshape(1, -1), w_rt, b_rt)


def _run_lists(units, run_off, gdst):
    k = jnp.arange(N_EXPERTS, dtype=I32)
    cnts, offs, dsts = [], [], []
    for b in range(RUN_BITS):
        bit = (units >> b) & 1
        low = (units & ((1 << b) - 1)) * SEG_ALIGN
        pos = jnp.cumsum(bit, axis=1) - bit
        hit = (bit[:, None, :] == 1) & (pos[:, None, :] == k[None, :, None])
        offs.append(jnp.sum(jnp.where(hit, (run_off + low)[:, None, :], 0), axis=2))
        dsts.append(jnp.sum(jnp.where(hit, (gdst + low)[:, None, :], 0), axis=2))
        cnts.append(jnp.sum(bit, axis=1))
    flat = lambda parts: jnp.stack(parts, axis=1).reshape(-1).astype(I32)
    return flat(cnts), flat(offs), flat(dsts)


def _run_copies(lists, tile, loc_ref, glob_hbm, sem, to_global, wait):
    cnt_ref, off_ref, dst_ref = lists
    for b in range(RUN_BITS):
        rows = SEG_ALIGN << b
        base = tile * RUN_BITS + b

        def piece(k, carry):
            off = pl.multiple_of(off_ref[base * N_EXPERTS + k], SEG_ALIGN)
            dst = pl.multiple_of(dst_ref[base * N_EXPERTS + k], SEG_ALIGN)
            l = loc_ref.at[pl.ds(off, rows), :]
            g = glob_hbm.at[pl.ds(dst, rows), :]
            cp = pltpu.make_async_copy(l, g, sem) if to_global else pltpu.make_async_copy(g, l, sem)
            if wait:
                cp.wait()
            else:
                cp.start()
            return carry

        lax.fori_loop(0, cnt_ref[base], piece, 0)


def _zero_rows(start, units, max_units, zero_ref, xs_hbm, sem, wait):
    pos = start
    for b in range((max_units - 1).bit_length()):
        rows = SEG_ALIGN << b
        bit = lax.bitwise_and(lax.shift_right_logical(units, b), 1)

        @pl.when(bit == 1)
        def _():
            dst = xs_hbm.at[pl.ds(pl.multiple_of(pos, SEG_ALIGN), rows), :]
            cp = pltpu.make_async_copy(zero_ref.at[pl.ds(0, rows), :], dst, sem)
            if wait:
                cp.wait()
            else:
                cp.start()

        pos = pos + bit * rows


def _onehot_rows(lpr_ref, n_rows):
    tm = lpr_ref.shape[1]
    sub = lax.broadcasted_iota(I32, (n_rows, tm), 0)
    lp1 = lpr_ref[0:1, :].astype(I32)
    lp2 = lpr_ref[1:2, :].astype(I32)
    return jnp.where((sub == lp1) | (sub == lp2), 1.0, 0.0).astype(BF16)


def _dispatch_body(cnt_ref, off_ref, dst_ref, pad_start_ref, pad_units_ref, total_ref, x1_ref, lpr_ref,
                   xs_hbm, loc_ref, zero_ref, sems, zsem):
    i = pl.program_id(0)
    last = pl.num_programs(0) - 1
    slot = lax.rem(i, 2)
    lists = (cnt_ref, off_ref, dst_ref)
    perm = _onehot_rows(lpr_ref, loc_ref.shape[1])
    loc_ref[slot] = jnp.dot(perm, x1_ref[...].astype(BF16), preferred_element_type=F32)
    _run_copies(lists, i, loc_ref.at[slot], xs_hbm, sems.at[slot], True, False)

    @pl.when(i > 0)
    def _():
        _run_copies(lists, i - 1, loc_ref.at[1 - slot], xs_hbm, sems.at[1 - slot], True, True)

    @pl.when(i == last)
    def _():
        zero_ref[...] = jnp.zeros_like(zero_ref)
        max_units = zero_ref.shape[0] // SEG_ALIGN
        for wait in (False, True):
            def region(e, carry):
                _zero_rows(pad_start_ref[e], pad_units_ref[e], max_units, zero_ref, xs_hbm, zsem, wait)
                return carry
            lax.fori_loop(0, N_EXPERTS, region, 0)
            _zero_tail(total_ref[0], zero_ref, xs_hbm, zsem, wait)
        _run_copies(lists, i, loc_ref.at[slot], xs_hbm, sems.at[slot], True, True)


def _zero_tail(total, zero_ref, xs_hbm, sem, wait):
    zr = zero_ref.shape[0]
    shift = zr.bit_length() - 1
    assert zr == 1 << shift and xs_hbm.shape[0] % SEG_ALIGN == 0
    dead = xs_hbm.shape[0] - total
    n_full = lax.shift_right_logical(dead, shift)

    def full(k, carry):
        dst = xs_hbm.at[pl.ds(pl.multiple_of(total + k * zr, SEG_ALIGN), zr), :]
        cp = pltpu.make_async_copy(zero_ref, dst, sem)
        if wait:
            cp.wait()
        else:
            cp.start()
        return carry

    lax.fori_loop(0, n_full, full, 0)
    rem = lax.shift_right_logical(dead - n_full * zr, SEG_ALIGN.bit_length() - 1)
    _zero_rows(total + n_full * zr, rem, zr // SEG_ALIGN, zero_ref, xs_hbm, sem, wait)


def _local_rows(tm):
    return TOP_K * tm + N_EXPERTS * SEG_ALIGN


def _dispatch(lists, pad_start, pad_units, total, x1, lpr, m_max):
    n = x1.shape[0]
    tm = min(ROUTE_TM, n)
    return pl.pallas_call(
        _dispatch_body,
        grid_spec=pltpu.PrefetchScalarGridSpec(
            num_scalar_prefetch=6,
            grid=(n // tm,),
            in_specs=[pl.BlockSpec((tm, D_MODEL), lambda i, *_: (i, 0)),
                      pl.BlockSpec((SUBLANES, tm), lambda i, *_: (0, i))],
            out_specs=pl.BlockSpec(memory_space=pl.ANY),
            scratch_shapes=[pltpu.VMEM((2, _local_rows(tm), D_MODEL), F32),
                            pltpu.VMEM((MOE_RT, D_MODEL), F32),
                            pltpu.SemaphoreType.DMA((2,)), pltpu.SemaphoreType.DMA],
        ),
        out_shape=jax.ShapeDtypeStruct((m_max, D_MODEL), F32),
        compiler_params=_cparams(("arbitrary",)),
        name="dispatch",
    )(*lists, pad_start, pad_units, total, x1, lpr)


def _expert_body(exp_ref, live_ref, xs_ref, wg_ref, wu_ref, wd_ref, y_ref, wgb_ref, wub_ref, wdb_ref, cur_ref):
    g = pl.program_id(0)
    e = exp_ref[g]

    @pl.when(g == 0)
    def _():
        cur_ref[0] = -1

    @pl.when(g < live_ref[0])
    def _():
        @pl.when(cur_ref[0] != e)
        def _():
            wgb_ref[...] = wg_ref[...].astype(BF16)
            wub_ref[...] = wu_ref[...].astype(BF16)
            wdb_ref[...] = wd_ref[...].astype(BF16)
            cur_ref[0] = e

        xb = xs_ref[...].astype(BF16)
        hg = jnp.dot(xb, wgb_ref[...], preferred_element_type=F32)
        hu = jnp.dot(xb, wub_ref[...], preferred_element_type=F32)
        hm = (hg * jax.nn.sigmoid(hg) * hu).astype(BF16)
        y_ref[...] = jnp.dot(hm, wdb_ref[...], preferred_element_type=F32)


def _experts(tile_expert, n_live, xs, w_gate, w_up, w_down):
    m = xs.shape[0]
    rt = min(MOE_RT, m)
    rows = lambda g, e, n: (jnp.minimum(g, n[0] - 1), 0)
    return pl.pallas_call(
        _expert_body,
        grid_spec=pltpu.PrefetchScalarGridSpec(
            num_scalar_prefetch=2,
            grid=(m // rt,),
            in_specs=[
                pl.BlockSpec((rt, D_MODEL), rows),
                pl.BlockSpec((None, D_MODEL, D_EXPERT), lambda g, e, n: (e[g], 0, 0)),
                pl.BlockSpec((None, D_MODEL, D_EXPERT), lambda g, e, n: (e[g], 0, 0)),
                pl.BlockSpec((None, D_EXPERT, D_MODEL), lambda g, e, n: (e[g], 0, 0)),
            ],
            out_specs=pl.BlockSpec((rt, D_MODEL), rows),
            scratch_shapes=[
                pltpu.VMEM((D_MODEL, D_EXPERT), BF16),
                pltpu.VMEM((D_MODEL, D_EXPERT), BF16),
                pltpu.VMEM((D_EXPERT, D_MODEL), BF16),
                pltpu.SMEM((1,), I32),
            ],
        ),
        out_shape=jax.ShapeDtypeStruct((m, D_MODEL), F32),
        input_output_aliases={2: 0},
        compiler_params=_cparams(("arbitrary",)),
        name="experts",
    )(tile_expert, n_live, xs, w_gate, w_up, w_down)


def _combine_body(alpha, cnt_ref, off_ref, dst_ref, x1_ref, rw_ref, lpc_ref, g2_ref, b2_ref, ys_hbm, o_ref,
                  loc_ref, sems):
    i = pl.program_id(0)
    slot = lax.rem(i, 2)
    lists = (cnt_ref, off_ref, dst_ref)

    @pl.when(i == 0)
    def _():
        loc_ref[...] = jnp.zeros_like(loc_ref)
        _run_copies(lists, i, loc_ref.at[slot], ys_hbm, sems.at[slot], False, False)

    @pl.when(i + 1 < pl.num_programs(0))
    def _():
        _run_copies(lists, i + 1, loc_ref.at[1 - slot], ys_hbm, sems.at[1 - slot], False, False)

    _run_copies(lists, i, loc_ref.at[slot], ys_hbm, sems.at[slot], False, True)
    tm = x1_ref.shape[0]
    n_rows = loc_ref.shape[1]
    yb = loc_ref[slot].astype(BF16)
    lanes = lax.broadcasted_iota(I32, (tm, n_rows), 1)
    sel = (jnp.where(lanes == lpc_ref[:, 0:1].astype(I32), rw_ref[:, 0:1], 0.0)
           + jnp.where(lanes == lpc_ref[:, 1:2].astype(I32), rw_ref[:, 1:2], 0.0))
    moe = jnp.dot(sel.astype(BF16), yb, preferred_element_type=F32)
    y = alpha * x1_ref[...] + moe
    o_ref[...] = _layer_norm_rows(y, g2_ref[...], b2_ref[...])


def _combine(lists, x1, rw, lpc, ln_g, ln_b, ys, alpha):
    n = x1.shape[0]
    tm = min(ROUTE_TM, n)
    rowmap = lambda i, *_: (i, 0)
    full = lambda i, *_: (0, 0)
    return pl.pallas_call(
        functools.partial(_combine_body, alpha),
        grid_spec=pltpu.PrefetchScalarGridSpec(
            num_scalar_prefetch=3,
            grid=(n // tm,),
            in_specs=[
                pl.BlockSpec((tm, D_MODEL), rowmap),
                pl.BlockSpec((tm, LANES), rowmap),
                pl.BlockSpec((tm, LANES), rowmap),
                pl.BlockSpec((1, D_MODEL), full),
                pl.BlockSpec((1, D_MODEL), full),
                pl.BlockSpec(memory_space=pl.ANY),
            ],
            out_specs=pl.BlockSpec((tm, D_MODEL), rowmap),
            scratch_shapes=[pltpu.VMEM((2, _local_rows(tm), D_MODEL), F32), pltpu.SemaphoreType.DMA((2,))],
        ),
        out_shape=jax.ShapeDtypeStruct((n, D_MODEL), F32),
        compiler_params=_cparams(("arbitrary",)),
        name="combine_ln",
    )(*lists, x1, rw, lpc, ln_g.reshape(1, -1), ln_b.reshape(1, -1), ys)


def _rotary_tables(seq):
    half = RET_DK // 2
    inv = ROPE_BASE ** (-jnp.arange(half, dtype=F32) / half)
    ang = jnp.arange(seq, dtype=F32)[:, None] * inv[None, :]
    return jnp.cos(ang), jnp.sin(ang)


def _router_weights(w_group, b_group, w_exp_router, b_exp_router):
    d = w_group.shape[0]
    w = jnp.zeros((d, LANES), F32).at[:, :N_GROUPS].set(w_group).at[:, N_GROUPS:N_GROUPS + N_EXPERTS].set(w_exp_router)
    b = jnp.zeros((1, LANES), F32).at[0, :N_GROUPS].set(b_group).at[0, N_GROUPS:N_GROUPS + N_EXPERTS].set(b_exp_router)
    w_hi = w.astype(BF16)
    w_lo = (w - w_hi.astype(F32)).astype(BF16)
    return jnp.concatenate([w_hi, w_lo], axis=1), b


def _layer(x, depth, w_in, b_merge, conv_w, conv_b, w_rg_r, b_rg_r, w_rg_i, b_rg_i, lru_lambda,
           w_ret_o, w_lru_o, w_out, ln1_g, ln1_b, w_group, b_group, w_exp_router, b_exp_router,
           w_e_gate, w_e_up, w_e_down, ln2_g, ln2_b):
    B, S, D = x.shape
    n = B * S
    alpha = (2.0 * depth) ** 0.25
    x2 = x.reshape(n, D)
    wb = w_in.astype(BF16)
    cos, sin = _rotary_tables(S)
    qk, v, sg, u, gl, gm = _proj(x2, wb, cos, sin, b_merge.reshape(1, -1), S)

    ret = _retention(qk, v, sg, B, S)
    lru = _rglru(u, gl, conv_w, conv_b, w_rg_r, b_rg_r, w_rg_i, b_rg_i, lru_lambda, B, S)

    w_rt, b_rt = _router_weights(w_group, b_group, w_exp_router, b_exp_router)
    x1, rw, lpc, lpr, tcnt = _merge(ret, lru, gm, x2, w_ret_o.astype(BF16), w_lru_o.astype(BF16),
                                    w_out.astype(BF16), ln1_g, ln1_b, w_rt, b_rt, alpha)

    tm = min(ROUTE_TM, n)
    n_t = n // tm
    m_max = n_t * _local_rows(tm) + N_EXPERTS * MOE_RT
    rt = min(MOE_RT, m_max)
    cnt = tcnt[:, 0, :N_EXPERTS].astype(I32)
    units = (cnt + (SEG_ALIGN - 1)) // SEG_ALIGN
    run = units * SEG_ALIGN
    sizes = jnp.sum(run, axis=0)
    region = (sizes + (rt - 1)) // rt * rt
    e_end = jnp.cumsum(region)
    e_start = e_end - region
    gdst = e_start[None, :] + jnp.cumsum(run, axis=0) - run
    run_off = jnp.cumsum(run, axis=1) - run
    lists = _run_lists(units, run_off, gdst)
    pad_start = (e_start + sizes).astype(I32)
    pad_units = ((region - sizes) // SEG_ALIGN).astype(I32)
    total = e_end[-1:].astype(I32)
    tile_start = jnp.arange(m_max // rt, dtype=I32) * rt
    n_live = total // rt
    tile_expert = jnp.sum((e_end[None, :] <= jnp.minimum(tile_start, total - rt)[:, None]).astype(I32), axis=1)

    xs = _dispatch(lists, pad_start, pad_units, total, x1, lpr, m_max)
    ys = _experts(tile_expert, n_live, xs, w_e_gate, w_e_up, w_e_down)
    out = _combine(lists, x1, rw, lpc, ln2_g, ln2_b, ys, alpha)
    return out.reshape(B, S, D)


def kernel(x, w_in, b_merge, conv_w, conv_b, w_rg_r, b_rg_r, w_rg_i, b_rg_i, lru_lambda, w_ret_o, w_lru_o, w_out, ln1_g, ln1_b, w_group, b_group, w_exp_router, b_exp_router, w_e_gate, w_e_up, w_e_down, ln2_g, ln2_b):
    depth = w_in.shape[0]
    for l in range(depth):
        x = _layer(x, depth, w_in[l], b_merge[l], conv_w[l], conv_b[l], w_rg_r[l], b_rg_r[l], w_rg_i[l],
                   b_rg_i[l], lru_lambda[l], w_ret_o[l], w_lru_o[l], w_out[l], ln1_g[l], ln1_b[l],
                   w_group[l], b_group[l], w_exp_router[l], b_exp_router[l], w_e_gate[l], w_e_up[l],
                   w_e_down[l], ln2_g[l], ln2_b[l])
    return x
```

```python
import functools

import jax
import jax.numpy as jnp
import numpy as np
from jax import lax
from jax.experimental import pallas as pl
from jax.experimental.pallas import tpu as pltpu

F32 = jnp.float32
BF16 = jnp.bfloat16
I32 = jnp.int32

D_MODEL = 1024
RET_HEADS = 4
RET_DK = 256
RET_DV = 512
RET_QK = RET_HEADS * RET_DK
RET_V = RET_HEADS * RET_DV
ROPE_BASE = 10000.0
LRU_WIDTH = 1536
LRU_BLOCKS = 8
LRU_BLOCK = LRU_WIDTH // LRU_BLOCKS
LRU_PAIR = 2 * LRU_BLOCK
LRU_PAIRS = LRU_BLOCKS // 2
CONV_WIDTH = 4
LRU_C = 8.0
N_GROUPS = 4
EXPERTS_PER_GROUP = 8
N_EXPERTS = N_GROUPS * EXPERTS_PER_GROUP
TOP_K = 2
D_EXPERT = 512
LN_EPS = 1e-5

LANES = 128
SUBLANES = 8
VMEM_LIMIT = 56 * 1024 * 1024

PROJ_TM = 256
PROJ_CHUNK = 512
RET_CHUNK = 256
ROUTE_TM = 512
MOE_RT = 512
SEG_ALIGN = SUBLANES
RUN_BITS = (ROUTE_TM // SEG_ALIGN).bit_length()


def _cparams(sem):
    return pltpu.CompilerParams(dimension_semantics=sem, vmem_limit_bytes=VMEM_LIMIT)


PROJ_WIDTHS = (2 * RET_QK, RET_V, RET_V, LRU_WIDTH, LRU_WIDTH, 2 * D_MODEL)


def _seg_pitch(t):
    g = t // SUBLANES
    units = -(-g // SUBLANES)
    return SUBLANES * (units + 1 - units % 2)


def _proj_lru_body(per_seq, x_ref, w_ref, cos_ref, sin_ref, bm_ref, cw_ref, cb_ref, wr_ref, wi_ref, br_ref,
                   bi_ref, lam_ref, qk_ref, v_ref, sg_ref, gm_ref, lru_ref,
                   ubuf_ref, gbuf_ref, a_ref, b_ref, carry_ref):
    T = x_ref.shape[0]
    G = T // SUBLANES
    P = _seg_pitch(T)
    ncb = LRU_WIDTH // LANES
    o_qk, o_v, o_sg, o_u, o_gl, o_gm = (int(o) for o in np.cumsum((0,) + PROJ_WIDTHS[:-1]))
    xb = x_ref[...].astype(BF16)

    def seg(start, lo, hi):
        return jnp.dot(xb, w_ref[:, start + lo:start + hi], preferred_element_type=F32)

    @pl.when(lax.rem(pl.program_id(0), per_seq) == 0)
    def _():
        ubuf_ref[0:SUBLANES, :] = jnp.zeros((SUBLANES, LRU_WIDTH), F32)
        carry_ref[...] = jnp.zeros_like(carry_ref)

    for lo in range(0, LRU_WIDTH, PROJ_CHUNK):
        hi = lo + PROJ_CHUNK
        ubuf_ref[SUBLANES:SUBLANES + T, lo:hi] = seg(o_u, lo, hi)
        gbuf_ref[:, lo:hi] = jax.nn.gelu(seg(o_gl, lo, hi))

    cw = cw_ref[...]
    uc = cb_ref[...] + cw[CONV_WIDTH - 1:CONV_WIDTH, :] * ubuf_ref[SUBLANES:SUBLANES + T, :]
    for j in range(CONV_WIDTH - 1):
        back = CONV_WIDTH - 1 - j
        uc = uc + cw[j:j + 1, :] * ubuf_ref[SUBLANES - back:SUBLANES - back + T, :]
    ubuf_ref[0:SUBLANES, :] = ubuf_ref[T:T + SUBLANES, :]

    ucb = uc.astype(BF16)
    neg_c_sp = -LRU_C * jax.nn.softplus(-lam_ref[...])
    for p in range(LRU_PAIRS):
        lo = p * LRU_PAIR
        sl = ucb[:, lo:lo + LRU_PAIR]
        r = jax.nn.sigmoid(jnp.dot(sl, wr_ref[p], preferred_element_type=F32) + br_ref[:, lo:lo + LRU_PAIR])
        i = jax.nn.sigmoid(jnp.dot(sl, wi_ref[p], preferred_element_type=F32) + bi_ref[:, lo:lo + LRU_PAIR])
        log_a = r * neg_c_sp[:, lo:lo + LRU_PAIR]
        a = jnp.exp(log_a)
        inp = jnp.sqrt(-jnp.tanh(log_a) * (a * a + 1.0)) * (i * uc[:, lo:lo + LRU_PAIR])
        for cc in range(LRU_PAIR // LANES):
            cb = p * (LRU_PAIR // LANES) + cc
            for s in range(SUBLANES):
                a_ref[cb, s * P:s * P + G, :] = a[s * G:(s + 1) * G, cc * LANES:(cc + 1) * LANES]
                b_ref[cb, s * P:s * P + G, :] = inp[s * G:(s + 1) * G, cc * LANES:(cc + 1) * LANES]

    half = RET_DK // 2
    for jc in range(2 * RET_HEADS):
        acc = seg(o_qk, jc * RET_DK, (jc + 1) * RET_DK)
        t1, t2 = acc[:, :half], acc[:, half:]
        c, s = cos_ref[...], sin_ref[...]
        scale = RET_DK ** -0.5 if jc >= RET_HEADS else 1.0
        qk_ref[:, jc * RET_DK:jc * RET_DK + half] = ((t1 * c - t2 * s) * scale).astype(qk_ref.dtype)
        qk_ref[:, jc * RET_DK + half:(jc + 1) * RET_DK] = ((t1 * s + t2 * c) * scale).astype(qk_ref.dtype)
    for lo in range(0, RET_V, PROJ_CHUNK):
        hi = lo + PROJ_CHUNK
        v_ref[:, lo:hi] = seg(o_v, lo, hi).astype(v_ref.dtype)
        g = seg(o_sg, lo, hi)
        sg_ref[:, lo:hi] = (g * jax.nn.sigmoid(g)).astype(sg_ref.dtype)
    for lo in range(0, 2 * D_MODEL, PROJ_CHUNK):
        hi = lo + PROJ_CHUNK
        gm_ref[:, lo:hi] = jax.nn.sigmoid(seg(o_gm, lo, hi) + bm_ref[:, lo:hi]).astype(gm_ref.dtype)

    def step(j, hp):
        hs, ps = hp
        nh, npr = [], []
        for cb in range(ncb):
            a = a_ref[cb, pl.ds(j, SUBLANES, stride=P), :]
            b = b_ref[cb, pl.ds(j, SUBLANES, stride=P), :]
            hn = a * hs[cb] + b
            pn = a * ps[cb]
            b_ref[cb, pl.ds(j, SUBLANES, stride=P), :] = hn
            a_ref[cb, pl.ds(j, SUBLANES, stride=P), :] = pn
            nh.append(hn)
            npr.append(pn)
        return tuple(nh), tuple(npr)

    zeros = tuple(jnp.zeros((SUBLANES, LANES), F32) for _ in range(ncb))
    ones = tuple(jnp.ones((SUBLANES, LANES), F32) for _ in range(ncb))
    h_end, p_end = lax.fori_loop(0, G, step, (zeros, ones))

    for cb in range(ncb):
        cin = carry_ref[:, cb * LANES:(cb + 1) * LANES]
        for s in range(SUBLANES):
            rows = slice(s * G, (s + 1) * G)
            hseg = b_ref[cb, s * P:s * P + G, :] + a_ref[cb, s * P:s * P + G, :] * cin
            gate = gbuf_ref[rows, cb * LANES:(cb + 1) * LANES]
            lru_ref[rows, cb * LANES:(cb + 1) * LANES] = (gate * hseg).astype(lru_ref.dtype)
            cin = h_end[cb][s:s + 1, :] + p_end[cb][s:s + 1, :] * cin
        carry_ref[:, cb * LANES:(cb + 1) * LANES] = cin


def _block_diag_pairs(w):
    w4 = w.reshape(LRU_PAIRS, 2, LRU_BLOCK, LRU_BLOCK)
    z = jnp.zeros_like(w4[:, 0])
    top = jnp.concatenate([w4[:, 0], z], axis=2)
    bottom = jnp.concatenate([z, w4[:, 1]], axis=2)
    return jnp.concatenate([top, bottom], axis=1).astype(BF16)


def _proj_lru(x2, w, cos, sin, b_merge, conv_w, conv_b, w_r, b_r, w_i, b_i, lam, seq):
    n, d = x2.shape
    tm = min(PROJ_TM, seq)
    per_seq = seq // tm
    assert sum(PROJ_WIDTHS) == w.shape[1]
    ncb = LRU_WIDTH // LANES
    rows = lambda width: pl.BlockSpec((tm, width), lambda i: (i, 0))
    full2 = lambda shape: pl.BlockSpec(shape, lambda i: (0, 0))
    pairs = pl.BlockSpec((LRU_PAIRS, LRU_PAIR, LRU_PAIR), lambda i: (0, 0, 0))
    rot = pl.BlockSpec((tm, RET_DK // 2), lambda i: (i % per_seq, 0))
    vec = lambda a: a.reshape(1, -1)
    out_widths = (2 * RET_QK, RET_V, RET_V, 2 * D_MODEL, LRU_WIDTH)
    return pl.pallas_call(
        functools.partial(_proj_lru_body, per_seq),
        grid=(n // tm,),
        in_specs=[rows(d),
                  pl.BlockSpec(w.shape, lambda i: (0, 0), pipeline_mode=pl.Buffered(1)),
                  rot, rot, full2((1, 2 * D_MODEL)),
                  full2((CONV_WIDTH, LRU_WIDTH)), full2((1, LRU_WIDTH)), pairs, pairs,
                  full2((1, LRU_WIDTH)), full2((1, LRU_WIDTH)), full2((1, LRU_WIDTH))],
        out_specs=[rows(width) for width in out_widths],
        out_shape=[jax.ShapeDtypeStruct((n, width), BF16) for width in out_widths],
        scratch_shapes=[
            pltpu.VMEM((tm + SUBLANES, LRU_WIDTH), F32),
            pltpu.VMEM((tm, LRU_WIDTH), F32),
            pltpu.VMEM((ncb, SUBLANES * _seg_pitch(tm), LANES), F32),
            pltpu.VMEM((ncb, SUBLANES * _seg_pitch(tm), LANES), F32),
            pltpu.VMEM((1, LRU_WIDTH), F32),
        ],
        compiler_params=_cparams(("arbitrary",)),
        name="proj_lru",
    )(x2, w, cos, sin, vec(b_merge), conv_w, vec(conv_b), _block_diag_pairs(w_r), _block_diag_pairs(w_i),
      vec(b_r), vec(b_i), vec(lam))


def _ret_body(dec_ref, q_ref, k_ref, v_ref, sg_ref, dm_ref, xi_ref, zeta_ref, o_ref, st_ref):
    c = pl.program_id(1)

    @pl.when(c == 0)
    def _():
        st_ref[...] = jnp.zeros_like(st_ref)

    for h in range(RET_HEADS):
        qc = slice(h * RET_DK, (h + 1) * RET_DK)
        vc = slice(h * RET_DV, (h + 1) * RET_DV)
        q, k, v = q_ref[:, qc], k_ref[:, qc], v_ref[:, vc]
        scores = lax.dot_general(q, k, (((1,), (1,)), ((), ())), preferred_element_type=F32) * dm_ref[h]
        inner = jnp.dot(scores.astype(BF16), v, preferred_element_type=F32)
        st = st_ref[h]
        cross = jnp.dot(q, st.astype(BF16), preferred_element_type=F32) * xi_ref[h]
        kz = (k.astype(F32) * zeta_ref[h]).astype(BF16)
        upd = lax.dot_general(kz, v, (((0,), (0,)), ((), ())), preferred_element_type=F32)
        st_ref[h] = st * dec_ref[h] + upd
        o = inner + cross
        mu = jnp.mean(o, axis=-1, keepdims=True)
        oc = o - mu
        var = jnp.mean(oc * oc, axis=-1, keepdims=True)
        on = oc * lax.rsqrt(var + LN_EPS)
        o_ref[:, vc] = (sg_ref[:, vc].astype(F32) * on).astype(o_ref.dtype)


def _retention(qk, v, sg, batch, seq):
    n = qk.shape[0]
    C = min(RET_CHUNK, seq)
    nc = seq // C
    H = RET_HEADS
    log_g = jnp.log1p(-(2.0 ** (-5.0 - jnp.arange(H, dtype=F32))))
    pos = jnp.arange(C, dtype=F32)
    diff = pos[:, None] - pos[None, :]
    causal = diff >= 0
    d_mask = jnp.where(causal[None], jnp.exp(log_g[:, None, None] * jnp.where(causal, diff, 0.0)[None]), 0.0)
    xi = jnp.exp(log_g[:, None] * (pos + 1.0)[None])[:, :, None]
    zeta = jnp.exp(log_g[:, None] * (C - 1.0 - pos)[None])[:, :, None]
    chunk_decay = jnp.exp(log_g * C)
    row = lambda b, c: b * nc + c
    full3 = lambda b, c: (0, 0, 0)
    return pl.pallas_call(
        _ret_body,
        grid=(batch, nc),
        in_specs=[
            pl.BlockSpec(memory_space=pltpu.SMEM),
            pl.BlockSpec((C, RET_QK), lambda b, c: (row(b, c), 0)),
            pl.BlockSpec((C, RET_QK), lambda b, c: (row(b, c), 1)),
            pl.BlockSpec((C, RET_V), lambda b, c: (row(b, c), 0)),
            pl.BlockSpec((C, RET_V), lambda b, c: (row(b, c), 0)),
            pl.BlockSpec((H, C, C), full3),
            pl.BlockSpec((H, C, 1), full3),
            pl.BlockSpec((H, C, 1), full3),
        ],
        out_specs=pl.BlockSpec((C, RET_V), lambda b, c: (row(b, c), 0)),
        out_shape=jax.ShapeDtypeStruct((n, RET_V), BF16),
        scratch_shapes=[pltpu.VMEM((H, RET_DK, RET_DV), F32)],
        compiler_params=_cparams(("parallel", "arbitrary")),
        name="retention",
    )(chunk_decay, qk, qk, v, sg, d_mask, xi, zeta)


def _layer_norm_rows(y, g, b):
    mu = jnp.mean(y, axis=-1, keepdims=True)
    yc = y - mu
    var = jnp.mean(yc * yc, axis=-1, keepdims=True)
    return yc * lax.rsqrt(var + LN_EPS) * g + b


def _merge_body(alpha, ret_ref, lru_ref, gm_ref, x_ref, wro_ref, wlo_ref, wo_ref, g1_ref, b1_ref,
                wrt_ref, brt_ref, x1_ref, rw_ref, lpc_ref, lpr_ref, cnt_ref):
    pr = jnp.dot(ret_ref[...], wro_ref[...], preferred_element_type=F32)
    pu = jnp.dot(lru_ref[...], wlo_ref[...], preferred_element_type=F32)
    merged = gm_ref[:, :D_MODEL].astype(F32) * pr + gm_ref[:, D_MODEL:].astype(F32) * pu
    y = alpha * x_ref[...] + jnp.dot(merged.astype(BF16), wo_ref[...], preferred_element_type=F32)
    x1 = _layer_norm_rows(y, g1_ref[...], b1_ref[...])
    x1_ref[...] = x1

    tm = x1.shape[0]
    x_hi = x1.astype(BF16)
    x_lo = (x1 - x_hi.astype(F32)).astype(BF16)
    hh = jnp.dot(x_hi, wrt_ref[...], preferred_element_type=F32)
    lh = jnp.dot(x_lo, wrt_ref[:, :LANES], preferred_element_type=F32)
    lg = hh[:, :LANES] + hh[:, LANES:] + lh + brt_ref[...]
    lane = lax.broadcasted_iota(I32, (tm, LANES), 1)
    big = jnp.int32(LANES)
    neg = jnp.float32(-jnp.inf)
    gmask = lane < N_GROUPS
    gl = jnp.where(gmask, lg, neg)
    gmax = jnp.max(gl, axis=-1, keepdims=True)
    g_idx = jnp.min(jnp.where(gmask & (gl == gmax), lane, big), axis=-1, keepdims=True)
    g_w = 1.0 / jnp.sum(jnp.where(gmask, jnp.exp(gl - gmax), 0.0), axis=-1, keepdims=True)
    e_lo = N_GROUPS + EXPERTS_PER_GROUP * g_idx
    emask = (lane >= e_lo) & (lane < e_lo + EXPERTS_PER_GROUP)
    el = jnp.where(emask, lg, neg)
    v1 = jnp.max(el, axis=-1, keepdims=True)
    i1 = jnp.min(jnp.where(emask & (el == v1), lane, big), axis=-1, keepdims=True)
    emask2 = emask & (lane != i1)
    el2 = jnp.where(emask2, lg, neg)
    v2 = jnp.max(el2, axis=-1, keepdims=True)
    i2 = jnp.min(jnp.where(emask2 & (el2 == v2), lane, big), axis=-1, keepdims=True)
    ex = jnp.exp(v2 - v1)
    den = 1.0 + ex
    w1 = g_w / den
    w2 = g_w * ex / den
    e1 = i1 - N_GROUPS
    e2 = i2 - N_GROUPS
    rw_ref[...] = jnp.where(lane == 0, w1, jnp.where(lane == 1, w2, 0.0))

    oh = (lane == e1).astype(F32) + (lane == e2).astype(F32)
    rowi = lax.broadcasted_iota(I32, (tm, tm), 0)
    coli = lax.broadcasted_iota(I32, (tm, tm), 1)
    tri = jnp.where(coli < rowi, 1.0, 0.0).astype(BF16)
    before = jnp.dot(tri, oh.astype(BF16), preferred_element_type=F32)
    cnt = jnp.sum(oh, axis=0, keepdims=True)
    units = jnp.floor((cnt + (SEG_ALIGN - 1.0)) * (1.0 / SEG_ALIGN))
    er = lax.broadcasted_iota(I32, (LANES, LANES), 0)
    ec = lax.broadcasted_iota(I32, (LANES, LANES), 1)
    upper = jnp.where(er < ec, 1.0, 0.0).astype(BF16)
    offs = SEG_ALIGN * jnp.dot(jnp.broadcast_to(units, (SUBLANES, LANES)).astype(BF16), upper,
                               preferred_element_type=F32)[0:1, :]
    pos = before + offs
    lp1 = jnp.sum(jnp.where(lane == e1, pos, 0.0), axis=-1, keepdims=True)
    lp2 = jnp.sum(jnp.where(lane == e2, pos, 0.0), axis=-1, keepdims=True)
    lpc = jnp.where(lane == 0, lp1, jnp.where(lane == 1, lp2, 0.0))
    lpc_ref[...] = lpc
    lpr_ref[...] = lpc.T[0:SUBLANES, :]
    cnt_ref[...] = jnp.broadcast_to(cnt, cnt_ref.shape)


def _merge(ret, lru, gm, x2, w_ret_o, w_lru_o, w_out, ln_g, ln_b, w_rt, b_rt, alpha):
    n = x2.shape[0]
    tm = min(ROUTE_TM, n)
    rowmap = lambda i: (i, 0)
    full = lambda i: (0, 0)
    return pl.pallas_call(
        functools.partial(_merge_body, alpha),
        grid=(n // tm,),
        in_specs=[
            pl.BlockSpec((tm, RET_V), rowmap),
            pl.BlockSpec((tm, LRU_WIDTH), rowmap),
            pl.BlockSpec((tm, 2 * D_MODEL), rowmap),
            pl.BlockSpec((tm, D_MODEL), rowmap),
            pl.BlockSpec((RET_V, D_MODEL), full),
            pl.BlockSpec((LRU_WIDTH, D_MODEL), full),
            pl.BlockSpec((D_MODEL, D_MODEL), full),
            pl.BlockSpec((1, D_MODEL), full),
            pl.BlockSpec((1, D_MODEL), full),
            pl.BlockSpec((D_MODEL, 2 * LANES), full),
            pl.BlockSpec((1, LANES), full),
        ],
        out_specs=[
            pl.BlockSpec((tm, D_MODEL), rowmap),
            pl.BlockSpec((tm, LANES), rowmap),
            pl.BlockSpec((tm, LANES), rowmap),
            pl.BlockSpec((SUBLANES, tm), lambda i: (0, i)),
            pl.BlockSpec((None, SUBLANES, LANES), lambda i: (i, 0, 0)),
        ],
        out_shape=[
            jax.ShapeDtypeStruct((n, D_MODEL), F32),
            jax.ShapeDtypeStruct((n, LANES), F32),
            jax.ShapeDtypeStruct((n, LANES), F32),
            jax.ShapeDtypeStruct((SUBLANES, n), F32),
            jax.ShapeDtypeStruct((n // tm, SUBLANES, LANES), F32),
        ],
        compiler_params=_cparams(("parallel",)),
        name="merge_ln_route",
    )(ret, lru, gm, x2, w_ret_o, w_lru_o, w_out, ln_g.reshape(1, -1), ln_b.reshape(1, -1), w_rt, b_rt)


def _run_lists(units, run_off, gdst):
    k = jnp.arange(N_EXPERTS, dtype=I32)
    cnts, offs, dsts = [], [], []
    for b in range(RUN_BITS):
        bit = (units >> b) & 1
        low = (units & ((1 << b) - 1)) * SEG_ALIGN
        pos = jnp.cumsum(bit, axis=1) - bit
        hit = (bit[:, None, :] == 1) & (pos[:, None, :] == k[None, :, None])
        offs.append(jnp.sum(jnp.where(hit, (run_off + low)[:, None, :], 0), axis=2))
        dsts.append(jnp.sum(jnp.where(hit, (gdst + low)[:, None, :], 0), axis=2))
        cnts.append(jnp.sum(bit, axis=1))
    flat = lambda parts: jnp.stack(parts, axis=1).reshape(-1).astype(I32)
    return flat(cnts), flat(offs), flat(dsts)


def _run_copies(lists, tile, loc_ref, glob_hbm, sem, to_global, wait):
    cnt_ref, off_ref, dst_ref = lists
    for b in range(RUN_BITS):
        rows = SEG_ALIGN << b
        base = tile * RUN_BITS + b

        def piece(k, carry):
            off = pl.multiple_of(off_ref[base * N_EXPERTS + k], SEG_ALIGN)
            dst = pl.multiple_of(dst_ref[base * N_EXPERTS + k], SEG_ALIGN)
            l = loc_ref.at[pl.ds(off, rows), :]
            g = glob_hbm.at[pl.ds(dst, rows), :]
            cp = pltpu.make_async_copy(l, g, sem) if to_global else pltpu.make_async_copy(g, l, sem)
            if wait:
                cp.wait()
            else:
                cp.start()
            return carry

        lax.fori_loop(0, cnt_ref[base], piece, 0)


def _zero_rows(start, units, max_units, zero_ref, xs_hbm, sem, wait):
    pos = start
    for b in range((max_units - 1).bit_length()):
        rows = SEG_ALIGN << b
        bit = lax.bitwise_and(lax.shift_right_logical(units, b), 1)

        @pl.when(bit == 1)
        def _():
            dst = xs_hbm.at[pl.ds(pl.multiple_of(pos, SEG_ALIGN), rows), :]
            cp = pltpu.make_async_copy(zero_ref.at[pl.ds(0, rows), :], dst, sem)
            if wait:
                cp.wait()
            else:
                cp.start()

        pos = pos + bit * rows


def _onehot_rows(lpr_ref, n_rows):
    tm = lpr_ref.shape[1]
    sub = lax.broadcasted_iota(I32, (n_rows, tm), 0)
    lp1 = lpr_ref[0:1, :].astype(I32)
    lp2 = lpr_ref[1:2, :].astype(I32)
    return jnp.where((sub == lp1) | (sub == lp2), 1.0, 0.0).astype(BF16)


def _dispatch_body(cnt_ref, off_ref, dst_ref, pad_start_ref, pad_units_ref, total_ref, x1_ref, lpr_ref,
                   xs_hbm, loc_ref, zero_ref, sems, zsem):
    i = pl.program_id(0)
    last = pl.num_programs(0) - 1
    slot = lax.rem(i, 2)
    lists = (cnt_ref, off_ref, dst_ref)
    perm = _onehot_rows(lpr_ref, loc_ref.shape[1])
    loc_ref[slot] = jnp.dot(perm, x1_ref[...].astype(BF16), preferred_element_type=F32)
    _run_copies(lists, i, loc_ref.at[slot], xs_hbm, sems.at[slot], True, False)

    @pl.when(i > 0)
    def _():
        _run_copies(lists, i - 1, loc_ref.at[1 - slot], xs_hbm, sems.at[1 - slot], True, True)

    @pl.when(i == last)
    def _():
        zero_ref[...] = jnp.zeros_like(zero_ref)
        max_units = zero_ref.shape[0] // SEG_ALIGN
        for wait in (False, True):
            def region(e, carry):
                _zero_rows(pad_start_ref[e], pad_units_ref[e], max_units, zero_ref, xs_hbm, zsem, wait)
                return carry
            lax.fori_loop(0, N_EXPERTS, region, 0)
            _zero_tail(total_ref[0], zero_ref, xs_hbm, zsem, wait)
        _run_copies(lists, i, loc_ref.at[slot], xs_hbm, sems.at[slot], True, True)


def _zero_tail(total, zero_ref, xs_hbm, sem, wait):
    zr = zero_ref.shape[0]
    shift = zr.bit_length() - 1
    assert zr == 1 << shift and xs_hbm.shape[0] % SEG_ALIGN == 0
    dead = xs_hbm.shape[0] - total
    n_full = lax.shift_right_logical(dead, shift)

    def full(k, carry):
        dst = xs_hbm.at[pl.ds(pl.multiple_of(total + k * zr, SEG_ALIGN), zr), :]
        cp = pltpu.make_async_copy(zero_ref, dst, sem)
        if wait:
            cp.wait()
        else:
            cp.start()
        return carry

    lax.fori_loop(0, n_full, full, 0)
    rem = lax.shift_right_logical(dead - n_full * zr, SEG_ALIGN.bit_length() - 1)
    _zero_rows(total + n_full * zr, rem, zr // SEG_ALIGN, zero_ref, xs_hbm, sem, wait)


def _local_rows(tm):
    return TOP_K * tm + N_EXPERTS * SEG_ALIGN


def _dispatch(lists, pad_start, pad_units, total, x1, lpr, m_max):
    n = x1.shape[0]
    tm = min(ROUTE_TM, n)
    return pl.pallas_call(
        _dispatch_body,
        grid_spec=pltpu.PrefetchScalarGridSpec(
            num_scalar_prefetch=6,
            grid=(n // tm,),
            in_specs=[pl.BlockSpec((tm, D_MODEL), lambda i, *_: (i, 0)),
                      pl.BlockSpec((SUBLANES, tm), lambda i, *_: (0, i))],
            out_specs=pl.BlockSpec(memory_space=pl.ANY),
            scratch_shapes=[pltpu.VMEM((2, _local_rows(tm), D_MODEL), F32),
                            pltpu.VMEM((MOE_RT, D_MODEL), F32),
                            pltpu.SemaphoreType.DMA((2,)), pltpu.SemaphoreType.DMA],
        ),
        out_shape=jax.ShapeDtypeStruct((m_max, D_MODEL), F32),
        compiler_params=_cparams(("arbitrary",)),
        name="dispatch",
    )(*lists, pad_start, pad_units, total, x1, lpr)


def _expert_body(exp_ref, live_ref, xs_ref, wg_ref, wu_ref, wd_ref, y_ref, wgb_ref, wub_ref, wdb_ref, cur_ref):
    g = pl.program_id(0)
    e = exp_ref[g]

    @pl.when(g == 0)
    def _():
        cur_ref[0] = -1

    @pl.when(g < live_ref[0])
    def _():
        @pl.when(cur_ref[0] != e)
        def _():
            wgb_ref[...] = wg_ref[...].astype(BF16)
            wub_ref[...] = wu_ref[...].astype(BF16)
            wdb_ref[...] = wd_ref[...].astype(BF16)
            cur_ref[0] = e

        xb = xs_ref[...].astype(BF16)
        hg = jnp.dot(xb, wgb_ref[...], preferred_element_type=F32)
        hu = jnp.dot(xb, wub_ref[...], preferred_element_type=F32)
        hm = (hg * jax.nn.sigmoid(hg) * hu).astype(BF16)
        y_ref[...] = jnp.dot(hm, wdb_ref[...], preferred_element_type=F32)


def _experts(tile_expert, n_live, xs, w_gate, w_up, w_down):
    m = xs.shape[0]
    rt = min(MOE_RT, m)
    rows = lambda g, e, n: (jnp.minimum(g, n[0] - 1), 0)
    return pl.pallas_call(
        _expert_body,
        grid_spec=pltpu.PrefetchScalarGridSpec(
            num_scalar_prefetch=2,
            grid=(m // rt,),
            in_specs=[
                pl.BlockSpec((rt, D_MODEL), rows),
                pl.BlockSpec((None, D_MODEL, D_EXPERT), lambda g, e, n: (e[g], 0, 0)),
                pl.BlockSpec((None, D_MODEL, D_EXPERT), lambda g, e, n: (e[g], 0, 0)),
                pl.BlockSpec((None, D_EXPERT, D_MODEL), lambda g, e, n: (e[g], 0, 0)),
            ],
            out_specs=pl.BlockSpec((rt, D_MODEL), rows),
            scratch_shapes=[
                pltpu.VMEM((D_MODEL, D_EXPERT), BF16),
                pltpu.VMEM((D_MODEL, D_EXPERT), BF16),
                pltpu.VMEM((D_EXPERT, D_MODEL), BF16),
                pltpu.SMEM((1,), I32),
            ],
        ),
        out_shape=jax.ShapeDtypeStruct((m, D_MODEL), F32),
        input_output_aliases={2: 0},
        compiler_params=_cparams(("arbitrary",)),
        name="experts",
    )(tile_expert, n_live, xs, w_gate, w_up, w_down)


def _combine_body(alpha, cnt_ref, off_ref, dst_ref, x1_ref, rw_ref, lpc_ref, g2_ref, b2_ref, ys_hbm, o_ref,
                  loc_ref, sems):
    i = pl.program_id(0)
    slot = lax.rem(i, 2)
    lists = (cnt_ref, off_ref, dst_ref)

    @pl.when(i == 0)
    def _():
        loc_ref[...] = jnp.zeros_like(loc_ref)
        _run_copies(lists, i, loc_ref.at[slot], ys_hbm, sems.at[slot], False, False)

    @pl.when(i + 1 < pl.num_programs(0))
    def _():
        _run_copies(lists, i + 1, loc_ref.at[1 - slot], ys_hbm, sems.at[1 - slot], False, False)

    _run_copies(lists, i, loc_ref.at[slot], ys_hbm, sems.at[slot], False, True)
    tm = x1_ref.shape[0]
    n_rows = loc_ref.shape[1]
    yb = loc_ref[slot].astype(BF16)
    lanes = lax.broadcasted_iota(I32, (tm, n_rows), 1)
    sel = (jnp.where(lanes == lpc_ref[:, 0:1].astype(I32), rw_ref[:, 0:1], 0.0)
           + jnp.where(lanes == lpc_ref[:, 1:2].astype(I32), rw_ref[:, 1:2], 0.0))
    moe = jnp.dot(sel.astype(BF16), yb, preferred_element_type=F32)
    y = alpha * x1_ref[...] + moe
    o_ref[...] = _layer_norm_rows(y, g2_ref[...], b2_ref[...])


def _combine(lists, x1, rw, lpc, ln_g, ln_b, ys, alpha):
    n = x1.shape[0]
    tm = min(ROUTE_TM, n)
    rowmap = lambda i, *_: (i, 0)
    full = lambda i, *_: (0, 0)
    return pl.pallas_call(
        functools.partial(_combine_body, alpha),
        grid_spec=pltpu.PrefetchScalarGridSpec(
            num_scalar_prefetch=3,
            grid=(n // tm,),
            in_specs=[
                pl.BlockSpec((tm, D_MODEL), rowmap),
                pl.BlockSpec((tm, LANES), rowmap),
                pl.BlockSpec((tm, LANES), rowmap),
                pl.BlockSpec((1, D_MODEL), full),
                pl.BlockSpec((1, D_MODEL), full),
                pl.BlockSpec(memory_space=pl.ANY),
            ],
            out_specs=pl.BlockSpec((tm, D_MODEL), rowmap),
            scratch_shapes=[pltpu.VMEM((2, _local_rows(tm), D_MODEL), F32), pltpu.SemaphoreType.DMA((2,))],
        ),
        out_shape=jax.ShapeDtypeStruct((n, D_MODEL), F32),
        compiler_params=_cparams(("arbitrary",)),
        name="combine_ln",
    )(*lists, x1, rw, lpc, ln_g.reshape(1, -1), ln_b.reshape(1, -1), ys)


def _rotary_tables(seq):
    half = RET_DK // 2
    inv = ROPE_BASE ** (-jnp.arange(half, dtype=F32) / half)
    ang = jnp.arange(seq, dtype=F32)[:, None] * inv[None, :]
    return jnp.cos(ang), jnp.sin(ang)


def _router_weights(w_group, b_group, w_exp_router, b_exp_router):
    spare = LANES - N_GROUPS - N_EXPERTS
    w = jnp.pad(jnp.concatenate([w_group, w_exp_router], axis=1), ((0, 0), (0, spare)))
    b = jnp.pad(jnp.concatenate([b_group, b_exp_router]), (0, spare)).reshape(1, LANES)
    w_hi = w.astype(BF16)
    w_lo = (w - w_hi.astype(F32)).astype(BF16)
    return jnp.concatenate([w_hi, w_lo], axis=1), b


def _layer(x, depth, w_in, b_merge, conv_w, conv_b, w_rg_r, b_rg_r, w_rg_i, b_rg_i, lru_lambda,
           w_ret_o, w_lru_o, w_out, ln1_g, ln1_b, w_group, b_group, w_exp_router, b_exp_router,
           w_e_gate, w_e_up, w_e_down, ln2_g, ln2_b):
    B, S, D = x.shape
    n = B * S
    alpha = (2.0 * depth) ** 0.25
    x2 = x.reshape(n, D)
    wb = w_in.astype(BF16)
    cos, sin = _rotary_tables(S)
    qk, v, sg, gm, lru = _proj_lru(x2, wb, cos, sin, b_merge, conv_w, conv_b, w_rg_r, b_rg_r, w_rg_i, b_rg_i,
                                   lru_lambda, S)

    ret = _retention(qk, v, sg, B, S)

    w_rt, b_rt = _router_weights(w_group, b_group, w_exp_router, b_exp_router)
    x1, rw, lpc, lpr, tcnt = _merge(ret, lru, gm, x2, w_ret_o.astype(BF16), w_lru_o.astype(BF16),
                                    w_out.astype(BF16), ln1_g, ln1_b, w_rt, b_rt, alpha)

    tm = min(ROUTE_TM, n)
    n_t = n // tm
    m_max = n_t * _local_rows(tm) + N_EXPERTS * MOE_RT
    rt = min(MOE_RT, m_max)
    cnt = tcnt[:, 0, :N_EXPERTS].astype(I32)
    units = (cnt + (SEG_ALIGN - 1)) // SEG_ALIGN
    run = units * SEG_ALIGN
    sizes = jnp.sum(run, axis=0)
    region = (sizes + (rt - 1)) // rt * rt
    e_end = jnp.cumsum(region)
    e_start = e_end - region
    gdst = e_start[None, :] + jnp.cumsum(run, axis=0) - run
    run_off = jnp.cumsum(run, axis=1) - run
    lists = _run_lists(units, run_off, gdst)
    pad_start = (e_start + sizes).astype(I32)
    pad_units = ((region - sizes) // SEG_ALIGN).astype(I32)
    total = e_end[-1:].astype(I32)
    tile_start = jnp.arange(m_max // rt, dtype=I32) * rt
    n_live = total // rt
    tile_expert = jnp.sum((e_end[None, :] <= jnp.minimum(tile_start, total - rt)[:, None]).astype(I32), axis=1)

    xs = _dispatch(lists, pad_start, pad_units, total, x1, lpr, m_max)
    ys = _experts(tile_expert, n_live, xs, w_e_gate, w_e_up, w_e_down)
    out = _combine(lists, x1, rw, lpc, ln2_g, ln2_b, ys, alpha)
    return out.reshape(B, S, D)


def kernel(x, w_in, b_merge, conv_w, conv_b, w_rg_r, b_rg_r, w_rg_i, b_rg_i, lru_lambda, w_ret_o, w_lru_o, w_out, ln1_g, ln1_b, w_group, b_group, w_exp_router, b_exp_router, w_e_gate, w_e_up, w_e_down, ln2_g, ln2_b):
    depth = w_in.shape[0]
    for l in range(depth):
        x = _layer(x, depth, w_in[l], b_merge[l], conv_w[l], conv_b[l], w_rg_r[l], b_rg_r[l], w_rg_i[l],
                   b_rg_i[l], lru_lambda[l], w_ret_o[l], w_lru_o[l], w_out[l], ln1_g[l], ln1_b[l],
                   w_group[l], b_group[l], w_exp_router[l], b_exp_router[l], w_e_gate[l], w_e_up[l],
                   w_e_down[l], ln2_g[l], ln2_b[l])
    return x
```

```python
import functools

import jax
import jax.numpy as jnp
import numpy as np
from jax import lax
from jax.experimental import pallas as pl
from jax.experimental.pallas import tpu as pltpu

F32 = jnp.float32
BF16 = jnp.bfloat16
I32 = jnp.int32

D_MODEL = 1024
RET_HEADS = 4
RET_DK = 256
RET_DV = 512
RET_QK = RET_HEADS * RET_DK
RET_V = RET_HEADS * RET_DV
ROPE_BASE = 10000.0
LRU_WIDTH = 1536
LRU_BLOCKS = 8
LRU_BLOCK = LRU_WIDTH // LRU_BLOCKS
LRU_PAIR = 2 * LRU_BLOCK
LRU_PAIRS = LRU_BLOCKS // 2
CONV_WIDTH = 4
LRU_C = 8.0
N_GROUPS = 4
EXPERTS_PER_GROUP = 8
N_EXPERTS = N_GROUPS * EXPERTS_PER_GROUP
TOP_K = 2
D_EXPERT = 512
LN_EPS = 1e-5

LANES = 128
SUBLANES = 8
VMEM_LIMIT = 56 * 1024 * 1024

PROJ_TM = 256
PROJ_CHUNK = 512
ROT_SPLIT = 64
ROUTE_TM = 512
MOE_RT = 512
SEG_ALIGN = SUBLANES
RUN_BITS = (ROUTE_TM // SEG_ALIGN).bit_length()


def _cparams(sem):
    return pltpu.CompilerParams(dimension_semantics=sem, vmem_limit_bytes=VMEM_LIMIT)


PROJ_WIDTHS = (2 * RET_QK, RET_V, RET_V, LRU_WIDTH, LRU_WIDTH, 2 * D_MODEL)


def _seg_pitch(t):
    g = t // SUBLANES
    units = -(-g // SUBLANES)
    return SUBLANES * (units + 1 - units % 2)


def _proj_lru_body(per_seq, dec_ref, x_ref, w_ref, cos_ref, sin_ref, bm_ref, cw_ref, cb_ref, wr_ref, wi_ref,
                   br_ref, bi_ref, lam_ref, dm_ref, xi_ref, zeta_ref, gm_ref, lru_ref, ret_ref,
                   ubuf_ref, gbuf_ref, a_ref, b_ref, carry_ref, st_ref):
    T = x_ref.shape[0]
    G = T // SUBLANES
    P = _seg_pitch(T)
    ncb = LRU_WIDTH // LANES
    o_qk, o_v, o_sg, o_u, o_gl, o_gm = (int(o) for o in np.cumsum((0,) + PROJ_WIDTHS[:-1]))
    xb = x_ref[...].astype(BF16)

    def seg(start, lo, hi):
        return jnp.dot(xb, w_ref[:, start + lo:start + hi], preferred_element_type=F32)

    @pl.when(lax.rem(pl.program_id(0), per_seq) == 0)
    def _():
        ubuf_ref[0:SUBLANES, :] = jnp.zeros((SUBLANES, LRU_WIDTH), F32)
        carry_ref[...] = jnp.zeros_like(carry_ref)
        st_ref[...] = jnp.zeros_like(st_ref)

    for lo in range(0, LRU_WIDTH, PROJ_CHUNK):
        hi = lo + PROJ_CHUNK
        ubuf_ref[SUBLANES:SUBLANES + T, lo:hi] = seg(o_u, lo, hi)
        gbuf_ref[:, lo:hi] = jax.nn.gelu(seg(o_gl, lo, hi))

    cw = cw_ref[...]
    uc = cb_ref[...] + cw[CONV_WIDTH - 1:CONV_WIDTH, :] * ubuf_ref[SUBLANES:SUBLANES + T, :]
    for j in range(CONV_WIDTH - 1):
        back = CONV_WIDTH - 1 - j
        uc = uc + cw[j:j + 1, :] * ubuf_ref[SUBLANES - back:SUBLANES - back + T, :]
    ubuf_ref[0:SUBLANES, :] = ubuf_ref[T:T + SUBLANES, :]

    ucb = uc.astype(BF16)
    neg_c_sp = -LRU_C * jax.nn.softplus(-lam_ref[...])
    for p in range(LRU_PAIRS):
        lo = p * LRU_PAIR
        sl = ucb[:, lo:lo + LRU_PAIR]
        r = jax.nn.sigmoid(jnp.dot(sl, wr_ref[p], preferred_element_type=F32) + br_ref[:, lo:lo + LRU_PAIR])
        i = jax.nn.sigmoid(jnp.dot(sl, wi_ref[p], preferred_element_type=F32) + bi_ref[:, lo:lo + LRU_PAIR])
        log_a = r * neg_c_sp[:, lo:lo + LRU_PAIR]
        a = jnp.exp(log_a)
        inp = jnp.sqrt(-jnp.tanh(log_a) * (a * a + 1.0)) * (i * uc[:, lo:lo + LRU_PAIR])
        for cc in range(LRU_PAIR // LANES):
            cb = p * (LRU_PAIR // LANES) + cc
            for s in range(SUBLANES):
                a_ref[cb, s * P:s * P + G, :] = a[s * G:(s + 1) * G, cc * LANES:(cc + 1) * LANES]
                b_ref[cb, s * P:s * P + G, :] = inp[s * G:(s + 1) * G, cc * LANES:(cc + 1) * LANES]

    half = RET_DK // 2
    cos, sin = cos_ref[...], sin_ref[...]

    def rotary(start, h, scale):
        acc = seg(start, h * RET_DK, (h + 1) * RET_DK)
        t1, t2 = acc[:, :half], acc[:, half:]
        return (jnp.concatenate([t1 * cos - t2 * sin, t1 * sin + t2 * cos], axis=1) * scale).astype(BF16)

    for h in range(RET_HEADS):
        vc = slice(h * RET_DV, (h + 1) * RET_DV)
        q = rotary(o_qk, h, 1.0)
        k = rotary(o_qk + RET_QK, h, RET_DK ** -0.5)
        v = seg(o_v, h * RET_DV, (h + 1) * RET_DV).astype(BF16)
        g = seg(o_sg, h * RET_DV, (h + 1) * RET_DV)
        scores = lax.dot_general(q, k, (((1,), (1,)), ((), ())), preferred_element_type=F32) * dm_ref[h]
        inner = jnp.dot(scores.astype(BF16), v, preferred_element_type=F32)
        st = st_ref[h]
        cross = jnp.dot(q, st.astype(BF16), preferred_element_type=F32) * xi_ref[h]
        kz = (k.astype(F32) * zeta_ref[h]).astype(BF16)
        upd = lax.dot_general(kz, v, (((0,), (0,)), ((), ())), preferred_element_type=F32)
        st_ref[h] = st * dec_ref[h] + upd
        o = inner + cross
        mu = jnp.mean(o, axis=-1, keepdims=True)
        oc = o - mu
        var = jnp.mean(oc * oc, axis=-1, keepdims=True)
        ret_ref[:, vc] = (g * jax.nn.sigmoid(g) * (oc * lax.rsqrt(var + LN_EPS))).astype(ret_ref.dtype)

    for lo in range(0, 2 * D_MODEL, PROJ_CHUNK):
        hi = lo + PROJ_CHUNK
        gm_ref[:, lo:hi] = jax.nn.sigmoid(seg(o_gm, lo, hi) + bm_ref[:, lo:hi]).astype(gm_ref.dtype)

    def step(j, hp):
        hs, ps = hp
        nh, npr = [], []
        for cb in range(ncb):
            a = a_ref[cb, pl.ds(j, SUBLANES, stride=P), :]
            b = b_ref[cb, pl.ds(j, SUBLANES, stride=P), :]
            hn = a * hs[cb] + b
            pn = a * ps[cb]
            b_ref[cb, pl.ds(j, SUBLANES, stride=P), :] = hn
            a_ref[cb, pl.ds(j, SUBLANES, stride=P), :] = pn
            nh.append(hn)
            npr.append(pn)
        return tuple(nh), tuple(npr)

    zeros = tuple(jnp.zeros((SUBLANES, LANES), F32) for _ in range(ncb))
    ones = tuple(jnp.ones((SUBLANES, LANES), F32) for _ in range(ncb))
    h_end, p_end = lax.fori_loop(0, G, step, (zeros, ones))

    for cb in range(ncb):
        cin = carry_ref[:, cb * LANES:(cb + 1) * LANES]
        for s in range(SUBLANES):
            rows = slice(s * G, (s + 1) * G)
            hseg = b_ref[cb, s * P:s * P + G, :] + a_ref[cb, s * P:s * P + G, :] * cin
            gate = gbuf_ref[rows, cb * LANES:(cb + 1) * LANES]
            lru_ref[rows, cb * LANES:(cb + 1) * LANES] = (gate * hseg).astype(lru_ref.dtype)
            cin = h_end[cb][s:s + 1, :] + p_end[cb][s:s + 1, :] * cin
        carry_ref[:, cb * LANES:(cb + 1) * LANES] = cin


def _block_diag_pairs(w):
    w4 = w.reshape(LRU_PAIRS, 2, LRU_BLOCK, LRU_BLOCK)
    z = jnp.zeros_like(w4[:, 0])
    top = jnp.concatenate([w4[:, 0], z], axis=2)
    bottom = jnp.concatenate([z, w4[:, 1]], axis=2)
    return jnp.concatenate([top, bottom], axis=1).astype(BF16)


def _retention_tables(chunk):
    H = RET_HEADS
    log_g = jnp.log1p(-(2.0 ** (-5.0 - jnp.arange(H, dtype=F32))))
    pos = jnp.arange(chunk, dtype=F32)
    diff = pos[:, None] - pos[None, :]
    causal = diff >= 0
    d_mask = jnp.where(causal[None], jnp.exp(log_g[:, None, None] * jnp.where(causal, diff, 0.0)[None]), 0.0)
    xi = jnp.exp(log_g[:, None] * (pos + 1.0)[None])[:, :, None]
    zeta = jnp.exp(log_g[:, None] * (chunk - 1.0 - pos)[None])[:, :, None]
    return d_mask, xi, zeta, jnp.exp(log_g * chunk)


def _proj_lru(x2, w, cos, sin, b_merge, conv_w, conv_b, w_r, b_r, w_i, b_i, lam, seq):
    n, d = x2.shape
    tm = min(PROJ_TM, seq)
    per_seq = seq // tm
    assert sum(PROJ_WIDTHS) == w.shape[1]
    ncb = LRU_WIDTH // LANES
    d_mask, xi, zeta, chunk_decay = _retention_tables(tm)
    rows = lambda width: pl.BlockSpec((tm, width), lambda i: (i, 0))
    full2 = lambda shape: pl.BlockSpec(shape, lambda i: (0, 0))
    full3 = lambda shape: pl.BlockSpec(shape, lambda i: (0, 0, 0))
    pairs = full3((LRU_PAIRS, LRU_PAIR, LRU_PAIR))
    rot = pl.BlockSpec((tm, RET_DK // 2), lambda i: (i % per_seq, 0))
    vec = lambda a: a.reshape(1, -1)
    out_widths = (2 * D_MODEL, LRU_WIDTH, RET_V)
    return pl.pallas_call(
        functools.partial(_proj_lru_body, per_seq),
        grid=(n // tm,),
        in_specs=[pl.BlockSpec(memory_space=pltpu.SMEM),
                  rows(d),
                  pl.BlockSpec(w.shape, lambda i: (0, 0), pipeline_mode=pl.Buffered(1)),
                  rot, rot, full2((1, 2 * D_MODEL)),
                  full2((CONV_WIDTH, LRU_WIDTH)), full2((1, LRU_WIDTH)), pairs, pairs,
                  full2((1, LRU_WIDTH)), full2((1, LRU_WIDTH)), full2((1, LRU_WIDTH)),
                  full3((RET_HEADS, tm, tm)), full3((RET_HEADS, tm, 1)), full3((RET_HEADS, tm, 1))],
        out_specs=[rows(width) for width in out_widths],
        out_shape=[jax.ShapeDtypeStruct((n, width), BF16) for width in out_widths],
        scratch_shapes=[
            pltpu.VMEM((tm + SUBLANES, LRU_WIDTH), F32),
            pltpu.VMEM((tm, LRU_WIDTH), F32),
            pltpu.VMEM((ncb, SUBLANES * _seg_pitch(tm), LANES), F32),
            pltpu.VMEM((ncb, SUBLANES * _seg_pitch(tm), LANES), F32),
            pltpu.VMEM((1, LRU_WIDTH), F32),
            pltpu.VMEM((RET_HEADS, RET_DK, RET_DV), F32),
        ],
        compiler_params=_cparams(("arbitrary",)),
        name="proj_mixers",
    )(chunk_decay, x2, w, cos, sin, vec(b_merge), conv_w, vec(conv_b), _block_diag_pairs(w_r),
      _block_diag_pairs(w_i), vec(b_r), vec(b_i), vec(lam), d_mask, xi, zeta)


def _layer_norm_rows(y, g, b):
    mu = jnp.mean(y, axis=-1, keepdims=True)
    yc = y - mu
    var = jnp.mean(yc * yc, axis=-1, keepdims=True)
    return yc * lax.rsqrt(var + LN_EPS) * g + b


def _merge_body(alpha, ret_ref, lru_ref, gm_ref, x_ref, wro_ref, wlo_ref, wo_ref, g1_ref, b1_ref,
                wrt_ref, brt_ref, x1_ref, rw_ref, lpc_ref, lpr_ref, cnt_ref):
    pr = jnp.dot(ret_ref[...], wro_ref[...], preferred_element_type=F32)
    pu = jnp.dot(lru_ref[...], wlo_ref[...], preferred_element_type=F32)
    merged = gm_ref[:, :D_MODEL].astype(F32) * pr + gm_ref[:, D_MODEL:].astype(F32) * pu
    y = alpha * x_ref[...] + jnp.dot(merged.astype(BF16), wo_ref[...], preferred_element_type=F32)
    x1 = _layer_norm_rows(y, g1_ref[...], b1_ref[...])
    x1_ref[...] = x1

    tm = x1.shape[0]
    x_hi = x1.astype(BF16)
    x_lo = (x1 - x_hi.astype(F32)).astype(BF16)
    hh = jnp.dot(x_hi, wrt_ref[...], preferred_element_type=F32)
    lh = jnp.dot(x_lo, wrt_ref[:, :LANES], preferred_element_type=F32)
    lg = hh[:, :LANES] + hh[:, LANES:] + lh + brt_ref[...]
    lane = lax.broadcasted_iota(I32, (tm, LANES), 1)
    big = jnp.int32(LANES)
    neg = jnp.float32(-jnp.inf)
    gmask = lane < N_GROUPS
    gl = jnp.where(gmask, lg, neg)
    gmax = jnp.max(gl, axis=-1, keepdims=True)
    g_idx = jnp.min(jnp.where(gmask & (gl == gmax), lane, big), axis=-1, keepdims=True)
    g_w = 1.0 / jnp.sum(jnp.where(gmask, jnp.exp(gl - gmax), 0.0), axis=-1, keepdims=True)
    e_lo = N_GROUPS + EXPERTS_PER_GROUP * g_idx
    emask = (lane >= e_lo) & (lane < e_lo + EXPERTS_PER_GROUP)
    el = jnp.where(emask, lg, neg)
    v1 = jnp.max(el, axis=-1, keepdims=True)
    i1 = jnp.min(jnp.where(emask & (el == v1), lane, big), axis=-1, keepdims=True)
    emask2 = emask & (lane != i1)
    el2 = jnp.where(emask2, lg, neg)
    v2 = jnp.max(el2, axis=-1, keepdims=True)
    i2 = jnp.min(jnp.where(emask2 & (el2 == v2), lane, big), axis=-1, keepdims=True)
    ex = jnp.exp(v2 - v1)
    den = 1.0 + ex
    w1 = g_w / den
    w2 = g_w * ex / den
    e1 = i1 - N_GROUPS
    e2 = i2 - N_GROUPS
    rw_ref[...] = jnp.where(lane == 0, w1, jnp.where(lane == 1, w2, 0.0))

    oh = (lane == e1).astype(F32) + (lane == e2).astype(F32)
    rowi = lax.broadcasted_iota(I32, (tm, tm), 0)
    coli = lax.broadcasted_iota(I32, (tm, tm), 1)
    tri = jnp.where(coli < rowi, 1.0, 0.0).astype(BF16)
    before = jnp.dot(tri, oh.astype(BF16), preferred_element_type=F32)
    cnt = jnp.sum(oh, axis=0, keepdims=True)
    units = jnp.floor((cnt + (SEG_ALIGN - 1.0)) * (1.0 / SEG_ALIGN))
    er = lax.broadcasted_iota(I32, (LANES, LANES), 0)
    ec = lax.broadcasted_iota(I32, (LANES, LANES), 1)
    upper = jnp.where(er < ec, 1.0, 0.0).astype(BF16)
    offs = SEG_ALIGN * jnp.dot(jnp.broadcast_to(units, (SUBLANES, LANES)).astype(BF16), upper,
                               preferred_element_type=F32)[0:1, :]
    pos = before + offs
    lp1 = jnp.sum(jnp.where(lane == e1, pos, 0.0), axis=-1, keepdims=True)
    lp2 = jnp.sum(jnp.where(lane == e2, pos, 0.0), axis=-1, keepdims=True)
    lpc = jnp.where(lane == 0, lp1, jnp.where(lane == 1, lp2, 0.0))
    lpc_ref[...] = lpc
    lpr_ref[...] = lpc.T[0:SUBLANES, :]
    cnt_ref[...] = jnp.broadcast_to(cnt, cnt_ref.shape)


def _merge(ret, lru, gm, x2, w_ret_o, w_lru_o, w_out, ln_g, ln_b, w_rt, b_rt, alpha):
    n = x2.shape[0]
    tm = min(ROUTE_TM, n)
    rowmap = lambda i: (i, 0)
    full = lambda i: (0, 0)
    return pl.pallas_call(
        functools.partial(_merge_body, alpha),
        grid=(n // tm,),
        in_specs=[
            pl.BlockSpec((tm, RET_V), rowmap),
            pl.BlockSpec((tm, LRU_WIDTH), rowmap),
            pl.BlockSpec((tm, 2 * D_MODEL), rowmap),
            pl.BlockSpec((tm, D_MODEL), rowmap),
            pl.BlockSpec((RET_V, D_MODEL), full),
            pl.BlockSpec((LRU_WIDTH, D_MODEL), full),
            pl.BlockSpec((D_MODEL, D_MODEL), full),
            pl.BlockSpec((1, D_MODEL), full),
            pl.BlockSpec((1, D_MODEL), full),
            pl.BlockSpec((D_MODEL, 2 * LANES), full),
            pl.BlockSpec((1, LANES), full),
        ],
        out_specs=[
            pl.BlockSpec((tm, D_MODEL), rowmap),
            pl.BlockSpec((tm, LANES), rowmap),
            pl.BlockSpec((tm, LANES), rowmap),
            pl.BlockSpec((SUBLANES, tm), lambda i: (0, i)),
            pl.BlockSpec((None, SUBLANES, LANES), lambda i: (i, 0, 0)),
        ],
        out_shape=[
            jax.ShapeDtypeStruct((n, D_MODEL), F32),
            jax.ShapeDtypeStruct((n, LANES), F32),
            jax.ShapeDtypeStruct((n, LANES), F32),
            jax.ShapeDtypeStruct((SUBLANES, n), F32),
            jax.ShapeDtypeStruct((n // tm, SUBLANES, LANES), F32),
        ],
        compiler_params=_cparams(("parallel",)),
        name="merge_ln_route",
    )(ret, lru, gm, x2, w_ret_o, w_lru_o, w_out, ln_g.reshape(1, -1), ln_b.reshape(1, -1), w_rt, b_rt)


def _run_lists(units, run_off, gdst):
    k = jnp.arange(N_EXPERTS, dtype=I32)
    cnts, offs, dsts = [], [], []
    for b in range(RUN_BITS):
        bit = (units >> b) & 1
        low = (units & ((1 << b) - 1)) * SEG_ALIGN
        pos = jnp.cumsum(bit, axis=1) - bit
        hit = (bit[:, None, :] == 1) & (pos[:, None, :] == k[None, :, None])
        offs.append(jnp.sum(jnp.where(hit, (run_off + low)[:, None, :], 0), axis=2))
        dsts.append(jnp.sum(jnp.where(hit, (gdst + low)[:, None, :], 0), axis=2))
        cnts.append(jnp.sum(bit, axis=1))
    flat = lambda parts: jnp.stack(parts, axis=1).reshape(-1).astype(I32)
    return flat(cnts), flat(offs), flat(dsts)


def _run_copies(lists, tile, loc_ref, glob_hbm, sem, to_global, wait):
    cnt_ref, off_ref, dst_ref = lists
    for b in range(RUN_BITS):
        rows = SEG_ALIGN << b
        base = tile * RUN_BITS + b

        def piece(k, carry):
            off = pl.multiple_of(off_ref[base * N_EXPERTS + k], SEG_ALIGN)
            dst = pl.multiple_of(dst_ref[base * N_EXPERTS + k], SEG_ALIGN)
            l = loc_ref.at[pl.ds(off, rows), :]
            g = glob_hbm.at[pl.ds(dst, rows), :]
            cp = pltpu.make_async_copy(l, g, sem) if to_global else pltpu.make_async_copy(g, l, sem)
            if wait:
                cp.wait()
            else:
                cp.start()
            return carry

        lax.fori_loop(0, cnt_ref[base], piece, 0)


def _zero_rows(start, units, max_units, zero_ref, xs_hbm, sem, wait):
    pos = start
    for b in range((max_units - 1).bit_length()):
        rows = SEG_ALIGN << b
        bit = lax.bitwise_and(lax.shift_right_logical(units, b), 1)

        @pl.when(bit == 1)
        def _():
            dst = xs_hbm.at[pl.ds(pl.multiple_of(pos, SEG_ALIGN), rows), :]
            cp = pltpu.make_async_copy(zero_ref.at[pl.ds(0, rows), :], dst, sem)
            if wait:
                cp.wait()
            else:
                cp.start()

        pos = pos + bit * rows


def _onehot_rows(lpr_ref, n_rows):
    tm = lpr_ref.shape[1]
    sub = lax.broadcasted_iota(I32, (n_rows, tm), 0)
    lp1 = lpr_ref[0:1, :].astype(I32)
    lp2 = lpr_ref[1:2, :].astype(I32)
    return jnp.where((sub == lp1) | (sub == lp2), 1.0, 0.0).astype(BF16)


def _dispatch_body(cnt_ref, off_ref, dst_ref, pad_start_ref, pad_units_ref, total_ref, x1_ref, lpr_ref,
                   xs_hbm, loc_ref, zero_ref, sems, zsem):
    i = pl.program_id(0)
    last = pl.num_programs(0) - 1
    slot = lax.rem(i, 2)
    lists = (cnt_ref, off_ref, dst_ref)
    perm = _onehot_rows(lpr_ref, loc_ref.shape[1])
    loc_ref[slot] = jnp.dot(perm, x1_ref[...].astype(BF16), preferred_element_type=F32)
    _run_copies(lists, i, loc_ref.at[slot], xs_hbm, sems.at[slot], True, False)

    @pl.when(i > 0)
    def _():
        _run_copies(lists, i - 1, loc_ref.at[1 - slot], xs_hbm, sems.at[1 - slot], True, True)

    @pl.when(i == last)
    def _():
        zero_ref[...] = jnp.zeros_like(zero_ref)
        max_units = zero_ref.shape[0] // SEG_ALIGN
        for wait in (False, True):
            def region(e, carry):
                _zero_rows(pad_start_ref[e], pad_units_ref[e], max_units, zero_ref, xs_hbm, zsem, wait)
                return carry
            lax.fori_loop(0, N_EXPERTS, region, 0)
            _zero_tail(total_ref[0], zero_ref, xs_hbm, zsem, wait)
        _run_copies(lists, i, loc_ref.at[slot], xs_hbm, sems.at[slot], True, True)


def _zero_tail(total, zero_ref, xs_hbm, sem, wait):
    zr = zero_ref.shape[0]
    shift = zr.bit_length() - 1
    assert zr == 1 << shift and xs_hbm.shape[0] % SEG_ALIGN == 0
    dead = xs_hbm.shape[0] - total
    n_full = lax.shift_right_logical(dead, shift)

    def full(k, carry):
        dst = xs_hbm.at[pl.ds(pl.multiple_of(total + k * zr, SEG_ALIGN), zr), :]
        cp = pltpu.make_async_copy(zero_ref, dst, sem)
        if wait:
            cp.wait()
        else:
            cp.start()
        return carry

    lax.fori_loop(0, n_full, full, 0)
    rem = lax.shift_right_logical(dead - n_full * zr, SEG_ALIGN.bit_length() - 1)
    _zero_rows(total + n_full * zr, rem, zr // SEG_ALIGN, zero_ref, xs_hbm, sem, wait)


def _local_rows(tm):
    return TOP_K * tm + N_EXPERTS * SEG_ALIGN


def _dispatch(lists, pad_start, pad_units, total, x1, lpr, m_max):
    n = x1.shape[0]
    tm = min(ROUTE_TM, n)
    return pl.pallas_call(
        _dispatch_body,
        grid_spec=pltpu.PrefetchScalarGridSpec(
            num_scalar_prefetch=6,
            grid=(n // tm,),
            in_specs=[pl.BlockSpec((tm, D_MODEL), lambda i, *_: (i, 0)),
                      pl.BlockSpec((SUBLANES, tm), lambda i, *_: (0, i))],
            out_specs=pl.BlockSpec(memory_space=pl.ANY),
            scratch_shapes=[pltpu.VMEM((2, _local_rows(tm), D_MODEL), F32),
                            pltpu.VMEM((MOE_RT, D_MODEL), F32),
                            pltpu.SemaphoreType.DMA((2,)), pltpu.SemaphoreType.DMA],
        ),
        out_shape=jax.ShapeDtypeStruct((m_max, D_MODEL), F32),
        compiler_params=_cparams(("arbitrary",)),
        name="dispatch",
    )(*lists, pad_start, pad_units, total, x1, lpr)


def _expert_body(exp_ref, live_ref, xs_ref, wg_ref, wu_ref, wd_ref, y_ref, wgb_ref, wub_ref, wdb_ref, cur_ref):
    g = pl.program_id(0)
    e = exp_ref[g]

    @pl.when(g == 0)
    def _():
        cur_ref[0] = -1

    @pl.when(g < live_ref[0])
    def _():
        @pl.when(cur_ref[0] != e)
        def _():
            wgb_ref[...] = wg_ref[...].astype(BF16)
            wub_ref[...] = wu_ref[...].astype(BF16)
            wdb_ref[...] = wd_ref[...].astype(BF16)
            cur_ref[0] = e

        xb = xs_ref[...].astype(BF16)
        hg = jnp.dot(xb, wgb_ref[...], preferred_element_type=F32)
        hu = jnp.dot(xb, wub_ref[...], preferred_element_type=F32)
        hm = (hg * jax.nn.sigmoid(hg) * hu).astype(BF16)
        y_ref[...] = jnp.dot(hm, wdb_ref[...], preferred_element_type=F32)


def _experts(tile_expert, n_live, xs, w_gate, w_up, w_down):
    m = xs.shape[0]
    rt = min(MOE_RT, m)
    rows = lambda g, e, n: (jnp.minimum(g, n[0] - 1), 0)
    return pl.pallas_call(
        _expert_body,
        grid_spec=pltpu.PrefetchScalarGridSpec(
            num_scalar_prefetch=2,
            grid=(m // rt,),
            in_specs=[
                pl.BlockSpec((rt, D_MODEL), rows),
                pl.BlockSpec((None, D_MODEL, D_EXPERT), lambda g, e, n: (e[g], 0, 0)),
                pl.BlockSpec((None, D_MODEL, D_EXPERT), lambda g, e, n: (e[g], 0, 0)),
                pl.BlockSpec((None, D_EXPERT, D_MODEL), lambda g, e, n: (e[g], 0, 0)),
            ],
            out_specs=pl.BlockSpec((rt, D_MODEL), rows),
            scratch_shapes=[
                pltpu.VMEM((D_MODEL, D_EXPERT), BF16),
                pltpu.VMEM((D_MODEL, D_EXPERT), BF16),
                pltpu.VMEM((D_EXPERT, D_MODEL), BF16),
                pltpu.SMEM((1,), I32),
            ],
        ),
        out_shape=jax.ShapeDtypeStruct((m, D_MODEL), F32),
        input_output_aliases={2: 0},
        compiler_params=_cparams(("arbitrary",)),
        name="experts",
    )(tile_expert, n_live, xs, w_gate, w_up, w_down)


def _combine_body(alpha, cnt_ref, off_ref, dst_ref, x1_ref, rw_ref, lpc_ref, g2_ref, b2_ref, ys_hbm, o_ref,
                  loc_ref, sems):
    i = pl.program_id(0)
    slot = lax.rem(i, 2)
    lists = (cnt_ref, off_ref, dst_ref)

    @pl.when(i == 0)
    def _():
        loc_ref[...] = jnp.zeros_like(loc_ref)
        _run_copies(lists, i, loc_ref.at[slot], ys_hbm, sems.at[slot], False, False)

    @pl.when(i + 1 < pl.num_programs(0))
    def _():
        _run_copies(lists, i + 1, loc_ref.at[1 - slot], ys_hbm, sems.at[1 - slot], False, False)

    _run_copies(lists, i, loc_ref.at[slot], ys_hbm, sems.at[slot], False, True)
    tm = x1_ref.shape[0]
    n_rows = loc_ref.shape[1]
    yb = loc_ref[slot].astype(BF16)
    lanes = lax.broadcasted_iota(I32, (tm, n_rows), 1)
    sel = (jnp.where(lanes == lpc_ref[:, 0:1].astype(I32), rw_ref[:, 0:1], 0.0)
           + jnp.where(lanes == lpc_ref[:, 1:2].astype(I32), rw_ref[:, 1:2], 0.0))
    moe = jnp.dot(sel.astype(BF16), yb, preferred_element_type=F32)
    y = alpha * x1_ref[...] + moe
    o_ref[...] = _layer_norm_rows(y, g2_ref[...], b2_ref[...])


def _combine(lists, x1, rw, lpc, ln_g, ln_b, ys, alpha):
    n = x1.shape[0]
    tm = min(ROUTE_TM, n)
    rowmap = lambda i, *_: (i, 0)
    full = lambda i, *_: (0, 0)
    return pl.pallas_call(
        functools.partial(_combine_body, alpha),
        grid_spec=pltpu.PrefetchScalarGridSpec(
            num_scalar_prefetch=3,
            grid=(n // tm,),
            in_specs=[
                pl.BlockSpec((tm, D_MODEL), rowmap),
                pl.BlockSpec((tm, LANES), rowmap),
                pl.BlockSpec((tm, LANES), rowmap),
                pl.BlockSpec((1, D_MODEL), full),
                pl.BlockSpec((1, D_MODEL), full),
                pl.BlockSpec(memory_space=pl.ANY),
            ],
            out_specs=pl.BlockSpec((tm, D_MODEL), rowmap),
            scratch_shapes=[pltpu.VMEM((2, _local_rows(tm), D_MODEL), F32), pltpu.SemaphoreType.DMA((2,))],
        ),
        out_shape=jax.ShapeDtypeStruct((n, D_MODEL), F32),
        compiler_params=_cparams(("arbitrary",)),
        name="combine_ln",
    )(*lists, x1, rw, lpc, ln_g.reshape(1, -1), ln_b.reshape(1, -1), ys)


def _rotary_tables(seq):
    half = RET_DK // 2
    inv = ROPE_BASE ** (-jnp.arange(half, dtype=F32) / half)
    split = min(ROT_SPLIT, seq)
    a_hi = (jnp.arange(seq // split, dtype=F32) * split)[:, None] * inv[None, :]
    a_lo = jnp.arange(split, dtype=F32)[:, None] * inv[None, :]
    ch, sh, cl, sl = jnp.cos(a_hi)[:, None], jnp.sin(a_hi)[:, None], jnp.cos(a_lo)[None], jnp.sin(a_lo)[None]
    return (ch * cl - sh * sl).reshape(seq, half), (sh * cl + ch * sl).reshape(seq, half)


def _router_weights(w_group, b_group, w_exp_router, b_exp_router):
    spare = LANES - N_GROUPS - N_EXPERTS
    w = jnp.pad(jnp.concatenate([w_group, w_exp_router], axis=1), ((0, 0), (0, spare)))
    b = jnp.pad(jnp.concatenate([b_group, b_exp_router]), (0, spare)).reshape(1, LANES)
    w_hi = w.astype(BF16)
    w_lo = (w - w_hi.astype(F32)).astype(BF16)
    return jnp.concatenate([w_hi, w_lo], axis=1), b


def _layer(x, depth, w_in, b_merge, conv_w, conv_b, w_rg_r, b_rg_r, w_rg_i, b_rg_i, lru_lambda,
           w_ret_o, w_lru_o, w_out, ln1_g, ln1_b, w_group, b_group, w_exp_router, b_exp_router,
           w_e_gate, w_e_up, w_e_down, ln2_g, ln2_b):
    B, S, D = x.shape
    n = B * S
    alpha = (2.0 * depth) ** 0.25
    x2 = x.reshape(n, D)
    wb = w_in.astype(BF16)
    cos, sin = _rotary_tables(S)
    gm, lru, ret = _proj_lru(x2, wb, cos, sin, b_merge, conv_w, conv_b, w_rg_r, b_rg_r, w_rg_i, b_rg_i,
                             lru_lambda, S)


    w_rt, b_rt = _router_weights(w_group, b_group, w_exp_router, b_exp_router)
    x1, rw, lpc, lpr, tcnt = _merge(ret, lru, gm, x2, w_ret_o.astype(BF16), w_lru_o.astype(BF16),
                                    w_out.astype(BF16), ln1_g, ln1_b, w_rt, b_rt, alpha)

    tm = min(ROUTE_TM, n)
    n_t = n // tm
    m_max = n_t * _local_rows(tm) + N_EXPERTS * MOE_RT
    rt = min(MOE_RT, m_max)
    cnt = tcnt[:, 0, :N_EXPERTS].astype(I32)
    units = (cnt + (SEG_ALIGN - 1)) // SEG_ALIGN
    run = units * SEG_ALIGN
    sizes = jnp.sum(run, axis=0)
    region = (sizes + (rt - 1)) // rt * rt
    e_end = jnp.cumsum(region)
    e_start = e_end - region
    gdst = e_start[None, :] + jnp.cumsum(run, axis=0) - run
    run_off = jnp.cumsum(run, axis=1) - run
    lists = _run_lists(units, run_off, gdst)
    pad_start = (e_start + sizes).astype(I32)
    pad_units = ((region - sizes) // SEG_ALIGN).astype(I32)
    total = e_end[-1:].astype(I32)
    tile_start = jnp.arange(m_max // rt, dtype=I32) * rt
    n_live = total // rt
    tile_expert = jnp.sum((e_end[None, :] <= jnp.minimum(tile_start, total - rt)[:, None]).astype(I32), axis=1)

    xs = _dispatch(lists, pad_start, pad_units, total, x1, lpr, m_max)
    ys = _experts(tile_expert, n_live, xs, w_e_gate, w_e_up, w_e_down)
    out = _combine(lists, x1, rw, lpc, ln2_g, ln2_b, ys, alpha)
    return out.reshape(B, S, D)


def kernel(x, w_in, b_merge, conv_w, conv_b, w_rg_r, b_rg_r, w_rg_i, b_rg_i, lru_lambda, w_ret_o, w_lru_o, w_out, ln1_g, ln1_b, w_group, b_group, w_exp_router, b_exp_router, w_e_gate, w_e_up, w_e_down, ln2_g, ln2_b):
    depth = w_in.shape[0]
    for l in range(depth):
        x = _layer(x, depth, w_in[l], b_merge[l], conv_w[l], conv_b[l], w_rg_r[l], b_rg_r[l], w_rg_i[l],
                   b_rg_i[l], lru_lambda[l], w_ret_o[l], w_lru_o[l], w_out[l], ln1_g[l], ln1_b[l],
                   w_group[l], b_group[l], w_exp_router[l], b_exp_router[l], w_e_gate[l], w_e_up[l],
                   w_e_down[l], ln2_g[l], ln2_b[l])
    return x
```

```python
import functools

import jax
import jax.numpy as jnp
import numpy as np
from jax import lax
from jax.experimental import pallas as pl
from jax.experimental.pallas import tpu as pltpu

F32 = jnp.float32
BF16 = jnp.bfloat16
I32 = jnp.int32

D_MODEL = 1024
RET_HEADS = 4
RET_DK = 256
RET_DV = 512
RET_QK = RET_HEADS * RET_DK
RET_V = RET_HEADS * RET_DV
ROPE_BASE = 10000.0
LRU_WIDTH = 1536
LRU_BLOCKS = 8
LRU_BLOCK = LRU_WIDTH // LRU_BLOCKS
LRU_PAIR = 2 * LRU_BLOCK
LRU_PAIRS = LRU_BLOCKS // 2
CONV_WIDTH = 4
LRU_C = 8.0
N_GROUPS = 4
EXPERTS_PER_GROUP = 8
N_EXPERTS = N_GROUPS * EXPERTS_PER_GROUP
TOP_K = 2
D_EXPERT = 512
LN_EPS = 1e-5

LANES = 128
SUBLANES = 8
VMEM_LIMIT = 56 * 1024 * 1024

PROJ_TM = 256
PROJ_CHUNK = 512
ROT_SPLIT = 64
ROUTE_TM = 512
MOE_RT = 512
SEG_ALIGN = SUBLANES
RUN_BITS = (ROUTE_TM // SEG_ALIGN).bit_length()


def _cparams(sem):
    return pltpu.CompilerParams(dimension_semantics=sem, vmem_limit_bytes=VMEM_LIMIT)


PROJ_WIDTHS = (2 * RET_QK, RET_V, RET_V, LRU_WIDTH, LRU_WIDTH, 2 * D_MODEL)


def _seg_pitch(t):
    g = t // SUBLANES
    units = -(-g // SUBLANES)
    return SUBLANES * (units + 1 - units % 2)


def _proj_lru_body(per_seq, dec_ref, x_ref, w_ref, cos_ref, sin_ref, bm_ref, cw_ref, cb_ref, wr_ref, wi_ref,
                   br_ref, bi_ref, lam_ref, dm_ref, xi_ref, zeta_ref, gm_ref, lru_ref, ret_ref,
                   ubuf_ref, gbuf_ref, a_ref, b_ref, carry_ref, st_ref):
    T = x_ref.shape[0]
    G = T // SUBLANES
    P = _seg_pitch(T)
    ncb = LRU_WIDTH // LANES
    o_qk, o_v, o_sg, o_u, o_gl, o_gm = (int(o) for o in np.cumsum((0,) + PROJ_WIDTHS[:-1]))
    xb = x_ref[...].astype(BF16)

    def seg(start, lo, hi):
        return jnp.dot(xb, w_ref[:, start + lo:start + hi], preferred_element_type=F32)

    @pl.when(lax.rem(pl.program_id(0), per_seq) == 0)
    def _():
        ubuf_ref[0:SUBLANES, :] = jnp.zeros((SUBLANES, LRU_WIDTH), F32)
        carry_ref[...] = jnp.zeros_like(carry_ref)
        st_ref[...] = jnp.zeros_like(st_ref)

    for lo in range(0, LRU_WIDTH, PROJ_CHUNK):
        hi = lo + PROJ_CHUNK
        ubuf_ref[SUBLANES:SUBLANES + T, lo:hi] = seg(o_u, lo, hi)
        gbuf_ref[:, lo:hi] = jax.nn.gelu(seg(o_gl, lo, hi))

    cw = cw_ref[...]
    uc = cb_ref[...] + cw[CONV_WIDTH - 1:CONV_WIDTH, :] * ubuf_ref[SUBLANES:SUBLANES + T, :]
    for j in range(CONV_WIDTH - 1):
        back = CONV_WIDTH - 1 - j
        uc = uc + cw[j:j + 1, :] * ubuf_ref[SUBLANES - back:SUBLANES - back + T, :]
    ubuf_ref[0:SUBLANES, :] = ubuf_ref[T:T + SUBLANES, :]

    ucb = uc.astype(BF16)
    neg_c_sp = -LRU_C * jax.nn.softplus(-lam_ref[...])
    for p in range(LRU_PAIRS):
        lo = p * LRU_PAIR
        sl = ucb[:, lo:lo + LRU_PAIR]
        r = jax.nn.sigmoid(jnp.dot(sl, wr_ref[p], preferred_element_type=F32) + br_ref[:, lo:lo + LRU_PAIR])
        i = jax.nn.sigmoid(jnp.dot(sl, wi_ref[p], preferred_element_type=F32) + bi_ref[:, lo:lo + LRU_PAIR])
        log_a = r * neg_c_sp[:, lo:lo + LRU_PAIR]
        a = jnp.exp(log_a)
        inp = jnp.sqrt(-jnp.tanh(log_a) * (a * a + 1.0)) * (i * uc[:, lo:lo + LRU_PAIR])
        for cc in range(LRU_PAIR // LANES):
            cb = p * (LRU_PAIR // LANES) + cc
            for s in range(SUBLANES):
                a_ref[cb, s * P:s * P + G, :] = a[s * G:(s + 1) * G, cc * LANES:(cc + 1) * LANES]
                b_ref[cb, s * P:s * P + G, :] = inp[s * G:(s + 1) * G, cc * LANES:(cc + 1) * LANES]

    half = RET_DK // 2
    cos, sin = cos_ref[...], sin_ref[...]

    def rotary(start, h, scale):
        acc = seg(start, h * RET_DK, (h + 1) * RET_DK)
        t1, t2 = acc[:, :half], acc[:, half:]
        return (jnp.concatenate([t1 * cos - t2 * sin, t1 * sin + t2 * cos], axis=1) * scale).astype(BF16)

    for h in range(RET_HEADS):
        vc = slice(h * RET_DV, (h + 1) * RET_DV)
        q = rotary(o_qk, h, 1.0)
        k = rotary(o_qk + RET_QK, h, RET_DK ** -0.5)
        v = seg(o_v, h * RET_DV, (h + 1) * RET_DV).astype(BF16)
        g = seg(o_sg, h * RET_DV, (h + 1) * RET_DV)
        scores = lax.dot_general(q, k, (((1,), (1,)), ((), ())), preferred_element_type=F32) * dm_ref[h]
        inner = jnp.dot(scores.astype(BF16), v, preferred_element_type=F32)
        st = st_ref[h]
        cross = jnp.dot(q, st.astype(BF16), preferred_element_type=F32) * xi_ref[h]
        kz = (k.astype(F32) * zeta_ref[h]).astype(BF16)
        upd = lax.dot_general(kz, v, (((0,), (0,)), ((), ())), preferred_element_type=F32)
        st_ref[h] = st * dec_ref[h] + upd
        o = inner + cross
        mu = jnp.mean(o, axis=-1, keepdims=True)
        oc = o - mu
        var = jnp.mean(oc * oc, axis=-1, keepdims=True)
        ret_ref[:, vc] = (g * jax.nn.sigmoid(g) * (oc * lax.rsqrt(var + LN_EPS))).astype(ret_ref.dtype)

    for lo in range(0, 2 * D_MODEL, PROJ_CHUNK):
        hi = lo + PROJ_CHUNK
        gm_ref[:, lo:hi] = jax.nn.sigmoid(seg(o_gm, lo, hi) + bm_ref[:, lo:hi]).astype(gm_ref.dtype)

    def step(j, hp):
        hs, ps = hp
        nh, npr = [], []
        for cb in range(ncb):
            a = a_ref[cb, pl.ds(j, SUBLANES, stride=P), :]
            b = b_ref[cb, pl.ds(j, SUBLANES, stride=P), :]
            hn = a * hs[cb] + b
            pn = a * ps[cb]
            b_ref[cb, pl.ds(j, SUBLANES, stride=P), :] = hn
            a_ref[cb, pl.ds(j, SUBLANES, stride=P), :] = pn
            nh.append(hn)
            npr.append(pn)
        return tuple(nh), tuple(npr)

    zeros = tuple(jnp.zeros((SUBLANES, LANES), F32) for _ in range(ncb))
    ones = tuple(jnp.ones((SUBLANES, LANES), F32) for _ in range(ncb))
    h_end, p_end = lax.fori_loop(0, G, step, (zeros, ones), unroll=True)

    for cb in range(ncb):
        cin = carry_ref[:, cb * LANES:(cb + 1) * LANES]
        for s in range(SUBLANES):
            rows = slice(s * G, (s + 1) * G)
            hseg = b_ref[cb, s * P:s * P + G, :] + a_ref[cb, s * P:s * P + G, :] * cin
            gate = gbuf_ref[rows, cb * LANES:(cb + 1) * LANES]
            lru_ref[rows, cb * LANES:(cb + 1) * LANES] = (gate * hseg).astype(lru_ref.dtype)
            cin = h_end[cb][s:s + 1, :] + p_end[cb][s:s + 1, :] * cin
        carry_ref[:, cb * LANES:(cb + 1) * LANES] = cin


def _block_diag_pairs(w):
    w4 = w.reshape(LRU_PAIRS, 2, LRU_BLOCK, LRU_BLOCK)
    z = jnp.zeros_like(w4[:, 0])
    top = jnp.concatenate([w4[:, 0], z], axis=2)
    bottom = jnp.concatenate([z, w4[:, 1]], axis=2)
    return jnp.concatenate([top, bottom], axis=1).astype(BF16)


def _retention_tables(chunk):
    H = RET_HEADS
    log_g = jnp.log1p(-(2.0 ** (-5.0 - jnp.arange(H, dtype=F32))))
    pos = jnp.arange(chunk, dtype=F32)
    diff = pos[:, None] - pos[None, :]
    causal = diff >= 0
    d_mask = jnp.where(causal[None], jnp.exp(log_g[:, None, None] * jnp.where(causal, diff, 0.0)[None]), 0.0)
    xi = jnp.exp(log_g[:, None] * (pos + 1.0)[None])[:, :, None]
    zeta = jnp.exp(log_g[:, None] * (chunk - 1.0 - pos)[None])[:, :, None]
    return d_mask, xi, zeta, jnp.exp(log_g * chunk)


def _proj_lru(x2, w, cos, sin, b_merge, conv_w, conv_b, w_r, b_r, w_i, b_i, lam, seq):
    n, d = x2.shape
    tm = min(PROJ_TM, seq)
    per_seq = seq // tm
    assert sum(PROJ_WIDTHS) == w.shape[1]
    ncb = LRU_WIDTH // LANES
    d_mask, xi, zeta, chunk_decay = _retention_tables(tm)
    rows = lambda width: pl.BlockSpec((tm, width), lambda i: (i, 0))
    full2 = lambda shape: pl.BlockSpec(shape, lambda i: (0, 0))
    full3 = lambda shape: pl.BlockSpec(shape, lambda i: (0, 0, 0))
    pairs = full3((LRU_PAIRS, LRU_PAIR, LRU_PAIR))
    rot = pl.BlockSpec((tm, RET_DK // 2), lambda i: (i % per_seq, 0))
    vec = lambda a: a.reshape(1, -1)
    out_widths = (2 * D_MODEL, LRU_WIDTH, RET_V)
    return pl.pallas_call(
        functools.partial(_proj_lru_body, per_seq),
        grid=(n // tm,),
        in_specs=[pl.BlockSpec(memory_space=pltpu.SMEM),
                  rows(d),
                  pl.BlockSpec(w.shape, lambda i: (0, 0), pipeline_mode=pl.Buffered(1)),
                  rot, rot, full2((1, 2 * D_MODEL)),
                  full2((CONV_WIDTH, LRU_WIDTH)), full2((1, LRU_WIDTH)), pairs, pairs,
                  full2((1, LRU_WIDTH)), full2((1, LRU_WIDTH)), full2((1, LRU_WIDTH)),
                  full3((RET_HEADS, tm, tm)), full3((RET_HEADS, tm, 1)), full3((RET_HEADS, tm, 1))],
        out_specs=[rows(width) for width in out_widths],
        out_shape=[jax.ShapeDtypeStruct((n, width), BF16) for width in out_widths],
        scratch_shapes=[
            pltpu.VMEM((tm + SUBLANES, LRU_WIDTH), F32),
            pltpu.VMEM((tm, LRU_WIDTH), F32),
            pltpu.VMEM((ncb, SUBLANES * _seg_pitch(tm), LANES), F32),
            pltpu.VMEM((ncb, SUBLANES * _seg_pitch(tm), LANES), F32),
            pltpu.VMEM((1, LRU_WIDTH), F32),
            pltpu.VMEM((RET_HEADS, RET_DK, RET_DV), F32),
        ],
        compiler_params=_cparams(("arbitrary",)),
        name="proj_mixers",
    )(chunk_decay, x2, w, cos, sin, vec(b_merge), conv_w, vec(conv_b), _block_diag_pairs(w_r),
      _block_diag_pairs(w_i), vec(b_r), vec(b_i), vec(lam), d_mask, xi, zeta)


def _layer_norm_rows(y, g, b):
    mu = jnp.mean(y, axis=-1, keepdims=True)
    yc = y - mu
    var = jnp.mean(yc * yc, axis=-1, keepdims=True)
    return yc * lax.rsqrt(var + LN_EPS) * g + b


def _merge_body(alpha, ret_ref, lru_ref, gm_ref, x_ref, wro_ref, wlo_ref, wo_ref, g1_ref, b1_ref,
                wrt_ref, brt_ref, x1_ref, rw_ref, lpc_ref, lpr_ref, cnt_ref, y_ref):
    @pl.when(pl.program_id(0) == 0)
    def _():
        y_ref[...] = jnp.zeros_like(y_ref)

    x1 = _layer_norm_rows(y_ref[...], g1_ref[...], b1_ref[...])
    x1_ref[...] = x1

    tm = x1.shape[0]
    x_hi = x1.astype(BF16)
    x_lo = (x1 - x_hi.astype(F32)).astype(BF16)
    hh = jnp.dot(x_hi, wrt_ref[...], preferred_element_type=F32)
    lh = jnp.dot(x_lo, wrt_ref[:, :LANES], preferred_element_type=F32)
    lg = hh[:, :LANES] + hh[:, LANES:] + lh + brt_ref[...]
    lane = lax.broadcasted_iota(I32, (tm, LANES), 1)
    big = jnp.int32(LANES)
    neg = jnp.float32(-jnp.inf)
    gmask = lane < N_GROUPS
    gl = jnp.where(gmask, lg, neg)
    gmax = jnp.max(gl, axis=-1, keepdims=True)
    g_idx = jnp.min(jnp.where(gmask & (gl == gmax), lane, big), axis=-1, keepdims=True)
    g_w = 1.0 / jnp.sum(jnp.where(gmask, jnp.exp(gl - gmax), 0.0), axis=-1, keepdims=True)
    e_lo = N_GROUPS + EXPERTS_PER_GROUP * g_idx
    emask = (lane >= e_lo) & (lane < e_lo + EXPERTS_PER_GROUP)
    el = jnp.where(emask, lg, neg)
    v1 = jnp.max(el, axis=-1, keepdims=True)
    i1 = jnp.min(jnp.where(emask & (el == v1), lane, big), axis=-1, keepdims=True)
    emask2 = emask & (lane != i1)
    el2 = jnp.where(emask2, lg, neg)
    v2 = jnp.max(el2, axis=-1, keepdims=True)
    i2 = jnp.min(jnp.where(emask2 & (el2 == v2), lane, big), axis=-1, keepdims=True)
    ex = jnp.exp(v2 - v1)
    den = 1.0 + ex
    w1 = g_w / den
    w2 = g_w * ex / den
    e1 = i1 - N_GROUPS
    e2 = i2 - N_GROUPS
    rw_ref[...] = jnp.where(lane == 0, w1, jnp.where(lane == 1, w2, 0.0))

    oh = (lane == e1).astype(F32) + (lane == e2).astype(F32)
    rowi = lax.broadcasted_iota(I32, (tm, tm), 0)
    coli = lax.broadcasted_iota(I32, (tm, tm), 1)
    tri = jnp.where(coli < rowi, 1.0, 0.0).astype(BF16)
    before = jnp.dot(tri, oh.astype(BF16), preferred_element_type=F32)
    cnt = jnp.sum(oh, axis=0, keepdims=True)
    units = jnp.floor((cnt + (SEG_ALIGN - 1.0)) * (1.0 / SEG_ALIGN))
    er = lax.broadcasted_iota(I32, (LANES, LANES), 0)
    ec = lax.broadcasted_iota(I32, (LANES, LANES), 1)
    upper = jnp.where(er < ec, 1.0, 0.0).astype(BF16)
    offs = SEG_ALIGN * jnp.dot(jnp.broadcast_to(units, (SUBLANES, LANES)).astype(BF16), upper,
                               preferred_element_type=F32)[0:1, :]
    pos = before + offs
    lp1 = jnp.sum(jnp.where(lane == e1, pos, 0.0), axis=-1, keepdims=True)
    lp2 = jnp.sum(jnp.where(lane == e2, pos, 0.0), axis=-1, keepdims=True)
    lpc = jnp.where(lane == 0, lp1, jnp.where(lane == 1, lp2, 0.0))
    lpc_ref[...] = lpc
    lpr_ref[...] = lpc.T[0:SUBLANES, :]
    cnt_ref[...] = jnp.broadcast_to(cnt, cnt_ref.shape)

    pr = jnp.dot(ret_ref[...], wro_ref[...], preferred_element_type=F32)
    pu = jnp.dot(lru_ref[...], wlo_ref[...], preferred_element_type=F32)
    merged = gm_ref[:, :D_MODEL].astype(F32) * pr + gm_ref[:, D_MODEL:].astype(F32) * pu
    y_ref[...] = alpha * x_ref[...] + jnp.dot(merged.astype(BF16), wo_ref[...], preferred_element_type=F32)


def _merge(ret, lru, gm, x2, w_ret_o, w_lru_o, w_out, ln_g, ln_b, w_rt, b_rt, alpha):
    n = x2.shape[0]
    tm = min(ROUTE_TM, n)
    n_t = n // tm
    inmap = lambda i: (jnp.minimum(i, n_t - 1), 0)
    rowmap = lambda i: (jnp.maximum(i - 1, 0), 0)
    full = lambda i: (0, 0)
    return pl.pallas_call(
        functools.partial(_merge_body, alpha),
        grid=(n_t + 1,),
        in_specs=[
            pl.BlockSpec((tm, RET_V), inmap),
            pl.BlockSpec((tm, LRU_WIDTH), inmap),
            pl.BlockSpec((tm, 2 * D_MODEL), inmap),
            pl.BlockSpec((tm, D_MODEL), inmap),
            pl.BlockSpec((RET_V, D_MODEL), full),
            pl.BlockSpec((LRU_WIDTH, D_MODEL), full),
            pl.BlockSpec((D_MODEL, D_MODEL), full),
            pl.BlockSpec((1, D_MODEL), full),
            pl.BlockSpec((1, D_MODEL), full),
            pl.BlockSpec((D_MODEL, 2 * LANES), full),
            pl.BlockSpec((1, LANES), full),
        ],
        out_specs=[
            pl.BlockSpec((tm, D_MODEL), rowmap),
            pl.BlockSpec((tm, LANES), rowmap),
            pl.BlockSpec((tm, LANES), rowmap),
            pl.BlockSpec((SUBLANES, tm), lambda i: (0, jnp.maximum(i - 1, 0))),
            pl.BlockSpec((None, SUBLANES, LANES), lambda i: (jnp.maximum(i - 1, 0), 0, 0)),
        ],
        out_shape=[
            jax.ShapeDtypeStruct((n, D_MODEL), F32),
            jax.ShapeDtypeStruct((n, LANES), F32),
            jax.ShapeDtypeStruct((n, LANES), F32),
            jax.ShapeDtypeStruct((SUBLANES, n), F32),
            jax.ShapeDtypeStruct((n_t, SUBLANES, LANES), F32),
        ],
        scratch_shapes=[pltpu.VMEM((tm, D_MODEL), F32)],
        compiler_params=_cparams(("arbitrary",)),
        name="merge_ln_route",
    )(ret, lru, gm, x2, w_ret_o, w_lru_o, w_out, ln_g.reshape(1, -1), ln_b.reshape(1, -1), w_rt, b_rt)


def _run_lists(units, run_off, gdst):
    k = jnp.arange(N_EXPERTS, dtype=I32)
    cnts, offs, dsts = [], [], []
    for b in range(RUN_BITS):
        bit = (units >> b) & 1
        low = (units & ((1 << b) - 1)) * SEG_ALIGN
        pos = jnp.cumsum(bit, axis=1) - bit
        hit = (bit[:, None, :] == 1) & (pos[:, None, :] == k[None, :, None])
        offs.append(jnp.sum(jnp.where(hit, (run_off + low)[:, None, :], 0), axis=2))
        dsts.append(jnp.sum(jnp.where(hit, (gdst + low)[:, None, :], 0), axis=2))
        cnts.append(jnp.sum(bit, axis=1))
    flat = lambda parts: jnp.stack(parts, axis=1).reshape(-1).astype(I32)
    return flat(cnts), flat(offs), flat(dsts)


def _run_copies(lists, tile, loc_ref, glob_hbm, sem, to_global, wait):
    cnt_ref, off_ref, dst_ref = lists
    for b in range(RUN_BITS):
        rows = SEG_ALIGN << b
        base = tile * RUN_BITS + b

        def piece(k, carry):
            off = pl.multiple_of(off_ref[base * N_EXPERTS + k], SEG_ALIGN)
            dst = pl.multiple_of(dst_ref[base * N_EXPERTS + k], SEG_ALIGN)
            l = loc_ref.at[pl.ds(off, rows), :]
            g = glob_hbm.at[pl.ds(dst, rows), :]
            cp = pltpu.make_async_copy(l, g, sem) if to_global else pltpu.make_async_copy(g, l, sem)
            if wait:
                cp.wait()
            else:
                cp.start()
            return carry

        lax.fori_loop(0, cnt_ref[base], piece, 0)


def _zero_rows(start, units, max_units, zero_ref, xs_hbm, sem, wait):
    pos = start
    for b in range((max_units - 1).bit_length()):
        rows = SEG_ALIGN << b
        bit = lax.bitwise_and(lax.shift_right_logical(units, b), 1)

        @pl.when(bit == 1)
        def _():
            dst = xs_hbm.at[pl.ds(pl.multiple_of(pos, SEG_ALIGN), rows), :]
            cp = pltpu.make_async_copy(zero_ref.at[pl.ds(0, rows), :], dst, sem)
            if wait:
                cp.wait()
            else:
                cp.start()

        pos = pos + bit * rows


def _onehot_rows(lpr_ref, n_rows):
    tm = lpr_ref.shape[1]
    sub = lax.broadcasted_iota(I32, (n_rows, tm), 0)
    lp1 = lpr_ref[0:1, :].astype(I32)
    lp2 = lpr_ref[1:2, :].astype(I32)
    return jnp.where((sub == lp1) | (sub == lp2), 1.0, 0.0).astype(BF16)


def _dispatch_body(cnt_ref, off_ref, dst_ref, pad_start_ref, pad_units_ref, total_ref, x1_ref, lpr_ref,
                   xs_hbm, loc_ref, zero_ref, sems, zsem):
    i = pl.program_id(0)
    last = pl.num_programs(0) - 1
    slot = lax.rem(i, 2)
    lists = (cnt_ref, off_ref, dst_ref)
    perm = _onehot_rows(lpr_ref, loc_ref.shape[1])
    loc_ref[slot] = jnp.dot(perm, x1_ref[...].astype(BF16), preferred_element_type=F32)
    _run_copies(lists, i, loc_ref.at[slot], xs_hbm, sems.at[slot], True, False)

    @pl.when(i > 0)
    def _():
        _run_copies(lists, i - 1, loc_ref.at[1 - slot], xs_hbm, sems.at[1 - slot], True, True)

    @pl.when(i == last)
    def _():
        zero_ref[...] = jnp.zeros_like(zero_ref)
        max_units = zero_ref.shape[0] // SEG_ALIGN
        for wait in (False, True):
            def region(e, carry):
                _zero_rows(pad_start_ref[e], pad_units_ref[e], max_units, zero_ref, xs_hbm, zsem, wait)
                return carry
            lax.fori_loop(0, N_EXPERTS, region, 0)
            _zero_tail(total_ref[0], zero_ref, xs_hbm, zsem, wait)
        _run_copies(lists, i, loc_ref.at[slot], xs_hbm, sems.at[slot], True, True)


def _zero_tail(total, zero_ref, xs_hbm, sem, wait):
    zr = zero_ref.shape[0]
    shift = zr.bit_length() - 1
    assert zr == 1 << shift and xs_hbm.shape[0] % SEG_ALIGN == 0
    dead = xs_hbm.shape[0] - total
    n_full = lax.shift_right_logical(dead, shift)

    def full(k, carry):
        dst = xs_hbm.at[pl.ds(pl.multiple_of(total + k * zr, SEG_ALIGN), zr), :]
        cp = pltpu.make_async_copy(zero_ref, dst, sem)
        if wait:
            cp.wait()
        else:
            cp.start()
        return carry

    lax.fori_loop(0, n_full, full, 0)
    rem = lax.shift_right_logical(dead - n_full * zr, SEG_ALIGN.bit_length() - 1)
    _zero_rows(total + n_full * zr, rem, zr // SEG_ALIGN, zero_ref, xs_hbm, sem, wait)


def _local_rows(tm):
    return TOP_K * tm + N_EXPERTS * SEG_ALIGN


def _dispatch(lists, pad_start, pad_units, total, x1, lpr, m_max):
    n = x1.shape[0]
    tm = min(ROUTE_TM, n)
    return pl.pallas_call(
        _dispatch_body,
        grid_spec=pltpu.PrefetchScalarGridSpec(
            num_scalar_prefetch=6,
            grid=(n // tm,),
            in_specs=[pl.BlockSpec((tm, D_MODEL), lambda i, *_: (i, 0)),
                      pl.BlockSpec((SUBLANES, tm), lambda i, *_: (0, i))],
            out_specs=pl.BlockSpec(memory_space=pl.ANY),
            scratch_shapes=[pltpu.VMEM((2, _local_rows(tm), D_MODEL), F32),
                            pltpu.VMEM((MOE_RT, D_MODEL), F32),
                            pltpu.SemaphoreType.DMA((2,)), pltpu.SemaphoreType.DMA],
        ),
        out_shape=jax.ShapeDtypeStruct((m_max, D_MODEL), F32),
        compiler_params=_cparams(("arbitrary",)),
        name="dispatch",
    )(*lists, pad_start, pad_units, total, x1, lpr)


def _expert_body(exp_ref, live_ref, xs_ref, wg_ref, wu_ref, wd_ref, y_ref, wgb_ref, wub_ref, wdb_ref, cur_ref):
    g = pl.program_id(0)
    e = exp_ref[g]

    @pl.when(g == 0)
    def _():
        cur_ref[0] = -1

    @pl.when(g < live_ref[0])
    def _():
        @pl.when(cur_ref[0] != e)
        def _():
            wgb_ref[...] = wg_ref[...].astype(BF16)
            wub_ref[...] = wu_ref[...].astype(BF16)
            wdb_ref[...] = wd_ref[...].astype(BF16)
            cur_ref[0] = e

        xb = xs_ref[...].astype(BF16)
        hg = jnp.dot(xb, wgb_ref[...], preferred_element_type=F32)
        hu = jnp.dot(xb, wub_ref[...], preferred_element_type=F32)
        hm = (hg * jax.nn.sigmoid(hg) * hu).astype(BF16)
        y_ref[...] = jnp.dot(hm, wdb_ref[...], preferred_element_type=F32)


def _experts(tile_expert, n_live, xs, w_gate, w_up, w_down):
    m = xs.shape[0]
    rt = min(MOE_RT, m)
    rows = lambda g, e, n: (jnp.minimum(g, n[0] - 1), 0)
    return pl.pallas_call(
        _expert_body,
        grid_spec=pltpu.PrefetchScalarGridSpec(
            num_scalar_prefetch=2,
            grid=(m // rt,),
            in_specs=[
                pl.BlockSpec((rt, D_MODEL), rows),
                pl.BlockSpec((None, D_MODEL, D_EXPERT), lambda g, e, n: (e[g], 0, 0)),
                pl.BlockSpec((None, D_MODEL, D_EXPERT), lambda g, e, n: (e[g], 0, 0)),
                pl.BlockSpec((None, D_EXPERT, D_MODEL), lambda g, e, n: (e[g], 0, 0)),
            ],
            out_specs=pl.BlockSpec((rt, D_MODEL), rows),
            scratch_shapes=[
                pltpu.VMEM((D_MODEL, D_EXPERT), BF16),
                pltpu.VMEM((D_MODEL, D_EXPERT), BF16),
                pltpu.VMEM((D_EXPERT, D_MODEL), BF16),
                pltpu.SMEM((1,), I32),
            ],
        ),
        out_shape=jax.ShapeDtypeStruct((m, D_MODEL), F32),
        input_output_aliases={2: 0},
        compiler_params=_cparams(("arbitrary",)),
        name="experts",
    )(tile_expert, n_live, xs, w_gate, w_up, w_down)


def _combine_body(alpha, cnt_ref, off_ref, dst_ref, x1_ref, rw_ref, lpc_ref, g2_ref, b2_ref, ys_hbm, o_ref,
                  loc_ref, sems):
    i = pl.program_id(0)
    slot = lax.rem(i, 2)
    lists = (cnt_ref, off_ref, dst_ref)

    @pl.when(i == 0)
    def _():
        loc_ref[...] = jnp.zeros_like(loc_ref)
        _run_copies(lists, i, loc_ref.at[slot], ys_hbm, sems.at[slot], False, False)

    @pl.when(i + 1 < pl.num_programs(0))
    def _():
        _run_copies(lists, i + 1, loc_ref.at[1 - slot], ys_hbm, sems.at[1 - slot], False, False)

    _run_copies(lists, i, loc_ref.at[slot], ys_hbm, sems.at[slot], False, True)
    tm = x1_ref.shape[0]
    n_rows = loc_ref.shape[1]
    yb = loc_ref[slot].astype(BF16)
    lanes = lax.broadcasted_iota(I32, (tm, n_rows), 1)
    sel = (jnp.where(lanes == lpc_ref[:, 0:1].astype(I32), rw_ref[:, 0:1], 0.0)
           + jnp.where(lanes == lpc_ref[:, 1:2].astype(I32), rw_ref[:, 1:2], 0.0))
    moe = jnp.dot(sel.astype(BF16), yb, preferred_element_type=F32)
    y = alpha * x1_ref[...] + moe
    o_ref[...] = _layer_norm_rows(y, g2_ref[...], b2_ref[...])


def _combine(lists, x1, rw, lpc, ln_g, ln_b, ys, alpha):
    n = x1.shape[0]
    tm = min(ROUTE_TM, n)
    rowmap = lambda i, *_: (i, 0)
    full = lambda i, *_: (0, 0)
    return pl.pallas_call(
        functools.partial(_combine_body, alpha),
        grid_spec=pltpu.PrefetchScalarGridSpec(
            num_scalar_prefetch=3,
            grid=(n // tm,),
            in_specs=[
                pl.BlockSpec((tm, D_MODEL), rowmap),
                pl.BlockSpec((tm, LANES), rowmap),
                pl.BlockSpec((tm, LANES), rowmap),
                pl.BlockSpec((1, D_MODEL), full),
                pl.BlockSpec((1, D_MODEL), full),
                pl.BlockSpec(memory_space=pl.ANY),
            ],
            out_specs=pl.BlockSpec((tm, D_MODEL), rowmap),
            scratch_shapes=[pltpu.VMEM((2, _local_rows(tm), D_MODEL), F32), pltpu.SemaphoreType.DMA((2,))],
        ),
        out_shape=jax.ShapeDtypeStruct((n, D_MODEL), F32),
        compiler_params=_cparams(("arbitrary",)),
        name="combine_ln",
    )(*lists, x1, rw, lpc, ln_g.reshape(1, -1), ln_b.reshape(1, -1), ys)


def _rotary_tables(seq):
    half = RET_DK // 2
    inv = ROPE_BASE ** (-jnp.arange(half, dtype=F32) / half)
    split = min(ROT_SPLIT, seq)
    a_hi = (jnp.arange(seq // split, dtype=F32) * split)[:, None] * inv[None, :]
    a_lo = jnp.arange(split, dtype=F32)[:, None] * inv[None, :]
    ch, sh, cl, sl = jnp.cos(a_hi)[:, None], jnp.sin(a_hi)[:, None], jnp.cos(a_lo)[None], jnp.sin(a_lo)[None]
    return (ch * cl - sh * sl).reshape(seq, half), (sh * cl + ch * sl).reshape(seq, half)


def _router_weights(w_group, b_group, w_exp_router, b_exp_router):
    spare = LANES - N_GROUPS - N_EXPERTS
    w = jnp.pad(jnp.concatenate([w_group, w_exp_router], axis=1), ((0, 0), (0, spare)))
    b = jnp.pad(jnp.concatenate([b_group, b_exp_router]), (0, spare)).reshape(1, LANES)
    w_hi = w.astype(BF16)
    w_lo = (w - w_hi.astype(F32)).astype(BF16)
    return jnp.concatenate([w_hi, w_lo], axis=1), b


def _layer(x, depth, w_in, b_merge, conv_w, conv_b, w_rg_r, b_rg_r, w_rg_i, b_rg_i, lru_lambda,
           w_ret_o, w_lru_o, w_out, ln1_g, ln1_b, w_group, b_group, w_exp_router, b_exp_router,
           w_e_gate, w_e_up, w_e_down, ln2_g, ln2_b):
    B, S, D = x.shape
    n = B * S
    alpha = (2.0 * depth) ** 0.25
    x2 = x.reshape(n, D)
    wb = w_in.astype(BF16)
    cos, sin = _rotary_tables(S)
    gm, lru, ret = _proj_lru(x2, wb, cos, sin, b_merge, conv_w, conv_b, w_rg_r, b_rg_r, w_rg_i, b_rg_i,
                             lru_lambda, S)


    w_rt, b_rt = _router_weights(w_group, b_group, w_exp_router, b_exp_router)
    x1, rw, lpc, lpr, tcnt = _merge(ret, lru, gm, x2, w_ret_o.astype(BF16), w_lru_o.astype(BF16),
                                    w_out.astype(BF16), ln1_g, ln1_b, w_rt, b_rt, alpha)

    tm = min(ROUTE_TM, n)
    n_t = n // tm
    m_max = n_t * _local_rows(tm) + N_EXPERTS * MOE_RT
    rt = min(MOE_RT, m_max)
    cnt = tcnt[:, 0, :N_EXPERTS].astype(I32)
    units = (cnt + (SEG_ALIGN - 1)) // SEG_ALIGN
    run = units * SEG_ALIGN
    sizes = jnp.sum(run, axis=0)
    region = (sizes + (rt - 1)) // rt * rt
    e_end = jnp.cumsum(region)
    e_start = e_end - region
    gdst = e_start[None, :] + jnp.cumsum(run, axis=0) - run
    run_off = jnp.cumsum(run, axis=1) - run
    lists = _run_lists(units, run_off, gdst)
    pad_start = (e_start + sizes).astype(I32)
    pad_units = ((region - sizes) // SEG_ALIGN).astype(I32)
    total = e_end[-1:].astype(I32)
    tile_start = jnp.arange(m_max // rt, dtype=I32) * rt
    n_live = total // rt
    tile_expert = jnp.sum((e_end[None, :] <= jnp.minimum(tile_start, total - rt)[:, None]).astype(I32), axis=1)

    xs = _dispatch(lists, pad_start, pad_units, total, x1, lpr, m_max)
    ys = _experts(tile_expert, n_live, xs, w_e_gate, w_e_up, w_e_down)
    out = _combine(lists, x1, rw, lpc, ln2_g, ln2_b, ys, alpha)
    return out.reshape(B, S, D)


def kernel(x, w_in, b_merge, conv_w, conv_b, w_rg_r, b_rg_r, w_rg_i, b_rg_i, lru_lambda, w_ret_o, w_lru_o, w_out, ln1_g, ln1_b, w_group, b_group, w_exp_router, b_exp_router, w_e_gate, w_e_up, w_e_down, ln2_g, ln2_b):
    depth = w_in.shape[0]
    for l in range(depth):
        x = _layer(x, depth, w_in[l], b_merge[l], conv_w[l], conv_b[l], w_rg_r[l], b_rg_r[l], w_rg_i[l],
                   b_rg_i[l], lru_lambda[l], w_ret_o[l], w_lru_o[l], w_out[l], ln1_g[l], ln1_b[l],
                   w_group[l], b_group[l], w_exp_router[l], b_exp_router[l], w_e_gate[l], w_e_up[l],
                   w_e_down[l], ln2_g[l], ln2_b[l])
    return x
```

```python
import functools

import jax
import jax.numpy as jnp
import numpy as np
from jax import lax
from jax.experimental import pallas as pl
from jax.experimental.pallas import tpu as pltpu

F32 = jnp.float32
BF16 = jnp.bfloat16
I32 = jnp.int32

D_MODEL = 1024
RET_HEADS = 4
RET_DK = 256
RET_DV = 512
RET_QK = RET_HEADS * RET_DK
RET_V = RET_HEADS * RET_DV
ROPE_BASE = 10000.0
LRU_WIDTH = 1536
LRU_BLOCKS = 8
LRU_BLOCK = LRU_WIDTH // LRU_BLOCKS
LRU_PAIR = 2 * LRU_BLOCK
LRU_PAIRS = LRU_BLOCKS // 2
GATE_WIN = 2 * 128
CONV_WIDTH = 4
LRU_C = 8.0
N_GROUPS = 4
EXPERTS_PER_GROUP = 8
N_EXPERTS = N_GROUPS * EXPERTS_PER_GROUP
TOP_K = 2
D_EXPERT = 512
LN_EPS = 1e-5

LANES = 128
SUBLANES = 8
VMEM_LIMIT = 56 * 1024 * 1024

PROJ_TM = 256
PROJ_CHUNK = 512
ROT_SPLIT = 64
ROUTE_TM = 512
MOE_RT = 512
SEG_ALIGN = SUBLANES
RUN_BITS = (ROUTE_TM // SEG_ALIGN).bit_length()


def _cparams(sem):
    return pltpu.CompilerParams(dimension_semantics=sem, vmem_limit_bytes=VMEM_LIMIT)


PROJ_WIDTHS = (2 * RET_QK, RET_V, RET_V, LRU_WIDTH, LRU_WIDTH, 2 * D_MODEL)


def _sigmoid(x):
    return 0.5 * jnp.tanh(0.5 * x) + 0.5


def _seg_pitch(t):
    g = t // SUBLANES
    units = -(-g // SUBLANES)
    return SUBLANES * (units + 1 - units % 2)


def _proj_lru_body(per_seq, dec_ref, x_ref, w_ref, cos_ref, sin_ref, bm_ref, cw_ref, cb_ref, wr_ref, wi_ref,
                   br_ref, bi_ref, lam_ref, dm_ref, xi_ref, zeta_ref, gm_ref, lru_ref, ret_ref,
                   ubuf_ref, gbuf_ref, a_ref, b_ref, carry_ref, st_ref):
    T = x_ref.shape[0]
    G = T // SUBLANES
    P = _seg_pitch(T)
    ncb = LRU_WIDTH // LANES
    o_qk, o_v, o_sg, o_u, o_gl, o_gm = (int(o) for o in np.cumsum((0,) + PROJ_WIDTHS[:-1]))
    xb = x_ref[...].astype(BF16)

    def seg(start, lo, hi):
        return jnp.dot(xb, w_ref[:, start + lo:start + hi], preferred_element_type=F32)

    @pl.when(lax.rem(pl.program_id(0), per_seq) == 0)
    def _():
        ubuf_ref[0:SUBLANES, :] = jnp.zeros((SUBLANES, LRU_WIDTH), F32)
        carry_ref[...] = jnp.zeros_like(carry_ref)
        st_ref[...] = jnp.zeros_like(st_ref)

    for lo in range(0, LRU_WIDTH, PROJ_CHUNK):
        hi = lo + PROJ_CHUNK
        ubuf_ref[SUBLANES:SUBLANES + T, lo:hi] = seg(o_u, lo, hi)
        gbuf_ref[:, lo:hi] = jax.nn.gelu(seg(o_gl, lo, hi))

    cw = cw_ref[...]
    uc = cb_ref[...] + cw[CONV_WIDTH - 1:CONV_WIDTH, :] * ubuf_ref[SUBLANES:SUBLANES + T, :]
    for j in range(CONV_WIDTH - 1):
        back = CONV_WIDTH - 1 - j
        uc = uc + cw[j:j + 1, :] * ubuf_ref[SUBLANES - back:SUBLANES - back + T, :]
    ubuf_ref[0:SUBLANES, :] = ubuf_ref[T:T + SUBLANES, :]

    ucb = uc.astype(BF16)
    neg_c_sp = -LRU_C * jax.nn.softplus(-lam_ref[...])
    for p in range(LRU_PAIRS):
        lo = p * LRU_PAIR
        even = ucb[:, lo:lo + GATE_WIN]
        odd = ucb[:, lo + LRU_PAIR - GATE_WIN:lo + LRU_PAIR]

        def gate_map(w_ref):
            e = jnp.dot(even, w_ref[2 * p], preferred_element_type=F32)
            o = jnp.dot(odd, w_ref[2 * p + 1], preferred_element_type=F32)
            return jnp.concatenate([e[:, :LANES], e[:, LANES:] + o[:, :LANES], o[:, LANES:]], axis=1)

        r = _sigmoid(gate_map(wr_ref) + br_ref[:, lo:lo + LRU_PAIR])
        i = _sigmoid(gate_map(wi_ref) + bi_ref[:, lo:lo + LRU_PAIR])
        log_a = r * neg_c_sp[:, lo:lo + LRU_PAIR]
        a = jnp.exp(log_a)
        inp = jnp.sqrt(-jnp.tanh(log_a) * (a * a + 1.0)) * (i * uc[:, lo:lo + LRU_PAIR])
        for cc in range(LRU_PAIR // LANES):
            cb = p * (LRU_PAIR // LANES) + cc
            for s in range(SUBLANES):
                a_ref[cb, s * P:s * P + G, :] = a[s * G:(s + 1) * G, cc * LANES:(cc + 1) * LANES]
                b_ref[cb, s * P:s * P + G, :] = inp[s * G:(s + 1) * G, cc * LANES:(cc + 1) * LANES]

    half = RET_DK // 2
    cos, sin = cos_ref[...], sin_ref[...]

    def rotary(start, h, scale):
        acc = seg(start, h * RET_DK, (h + 1) * RET_DK)
        t1, t2 = acc[:, :half], acc[:, half:]
        return (jnp.concatenate([t1 * cos - t2 * sin, t1 * sin + t2 * cos], axis=1) * scale).astype(BF16)

    for h in range(RET_HEADS):
        vc = slice(h * RET_DV, (h + 1) * RET_DV)
        q = rotary(o_qk, h, 1.0)
        k = rotary(o_qk + RET_QK, h, RET_DK ** -0.5)
        v = seg(o_v, h * RET_DV, (h + 1) * RET_DV).astype(BF16)
        g = seg(o_sg, h * RET_DV, (h + 1) * RET_DV)
        scores = lax.dot_general(q, k, (((1,), (1,)), ((), ())), preferred_element_type=F32) * dm_ref[h]
        inner = jnp.dot(scores.astype(BF16), v, preferred_element_type=F32)
        st = st_ref[h]
        cross = jnp.dot(q, st.astype(BF16), preferred_element_type=F32) * xi_ref[h]
        kz = (k.astype(F32) * zeta_ref[h]).astype(BF16)
        upd = lax.dot_general(kz, v, (((0,), (0,)), ((), ())), preferred_element_type=F32)
        st_ref[h] = st * dec_ref[h] + upd
        o = inner + cross
        mu = jnp.mean(o, axis=-1, keepdims=True)
        oc = o - mu
        var = jnp.mean(oc * oc, axis=-1, keepdims=True)
        ret_ref[:, vc] = (g * _sigmoid(g) * (oc * lax.rsqrt(var + LN_EPS))).astype(ret_ref.dtype)

    for lo in range(0, 2 * D_MODEL, PROJ_CHUNK):
        hi = lo + PROJ_CHUNK
        gm_ref[:, lo:hi] = _sigmoid(seg(o_gm, lo, hi) + bm_ref[:, lo:hi]).astype(gm_ref.dtype)

    def step(j, hp):
        hs, ps = hp
        nh, npr = [], []
        for cb in range(ncb):
            a = a_ref[cb, pl.ds(j, SUBLANES, stride=P), :]
            b = b_ref[cb, pl.ds(j, SUBLANES, stride=P), :]
            hn = a * hs[cb] + b
            pn = a * ps[cb]
            b_ref[cb, pl.ds(j, SUBLANES, stride=P), :] = hn
            a_ref[cb, pl.ds(j, SUBLANES, stride=P), :] = pn
            nh.append(hn)
            npr.append(pn)
        return tuple(nh), tuple(npr)

    zeros = tuple(jnp.zeros((SUBLANES, LANES), F32) for _ in range(ncb))
    ones = tuple(jnp.ones((SUBLANES, LANES), F32) for _ in range(ncb))
    h_end, p_end = lax.fori_loop(0, G, step, (zeros, ones), unroll=True)

    for cb in range(ncb):
        cin = carry_ref[:, cb * LANES:(cb + 1) * LANES]
        for s in range(SUBLANES):
            rows = slice(s * G, (s + 1) * G)
            hseg = b_ref[cb, s * P:s * P + G, :] + a_ref[cb, s * P:s * P + G, :] * cin
            gate = gbuf_ref[rows, cb * LANES:(cb + 1) * LANES]
            lru_ref[rows, cb * LANES:(cb + 1) * LANES] = (gate * hseg).astype(lru_ref.dtype)
            cin = h_end[cb][s:s + 1, :] + p_end[cb][s:s + 1, :] * cin
        carry_ref[:, cb * LANES:(cb + 1) * LANES] = cin


def _gate_windows(w):
    spare = GATE_WIN - LRU_BLOCK
    w4 = w.reshape(LRU_PAIRS, 2, LRU_BLOCK, LRU_BLOCK)
    first = jnp.pad(w4[:, 0], ((0, 0), (0, spare), (0, spare)))
    second = jnp.pad(w4[:, 1], ((0, 0), (spare, 0), (spare, 0)))
    return jnp.stack([first, second], axis=1).reshape(LRU_BLOCKS, GATE_WIN, GATE_WIN).astype(BF16)


def _retention_tables(chunk):
    H = RET_HEADS
    log_g = jnp.log1p(-(2.0 ** (-5.0 - jnp.arange(H, dtype=F32))))
    pos = jnp.arange(chunk, dtype=F32)
    diff = pos[:, None] - pos[None, :]
    causal = diff >= 0
    d_mask = jnp.where(causal[None], jnp.exp(log_g[:, None, None] * jnp.where(causal, diff, 0.0)[None]), 0.0)
    xi = jnp.exp(log_g[:, None] * (pos + 1.0)[None])[:, :, None]
    zeta = jnp.exp(log_g[:, None] * (chunk - 1.0 - pos)[None])[:, :, None]
    return d_mask, xi, zeta, jnp.exp(log_g * chunk)


def _proj_lru(x2, w, cos, sin, b_merge, conv_w, conv_b, w_r, b_r, w_i, b_i, lam, seq):
    n, d = x2.shape
    tm = min(PROJ_TM, seq)
    per_seq = seq // tm
    assert sum(PROJ_WIDTHS) == w.shape[1]
    ncb = LRU_WIDTH // LANES
    d_mask, xi, zeta, chunk_decay = _retention_tables(tm)
    rows = lambda width: pl.BlockSpec((tm, width), lambda i: (i, 0))
    full2 = lambda shape: pl.BlockSpec(shape, lambda i: (0, 0))
    full3 = lambda shape: pl.BlockSpec(shape, lambda i: (0, 0, 0))
    pairs = full3((LRU_BLOCKS, GATE_WIN, GATE_WIN))
    rot = pl.BlockSpec((tm, RET_DK // 2), lambda i: (i % per_seq, 0))
    vec = lambda a: a.reshape(1, -1)
    out_widths = (2 * D_MODEL, LRU_WIDTH, RET_V)
    return pl.pallas_call(
        functools.partial(_proj_lru_body, per_seq),
        grid=(n // tm,),
        in_specs=[pl.BlockSpec(memory_space=pltpu.SMEM),
                  rows(d),
                  pl.BlockSpec(w.shape, lambda i: (0, 0), pipeline_mode=pl.Buffered(1)),
                  rot, rot, full2((1, 2 * D_MODEL)),
                  full2((CONV_WIDTH, LRU_WIDTH)), full2((1, LRU_WIDTH)), pairs, pairs,
                  full2((1, LRU_WIDTH)), full2((1, LRU_WIDTH)), full2((1, LRU_WIDTH)),
                  full3((RET_HEADS, tm, tm)), full3((RET_HEADS, tm, 1)), full3((RET_HEADS, tm, 1))],
        out_specs=[rows(width) for width in out_widths],
        out_shape=[jax.ShapeDtypeStruct((n, width), BF16) for width in out_widths],
        scratch_shapes=[
            pltpu.VMEM((tm + SUBLANES, LRU_WIDTH), F32),
            pltpu.VMEM((tm, LRU_WIDTH), F32),
            pltpu.VMEM((ncb, SUBLANES * _seg_pitch(tm), LANES), F32),
            pltpu.VMEM((ncb, SUBLANES * _seg_pitch(tm), LANES), F32),
            pltpu.VMEM((1, LRU_WIDTH), F32),
            pltpu.VMEM((RET_HEADS, RET_DK, RET_DV), F32),
        ],
        compiler_params=_cparams(("arbitrary",)),
        name="proj_mixers",
    )(chunk_decay, x2, w, cos, sin, vec(b_merge), conv_w, vec(conv_b), _gate_windows(w_r),
      _gate_windows(w_i), vec(b_r), vec(b_i), vec(lam), d_mask, xi, zeta)


def _layer_norm_rows(y, g, b):
    mu = jnp.mean(y, axis=-1, keepdims=True)
    yc = y - mu
    var = jnp.mean(yc * yc, axis=-1, keepdims=True)
    return yc * lax.rsqrt(var + LN_EPS) * g + b


def _merge_body(alpha, ret_ref, lru_ref, gm_ref, x_ref, wro_ref, wlo_ref, wo_ref, g1_ref, b1_ref,
                wrt_ref, brt_ref, x1_ref, rw_ref, lpc_ref, lpr_ref, cnt_ref, y_ref):
    @pl.when(pl.program_id(0) == 0)
    def _():
        y_ref[...] = jnp.zeros_like(y_ref)

    x1 = _layer_norm_rows(y_ref[...], g1_ref[...], b1_ref[...])
    x1_ref[...] = x1

    tm = x1.shape[0]
    x_hi = x1.astype(BF16)
    x_lo = (x1 - x_hi.astype(F32)).astype(BF16)
    hh = jnp.dot(x_hi, wrt_ref[...], preferred_element_type=F32)
    lh = jnp.dot(x_lo, wrt_ref[:, :LANES], preferred_element_type=F32)
    lg = hh[:, :LANES] + hh[:, LANES:] + lh + brt_ref[...]
    lane = lax.broadcasted_iota(I32, (tm, LANES), 1)
    big = jnp.int32(LANES)
    neg = jnp.float32(-jnp.inf)
    gmask = lane < N_GROUPS
    gl = jnp.where(gmask, lg, neg)
    gmax = jnp.max(gl, axis=-1, keepdims=True)
    g_idx = jnp.min(jnp.where(gmask & (gl == gmax), lane, big), axis=-1, keepdims=True)
    g_w = 1.0 / jnp.sum(jnp.where(gmask, jnp.exp(gl - gmax), 0.0), axis=-1, keepdims=True)
    e_lo = N_GROUPS + EXPERTS_PER_GROUP * g_idx
    emask = (lane >= e_lo) & (lane < e_lo + EXPERTS_PER_GROUP)
    el = jnp.where(emask, lg, neg)
    v1 = jnp.max(el, axis=-1, keepdims=True)
    i1 = jnp.min(jnp.where(emask & (el == v1), lane, big), axis=-1, keepdims=True)
    emask2 = emask & (lane != i1)
    el2 = jnp.where(emask2, lg, neg)
    v2 = jnp.max(el2, axis=-1, keepdims=True)
    i2 = jnp.min(jnp.where(emask2 & (el2 == v2), lane, big), axis=-1, keepdims=True)
    ex = jnp.exp(v2 - v1)
    den = 1.0 + ex
    w1 = g_w / den
    w2 = g_w * ex / den
    e1 = i1 - N_GROUPS
    e2 = i2 - N_GROUPS
    rw_ref[...] = jnp.where(lane == 0, w1, jnp.where(lane == 1, w2, 0.0))

    oh = (lane == e1).astype(F32) + (lane == e2).astype(F32)
    rowi = lax.broadcasted_iota(I32, (tm, tm), 0)
    coli = lax.broadcasted_iota(I32, (tm, tm), 1)
    tri = jnp.where(coli < rowi, 1.0, 0.0).astype(BF16)
    before = jnp.dot(tri, oh.astype(BF16), preferred_element_type=F32)
    cnt = jnp.sum(oh, axis=0, keepdims=True)
    units = jnp.floor((cnt + (SEG_ALIGN - 1.0)) * (1.0 / SEG_ALIGN))
    er = lax.broadcasted_iota(I32, (LANES, LANES), 0)
    ec = lax.broadcasted_iota(I32, (LANES, LANES), 1)
    upper = jnp.where(er < ec, 1.0, 0.0).astype(BF16)
    offs = SEG_ALIGN * jnp.dot(jnp.broadcast_to(units, (SUBLANES, LANES)).astype(BF16), upper,
                               preferred_element_type=F32)[0:1, :]
    pos = before + offs
    lp1 = jnp.sum(jnp.where(lane == e1, pos, 0.0), axis=-1, keepdims=True)
    lp2 = jnp.sum(jnp.where(lane == e2, pos, 0.0), axis=-1, keepdims=True)
    lpc = jnp.where(lane == 0, lp1, jnp.where(lane == 1, lp2, 0.0))
    lpc_ref[...] = lpc
    lpr_ref[...] = lpc.T[0:SUBLANES, :]
    cnt_ref[...] = jnp.broadcast_to(cnt, cnt_ref.shape)

    pr = jnp.dot(ret_ref[...], wro_ref[...], preferred_element_type=F32)
    pu = jnp.dot(lru_ref[...], wlo_ref[...], preferred_element_type=F32)
    merged = gm_ref[:, :D_MODEL].astype(F32) * pr + gm_ref[:, D_MODEL:].astype(F32) * pu
    y_ref[...] = alpha * x_ref[...] + jnp.dot(merged.astype(BF16), wo_ref[...], preferred_element_type=F32)


def _merge(ret, lru, gm, x2, w_ret_o, w_lru_o, w_out, ln_g, ln_b, w_rt, b_rt, alpha):
    n = x2.shape[0]
    tm = min(ROUTE_TM, n)
    n_t = n // tm
    inmap = lambda i: (jnp.minimum(i, n_t - 1), 0)
    rowmap = lambda i: (jnp.maximum(i - 1, 0), 0)
    full = lambda i: (0, 0)
    return pl.pallas_call(
        functools.partial(_merge_body, alpha),
        grid=(n_t + 1,),
        in_specs=[
            pl.BlockSpec((tm, RET_V), inmap),
            pl.BlockSpec((tm, LRU_WIDTH), inmap),
            pl.BlockSpec((tm, 2 * D_MODEL), inmap),
            pl.BlockSpec((tm, D_MODEL), inmap),
            pl.BlockSpec((RET_V, D_MODEL), full),
            pl.BlockSpec((LRU_WIDTH, D_MODEL), full),
            pl.BlockSpec((D_MODEL, D_MODEL), full),
            pl.BlockSpec((1, D_MODEL), full),
            pl.BlockSpec((1, D_MODEL), full),
            pl.BlockSpec((D_MODEL, 2 * LANES), full),
            pl.BlockSpec((1, LANES), full),
        ],
        out_specs=[
            pl.BlockSpec((tm, D_MODEL), rowmap),
            pl.BlockSpec((tm, LANES), rowmap),
            pl.BlockSpec((tm, LANES), rowmap),
            pl.BlockSpec((SUBLANES, tm), lambda i: (0, jnp.maximum(i - 1, 0))),
            pl.BlockSpec((None, SUBLANES, LANES), lambda i: (jnp.maximum(i - 1, 0), 0, 0)),
        ],
        out_shape=[
            jax.ShapeDtypeStruct((n, D_MODEL), F32),
            jax.ShapeDtypeStruct((n, LANES), F32),
            jax.ShapeDtypeStruct((n, LANES), F32),
            jax.ShapeDtypeStruct((SUBLANES, n), F32),
            jax.ShapeDtypeStruct((n_t, SUBLANES, LANES), F32),
        ],
        scratch_shapes=[pltpu.VMEM((tm, D_MODEL), F32)],
        compiler_params=_cparams(("arbitrary",)),
        name="merge_ln_route",
    )(ret, lru, gm, x2, w_ret_o, w_lru_o, w_out, ln_g.reshape(1, -1), ln_b.reshape(1, -1), w_rt, b_rt)


def _run_lists(units, run_off, gdst):
    k = jnp.arange(N_EXPERTS, dtype=I32)
    cnts, offs, dsts = [], [], []
    for b in range(RUN_BITS):
        bit = (units >> b) & 1
        low = (units & ((1 << b) - 1)) * SEG_ALIGN
        pos = jnp.cumsum(bit, axis=1) - bit
        hit = (bit[:, None, :] == 1) & (pos[:, None, :] == k[None, :, None])
        offs.append(jnp.sum(jnp.where(hit, (run_off + low)[:, None, :], 0), axis=2))
        dsts.append(jnp.sum(jnp.where(hit, (gdst + low)[:, None, :], 0), axis=2))
        cnts.append(jnp.sum(bit, axis=1))
    flat = lambda parts: jnp.stack(parts, axis=1).reshape(-1).astype(I32)
    return flat(cnts), flat(offs), flat(dsts)


def _run_copies(lists, tile, loc_ref, glob_hbm, sem, to_global, wait):
    cnt_ref, off_ref, dst_ref = lists
    for b in range(RUN_BITS):
        rows = SEG_ALIGN << b
        base = tile * RUN_BITS + b

        def piece(k, carry):
            off = pl.multiple_of(off_ref[base * N_EXPERTS + k], SEG_ALIGN)
            dst = pl.multiple_of(dst_ref[base * N_EXPERTS + k], SEG_ALIGN)
            l = loc_ref.at[pl.ds(off, rows), :]
            g = glob_hbm.at[pl.ds(dst, rows), :]
            cp = pltpu.make_async_copy(l, g, sem) if to_global else pltpu.make_async_copy(g, l, sem)
            if wait:
                cp.wait()
            else:
                cp.start()
            return carry

        lax.fori_loop(0, cnt_ref[base], piece, 0)


def _zero_rows(start, units, max_units, zero_ref, xs_hbm, sem, wait):
    pos = start
    for b in range((max_units - 1).bit_length()):
        rows = SEG_ALIGN << b
        bit = lax.bitwise_and(lax.shift_right_logical(units, b), 1)

        @pl.when(bit == 1)
        def _():
            dst = xs_hbm.at[pl.ds(pl.multiple_of(pos, SEG_ALIGN), rows), :]
            cp = pltpu.make_async_copy(zero_ref.at[pl.ds(0, rows), :], dst, sem)
            if wait:
                cp.wait()
            else:
                cp.start()

        pos = pos + bit * rows


def _onehot_rows(lpr_ref, n_rows):
    tm = lpr_ref.shape[1]
    sub = lax.broadcasted_iota(I32, (n_rows, tm), 0)
    lp1 = lpr_ref[0:1, :].astype(I32)
    lp2 = lpr_ref[1:2, :].astype(I32)
    return jnp.where((sub == lp1) | (sub == lp2), 1.0, 0.0).astype(BF16)


def _dispatch_body(cnt_ref, off_ref, dst_ref, pad_start_ref, pad_units_ref, total_ref, x1_ref, lpr_ref,
                   xs_hbm, loc_ref, zero_ref, sems, zsem):
    i = pl.program_id(0)
    last = pl.num_programs(0) - 1
    slot = lax.rem(i, 2)
    lists = (cnt_ref, off_ref, dst_ref)
    perm = _onehot_rows(lpr_ref, loc_ref.shape[1])
    loc_ref[slot] = jnp.dot(perm, x1_ref[...].astype(BF16), preferred_element_type=F32)
    _run_copies(lists, i, loc_ref.at[slot], xs_hbm, sems.at[slot], True, False)

    @pl.when(i > 0)
    def _():
        _run_copies(lists, i - 1, loc_ref.at[1 - slot], xs_hbm, sems.at[1 - slot], True, True)

    def zero_fill(wait):
        def region(e, carry):
            _zero_rows(pad_start_ref[e], pad_units_ref[e], zero_ref.shape[0] // SEG_ALIGN, zero_ref, xs_hbm,
                       zsem, wait)
            return carry
        lax.fori_loop(0, N_EXPERTS, region, 0)
        _zero_tail(total_ref[0], zero_ref, xs_hbm, zsem, wait)

    @pl.when(i == 0)
    def _():
        zero_ref[...] = jnp.zeros_like(zero_ref)
        zero_fill(False)

    @pl.when(i == last)
    def _():
        zero_fill(True)
        _run_copies(lists, i, loc_ref.at[slot], xs_hbm, sems.at[slot], True, True)


def _zero_tail(total, zero_ref, xs_hbm, sem, wait):
    zr = zero_ref.shape[0]
    shift = zr.bit_length() - 1
    assert zr == 1 << shift and xs_hbm.shape[0] % SEG_ALIGN == 0
    dead = xs_hbm.shape[0] - total
    n_full = lax.shift_right_logical(dead, shift)

    def full(k, carry):
        dst = xs_hbm.at[pl.ds(pl.multiple_of(total + k * zr, SEG_ALIGN), zr), :]
        cp = pltpu.make_async_copy(zero_ref, dst, sem)
        if wait:
            cp.wait()
        else:
            cp.start()
        return carry

    lax.fori_loop(0, n_full, full, 0)
    rem = lax.shift_right_logical(dead - n_full * zr, SEG_ALIGN.bit_length() - 1)
    _zero_rows(total + n_full * zr, rem, zr // SEG_ALIGN, zero_ref, xs_hbm, sem, wait)


def _local_rows(tm):
    return TOP_K * tm + N_EXPERTS * SEG_ALIGN


def _dispatch(lists, pad_start, pad_units, total, x1, lpr, m_max):
    n = x1.shape[0]
    tm = min(ROUTE_TM, n)
    return pl.pallas_call(
        _dispatch_body,
        grid_spec=pltpu.PrefetchScalarGridSpec(
            num_scalar_prefetch=6,
            grid=(n // tm,),
            in_specs=[pl.BlockSpec((tm, D_MODEL), lambda i, *_: (i, 0)),
                      pl.BlockSpec((SUBLANES, tm), lambda i, *_: (0, i))],
            out_specs=pl.BlockSpec(memory_space=pl.ANY),
            scratch_shapes=[pltpu.VMEM((2, _local_rows(tm), D_MODEL), F32),
                            pltpu.VMEM((MOE_RT, D_MODEL), F32),
                            pltpu.SemaphoreType.DMA((2,)), pltpu.SemaphoreType.DMA],
        ),
        out_shape=jax.ShapeDtypeStruct((m_max, D_MODEL), F32),
        compiler_params=_cparams(("arbitrary",)),
        name="dispatch",
    )(*lists, pad_start, pad_units, total, x1, lpr)


def _expert_body(exp_ref, live_ref, xs_ref, wg_ref, wu_ref, wd_ref, y_ref, wgb_ref, wub_ref, wdb_ref, cur_ref):
    g = pl.program_id(0)
    e = exp_ref[g]

    @pl.when(g == 0)
    def _():
        cur_ref[0] = -1

    @pl.when(g < live_ref[0])
    def _():
        @pl.when(cur_ref[0] != e)
        def _():
            wgb_ref[...] = wg_ref[...].astype(BF16)
            wub_ref[...] = wu_ref[...].astype(BF16)
            wdb_ref[...] = wd_ref[...].astype(BF16)
            cur_ref[0] = e

        xb = xs_ref[...].astype(BF16)
        hg = jnp.dot(xb, wgb_ref[...], preferred_element_type=F32)
        hu = jnp.dot(xb, wub_ref[...], preferred_element_type=F32)
        hm = (hg * jax.nn.sigmoid(hg) * hu).astype(BF16)
        y_ref[...] = jnp.dot(hm, wdb_ref[...], preferred_element_type=F32)


def _experts(tile_expert, n_live, xs, w_gate, w_up, w_down):
    m = xs.shape[0]
    rt = min(MOE_RT, m)
    rows = lambda g, e, n: (jnp.minimum(g, n[0] - 1), 0)
    return pl.pallas_call(
        _expert_body,
        grid_spec=pltpu.PrefetchScalarGridSpec(
            num_scalar_prefetch=2,
            grid=(m // rt,),
            in_specs=[
                pl.BlockSpec((rt, D_MODEL), rows),
                pl.BlockSpec((None, D_MODEL, D_EXPERT), lambda g, e, n: (e[g], 0, 0)),
                pl.BlockSpec((None, D_MODEL, D_EXPERT), lambda g, e, n: (e[g], 0, 0)),
                pl.BlockSpec((None, D_EXPERT, D_MODEL), lambda g, e, n: (e[g], 0, 0)),
            ],
            out_specs=pl.BlockSpec((rt, D_MODEL), rows),
            scratch_shapes=[
                pltpu.VMEM((D_MODEL, D_EXPERT), BF16),
                pltpu.VMEM((D_MODEL, D_EXPERT), BF16),
                pltpu.VMEM((D_EXPERT, D_MODEL), BF16),
                pltpu.SMEM((1,), I32),
            ],
        ),
        out_shape=jax.ShapeDtypeStruct((m, D_MODEL), F32),
        input_output_aliases={2: 0},
        compiler_params=_cparams(("arbitrary",)),
        name="experts",
    )(tile_expert, n_live, xs, w_gate, w_up, w_down)


def _combine_body(alpha, cnt_ref, off_ref, dst_ref, x1_ref, rw_ref, lpc_ref, g2_ref, b2_ref, ys_hbm, o_ref,
                  loc_ref, sems):
    i = pl.program_id(0)
    slot = lax.rem(i, 2)
    lists = (cnt_ref, off_ref, dst_ref)

    @pl.when(i == 0)
    def _():
        loc_ref[...] = jnp.zeros_like(loc_ref)
        _run_copies(lists, i, loc_ref.at[slot], ys_hbm, sems.at[slot], False, False)

    @pl.when(i + 1 < pl.num_programs(0))
    def _():
        _run_copies(lists, i + 1, loc_ref.at[1 - slot], ys_hbm, sems.at[1 - slot], False, False)

    _run_copies(lists, i, loc_ref.at[slot], ys_hbm, sems.at[slot], False, True)
    tm = x1_ref.shape[0]
    n_rows = loc_ref.shape[1]
    yb = loc_ref[slot].astype(BF16)
    lanes = lax.broadcasted_iota(I32, (tm, n_rows), 1)
    sel = (jnp.where(lanes == lpc_ref[:, 0:1].astype(I32), rw_ref[:, 0:1], 0.0)
           + jnp.where(lanes == lpc_ref[:, 1:2].astype(I32), rw_ref[:, 1:2], 0.0))
    moe = jnp.dot(sel.astype(BF16), yb, preferred_element_type=F32)
    y = alpha * x1_ref[...] + moe
    o_ref[...] = _layer_norm_rows(y, g2_ref[...], b2_ref[...])


def _combine(lists, x1, rw, lpc, ln_g, ln_b, ys, alpha):
    n = x1.shape[0]
    tm = min(ROUTE_TM, n)
    rowmap = lambda i, *_: (i, 0)
    full = lambda i, *_: (0, 0)
    return pl.pallas_call(
        functools.partial(_combine_body, alpha),
        grid_spec=pltpu.PrefetchScalarGridSpec(
            num_scalar_prefetch=3,
            grid=(n // tm,),
            in_specs=[
                pl.BlockSpec((tm, D_MODEL), rowmap),
                pl.BlockSpec((tm, LANES), rowmap),
                pl.BlockSpec((tm, LANES), rowmap),
                pl.BlockSpec((1, D_MODEL), full),
                pl.BlockSpec((1, D_MODEL), full),
                pl.BlockSpec(memory_space=pl.ANY),
            ],
            out_specs=pl.BlockSpec((tm, D_MODEL), rowmap),
            scratch_shapes=[pltpu.VMEM((2, _local_rows(tm), D_MODEL), F32), pltpu.SemaphoreType.DMA((2,))],
        ),
        out_shape=jax.ShapeDtypeStruct((n, D_MODEL), F32),
        compiler_params=_cparams(("arbitrary",)),
        name="combine_ln",
    )(*lists, x1, rw, lpc, ln_g.reshape(1, -1), ln_b.reshape(1, -1), ys)


def _rotary_tables(seq):
    half = RET_DK // 2
    inv = ROPE_BASE ** (-jnp.arange(half, dtype=F32) / half)
    split = min(ROT_SPLIT, seq)
    a_hi = (jnp.arange(seq // split, dtype=F32) * split)[:, None] * inv[None, :]
    a_lo = jnp.arange(split, dtype=F32)[:, None] * inv[None, :]
    ch, sh, cl, sl = jnp.cos(a_hi)[:, None], jnp.sin(a_hi)[:, None], jnp.cos(a_lo)[None], jnp.sin(a_lo)[None]
    return (ch * cl - sh * sl).reshape(seq, half), (sh * cl + ch * sl).reshape(seq, half)


def _router_weights(w_group, b_group, w_exp_router, b_exp_router):
    spare = LANES - N_GROUPS - N_EXPERTS
    w = jnp.pad(jnp.concatenate([w_group, w_exp_router], axis=1), ((0, 0), (0, spare)))
    b = jnp.pad(jnp.concatenate([b_group, b_exp_router]), (0, spare)).reshape(1, LANES)
    w_hi = w.astype(BF16)
    w_lo = (w - w_hi.astype(F32)).astype(BF16)
    return jnp.concatenate([w_hi, w_lo], axis=1), b


def _layer(x, depth, w_in, b_merge, conv_w, conv_b, w_rg_r, b_rg_r, w_rg_i, b_rg_i, lru_lambda,
           w_ret_o, w_lru_o, w_out, ln1_g, ln1_b, w_group, b_group, w_exp_router, b_exp_router,
           w_e_gate, w_e_up, w_e_down, ln2_g, ln2_b):
    B, S, D = x.shape
    n = B * S
    alpha = (2.0 * depth) ** 0.25
    x2 = x.reshape(n, D)
    wb = w_in.astype(BF16)
    cos, sin = _rotary_tables(S)
    gm, lru, ret = _proj_lru(x2, wb, cos, sin, b_merge, conv_w, conv_b, w_rg_r, b_rg_r, w_rg_i, b_rg_i,
                             lru_lambda, S)


    w_rt, b_rt = _router_weights(w_group, b_group, w_exp_router, b_exp_router)
    x1, rw, lpc, lpr, tcnt = _merge(ret, lru, gm, x2, w_ret_o.astype(BF16), w_lru_o.astype(BF16),
                                    w_out.astype(BF16), ln1_g, ln1_b, w_rt, b_rt, alpha)

    tm = min(ROUTE_TM, n)
    n_t = n // tm
    m_max = n_t * _local_rows(tm) + N_EXPERTS * MOE_RT
    rt = min(MOE_RT, m_max)
    cnt = tcnt[:, 0, :N_EXPERTS].astype(I32)
    units = (cnt + (SEG_ALIGN - 1)) // SEG_ALIGN
    run = units * SEG_ALIGN
    sizes = jnp.sum(run, axis=0)
    region = (sizes + (rt - 1)) // rt * rt
    e_end = jnp.cumsum(region)
    e_start = e_end - region
    gdst = e_start[None, :] + jnp.cumsum(run, axis=0) - run
    run_off = jnp.cumsum(run, axis=1) - run
    lists = _run_lists(units, run_off, gdst)
    pad_start = (e_start + sizes).astype(I32)
    pad_units = ((region - sizes) // SEG_ALIGN).astype(I32)
    total = e_end[-1:].astype(I32)
    tile_start = jnp.arange(m_max // rt, dtype=I32) * rt
    n_live = total // rt
    tile_expert = jnp.sum((e_end[None, :] <= jnp.minimum(tile_start, total - rt)[:, None]).astype(I32), axis=1)

    xs = _dispatch(lists, pad_start, pad_units, total, x1, lpr, m_max)
    ys = _experts(tile_expert, n_live, xs, w_e_gate, w_e_up, w_e_down)
    out = _combine(lists, x1, rw, lpc, ln2_g, ln2_b, ys, alpha)
    return out.reshape(B, S, D)


def kernel(x, w_in, b_merge, conv_w, conv_b, w_rg_r, b_rg_r, w_rg_i, b_rg_i, lru_lambda, w_ret_o, w_lru_o, w_out, ln1_g, ln1_b, w_group, b_group, w_exp_router, b_exp_router, w_e_gate, w_e_up, w_e_down, ln2_g, ln2_b):
    depth = w_in.shape[0]
    for l in range(depth):
        x = _layer(x, depth, w_in[l], b_merge[l], conv_w[l], conv_b[l], w_rg_r[l], b_rg_r[l], w_rg_i[l],
                   b_rg_i[l], lru_lambda[l], w_ret_o[l], w_lru_o[l], w_out[l], ln1_g[l], ln1_b[l],
                   w_group[l], b_group[l], w_exp_router[l], b_exp_router[l], w_e_gate[l], w_e_up[l],
                   w_e_down[l], ln2_g[l], ln2_b[l])
    return x
```

```python
import functools

import jax
import jax.numpy as jnp
import numpy as np
from jax import lax
from jax.experimental import pallas as pl
from jax.experimental.pallas import tpu as pltpu

F32 = jnp.float32
BF16 = jnp.bfloat16
I32 = jnp.int32

D_MODEL = 1024
RET_HEADS = 4
RET_DK = 256
RET_DV = 512
RET_QK = RET_HEADS * RET_DK
RET_V = RET_HEADS * RET_DV
ROPE_BASE = 10000.0
LRU_WIDTH = 1536
LRU_BLOCKS = 8
LRU_BLOCK = LRU_WIDTH // LRU_BLOCKS
LRU_PAIR = 2 * LRU_BLOCK
LRU_PAIRS = LRU_BLOCKS // 2
GATE_WIN = 2 * 128
CONV_WIDTH = 4
LRU_C = 8.0
N_GROUPS = 4
EXPERTS_PER_GROUP = 8
N_EXPERTS = N_GROUPS * EXPERTS_PER_GROUP
TOP_K = 2
D_EXPERT = 512
LN_EPS = 1e-5

LANES = 128
SUBLANES = 8
VMEM_LIMIT = 56 * 1024 * 1024

PROJ_TM = 256
ROT_SPLIT = 64
ROUTE_TM = 512
MOE_RT = 512
SEG_ALIGN = SUBLANES
RUN_BITS = (ROUTE_TM // SEG_ALIGN).bit_length()


def _cparams(sem):
    return pltpu.CompilerParams(dimension_semantics=sem, vmem_limit_bytes=VMEM_LIMIT)


PROJ_WIDTHS = (2 * RET_QK, RET_V, RET_V, LRU_WIDTH, LRU_WIDTH, 2 * D_MODEL)


def _sigmoid(x):
    return 0.5 * jnp.tanh(0.5 * x) + 0.5


def _seg_pitch(t):
    g = t // SUBLANES
    units = -(-g // SUBLANES)
    return SUBLANES * (units + 1 - units % 2)


def _proj_lru_body(per_seq, dec_ref, x_ref, w_ref, cos_ref, sin_ref, bm_ref, cw_ref, cb_ref, wr_ref, wi_ref,
                   br_ref, bi_ref, lam_ref, dm_ref, xi_ref, zeta_ref, gm_ref, lru_ref, ret_ref,
                   ubuf_ref, gbuf_ref, a_ref, b_ref, carry_ref, st_ref):
    T = x_ref.shape[0]
    G = T // SUBLANES
    P = _seg_pitch(T)
    ncb = LRU_WIDTH // LANES
    o_qk, o_v, o_sg, o_u, o_gl, o_gm = (int(o) for o in np.cumsum((0,) + PROJ_WIDTHS[:-1]))
    xb = x_ref[...].astype(BF16)

    def seg(start, lo, hi):
        return jnp.dot(xb, w_ref[:, start + lo:start + hi], preferred_element_type=F32)

    @pl.when(lax.rem(pl.program_id(0), per_seq) == 0)
    def _():
        ubuf_ref[0:SUBLANES, :] = jnp.zeros((SUBLANES, LRU_WIDTH), F32)
        carry_ref[...] = jnp.zeros_like(carry_ref)
        st_ref[...] = jnp.zeros_like(st_ref)

    cw = cw_ref[...]
    neg_c_sp = -LRU_C * jax.nn.softplus(-lam_ref[...])
    half = RET_DK // 2
    cos, sin = cos_ref[...], sin_ref[...]

    def lru_inputs(lo, hi):
        ubuf_ref[SUBLANES:SUBLANES + T, lo:hi] = seg(o_u, lo, hi)
        gbuf_ref[:, lo:hi] = jax.nn.gelu(seg(o_gl, lo, hi))
        uc = cb_ref[:, lo:hi] + cw[CONV_WIDTH - 1:CONV_WIDTH, lo:hi] * ubuf_ref[SUBLANES:SUBLANES + T, lo:hi]
        for j in range(CONV_WIDTH - 1):
            back = CONV_WIDTH - 1 - j
            uc = uc + cw[j:j + 1, lo:hi] * ubuf_ref[SUBLANES - back:SUBLANES - back + T, lo:hi]
        ubuf_ref[0:SUBLANES, lo:hi] = ubuf_ref[T:T + SUBLANES, lo:hi]
        return uc

    def lru_pair(p, uc):
        lo = p * LRU_PAIR
        ucb = uc.astype(BF16)
        even = ucb[:, :GATE_WIN]
        odd = ucb[:, LRU_PAIR - GATE_WIN:]

        def gate_map(w_ref):
            e = jnp.dot(even, w_ref[2 * p], preferred_element_type=F32)
            o = jnp.dot(odd, w_ref[2 * p + 1], preferred_element_type=F32)
            return jnp.concatenate([e[:, :LANES], e[:, LANES:] + o[:, :LANES], o[:, LANES:]], axis=1)

        r = _sigmoid(gate_map(wr_ref) + br_ref[:, lo:lo + LRU_PAIR])
        i = _sigmoid(gate_map(wi_ref) + bi_ref[:, lo:lo + LRU_PAIR])
        log_a = r * neg_c_sp[:, lo:lo + LRU_PAIR]
        a = jnp.exp(log_a)
        inp = jnp.sqrt(-jnp.tanh(log_a) * (a * a + 1.0)) * (i * uc)
        for cc in range(LRU_PAIR // LANES):
            cb = p * (LRU_PAIR // LANES) + cc
            for s in range(SUBLANES):
                a_ref[cb, s * P:s * P + G, :] = a[s * G:(s + 1) * G, cc * LANES:(cc + 1) * LANES]
                b_ref[cb, s * P:s * P + G, :] = inp[s * G:(s + 1) * G, cc * LANES:(cc + 1) * LANES]

    def rotary(start, h, scale):
        acc = seg(start, h * RET_DK, (h + 1) * RET_DK)
        t1, t2 = acc[:, :half], acc[:, half:]
        return (jnp.concatenate([t1 * cos - t2 * sin, t1 * sin + t2 * cos], axis=1) * scale).astype(BF16)

    def retention_head(h):
        vc = slice(h * RET_DV, (h + 1) * RET_DV)
        q = rotary(o_qk, h, 1.0)
        k = rotary(o_qk + RET_QK, h, RET_DK ** -0.5)
        v = seg(o_v, h * RET_DV, (h + 1) * RET_DV).astype(BF16)
        g = seg(o_sg, h * RET_DV, (h + 1) * RET_DV)
        scores = lax.dot_general(q, k, (((1,), (1,)), ((), ())), preferred_element_type=F32) * dm_ref[h]
        inner = jnp.dot(scores.astype(BF16), v, preferred_element_type=F32)
        st = st_ref[h]
        cross = jnp.dot(q, st.astype(BF16), preferred_element_type=F32) * xi_ref[h]
        kz = (k.astype(F32) * zeta_ref[h]).astype(BF16)
        upd = lax.dot_general(kz, v, (((0,), (0,)), ((), ())), preferred_element_type=F32)
        st_ref[h] = st * dec_ref[h] + upd
        o = inner + cross
        mu = jnp.mean(o, axis=-1, keepdims=True)
        oc = o - mu
        var = jnp.mean(oc * oc, axis=-1, keepdims=True)
        ret_ref[:, vc] = (g * _sigmoid(g) * (oc * lax.rsqrt(var + LN_EPS))).astype(ret_ref.dtype)

    def merge_gates(lo, hi):
        gm_ref[:, lo:hi] = _sigmoid(seg(o_gm, lo, hi) + bm_ref[:, lo:hi]).astype(gm_ref.dtype)

    assert LRU_PAIRS == RET_HEADS and (2 * D_MODEL) % LRU_PAIRS == 0
    gm_chunk = 2 * D_MODEL // LRU_PAIRS
    for p in range(0, LRU_PAIRS, 2):
        uc2 = lru_inputs(p * LRU_PAIR, (p + 2) * LRU_PAIR)
        for q in range(2):
            merge_gates((p + q) * gm_chunk, (p + q + 1) * gm_chunk)
            lru_pair(p + q, uc2[:, q * LRU_PAIR:(q + 1) * LRU_PAIR])
            retention_head(p + q)

    def step(j, hp):
        hs, ps = hp
        nh, npr = [], []
        for cb in range(ncb):
            a = a_ref[cb, pl.ds(j, SUBLANES, stride=P), :]
            b = b_ref[cb, pl.ds(j, SUBLANES, stride=P), :]
            hn = a * hs[cb] + b
            pn = a * ps[cb]
            b_ref[cb, pl.ds(j, SUBLANES, stride=P), :] = hn
            a_ref[cb, pl.ds(j, SUBLANES, stride=P), :] = pn
            nh.append(hn)
            npr.append(pn)
        return tuple(nh), tuple(npr)

    zeros = tuple(jnp.zeros((SUBLANES, LANES), F32) for _ in range(ncb))
    ones = tuple(jnp.ones((SUBLANES, LANES), F32) for _ in range(ncb))
    h_end, p_end = lax.fori_loop(0, G, step, (zeros, ones), unroll=True)

    for cb in range(ncb):
        cin = carry_ref[:, cb * LANES:(cb + 1) * LANES]
        for s in range(SUBLANES):
            rows = slice(s * G, (s + 1) * G)
            hseg = b_ref[cb, s * P:s * P + G, :] + a_ref[cb, s * P:s * P + G, :] * cin
            gate = gbuf_ref[rows, cb * LANES:(cb + 1) * LANES]
            lru_ref[rows, cb * LANES:(cb + 1) * LANES] = (gate * hseg).astype(lru_ref.dtype)
            cin = h_end[cb][s:s + 1, :] + p_end[cb][s:s + 1, :] * cin
        carry_ref[:, cb * LANES:(cb + 1) * LANES] = cin


def _gate_windows(w):
    spare = GATE_WIN - LRU_BLOCK
    w4 = w.reshape(LRU_PAIRS, 2, LRU_BLOCK, LRU_BLOCK)
    first = jnp.pad(w4[:, 0], ((0, 0), (0, spare), (0, spare)))
    second = jnp.pad(w4[:, 1], ((0, 0), (spare, 0), (spare, 0)))
    return jnp.stack([first, second], axis=1).reshape(LRU_BLOCKS, GATE_WIN, GATE_WIN).astype(BF16)


def _retention_tables(chunk):
    H = RET_HEADS
    log_g = jnp.log1p(-(2.0 ** (-5.0 - jnp.arange(H, dtype=F32))))
    pos = jnp.arange(chunk, dtype=F32)
    diff = pos[:, None] - pos[None, :]
    causal = diff >= 0
    d_mask = jnp.where(causal[None], jnp.exp(log_g[:, None, None] * jnp.where(causal, diff, 0.0)[None]), 0.0)
    xi = jnp.exp(log_g[:, None] * (pos + 1.0)[None])[:, :, None]
    zeta = jnp.exp(log_g[:, None] * (chunk - 1.0 - pos)[None])[:, :, None]
    return d_mask, xi, zeta, jnp.exp(log_g * chunk)


def _proj_lru(x2, w, cos, sin, b_merge, conv_w, conv_b, w_r, b_r, w_i, b_i, lam, seq):
    n, d = x2.shape
    tm = min(PROJ_TM, seq)
    per_seq = seq // tm
    assert sum(PROJ_WIDTHS) == w.shape[1]
    ncb = LRU_WIDTH // LANES
    d_mask, xi, zeta, chunk_decay = _retention_tables(tm)
    rows = lambda width: pl.BlockSpec((tm, width), lambda i: (i, 0))
    full2 = lambda shape: pl.BlockSpec(shape, lambda i: (0, 0))
    full3 = lambda shape: pl.BlockSpec(shape, lambda i: (0, 0, 0))
    pairs = full3((LRU_BLOCKS, GATE_WIN, GATE_WIN))
    rot = pl.BlockSpec((tm, RET_DK // 2), lambda i: (i % per_seq, 0))
    vec = lambda a: a.reshape(1, -1)
    out_widths = (2 * D_MODEL, LRU_WIDTH, RET_V)
    return pl.pallas_call(
        functools.partial(_proj_lru_body, per_seq),
        grid=(n // tm,),
        in_specs=[pl.BlockSpec(memory_space=pltpu.SMEM),
                  rows(d),
                  pl.BlockSpec(w.shape, lambda i: (0, 0), pipeline_mode=pl.Buffered(1)),
                  rot, rot, full2((1, 2 * D_MODEL)),
                  full2((CONV_WIDTH, LRU_WIDTH)), full2((1, LRU_WIDTH)), pairs, pairs,
                  full2((1, LRU_WIDTH)), full2((1, LRU_WIDTH)), full2((1, LRU_WIDTH)),
                  full3((RET_HEADS, tm, tm)), full3((RET_HEADS, tm, 1)), full3((RET_HEADS, tm, 1))],
        out_specs=[rows(width) for width in out_widths],
        out_shape=[jax.ShapeDtypeStruct((n, width), BF16) for width in out_widths],
        scratch_shapes=[
            pltpu.VMEM((tm + SUBLANES, LRU_WIDTH), F32),
            pltpu.VMEM((tm, LRU_WIDTH), F32),
            pltpu.VMEM((ncb, SUBLANES * _seg_pitch(tm), LANES), F32),
            pltpu.VMEM((ncb, SUBLANES * _seg_pitch(tm), LANES), F32),
            pltpu.VMEM((1, LRU_WIDTH), F32),
            pltpu.VMEM((RET_HEADS, RET_DK, RET_DV), F32),
        ],
        compiler_params=_cparams(("arbitrary",)),
        name="proj_mixers",
    )(chunk_decay, x2, w, cos, sin, vec(b_merge), conv_w, vec(conv_b), _gate_windows(w_r),
      _gate_windows(w_i), vec(b_r), vec(b_i), vec(lam), d_mask, xi, zeta)


def _layer_norm_rows(y, g, b):
    mu = jnp.mean(y, axis=-1, keepdims=True)
    yc = y - mu
    var = jnp.mean(yc * yc, axis=-1, keepdims=True)
    return yc * lax.rsqrt(var + LN_EPS) * g + b


def _merge_body(alpha, ret_ref, lru_ref, gm_ref, x_ref, wro_ref, wlo_ref, wo_ref, g1_ref, b1_ref,
                wrt_ref, brt_ref, x1_ref, rw_ref, lpc_ref, lpr_ref, cnt_ref, y_ref):
    @pl.when(pl.program_id(0) == 0)
    def _():
        y_ref[...] = jnp.zeros_like(y_ref)

    x1 = _layer_norm_rows(y_ref[...], g1_ref[...], b1_ref[...])
    x1_ref[...] = x1

    tm = x1.shape[0]
    x_hi = x1.astype(BF16)
    x_lo = (x1 - x_hi.astype(F32)).astype(BF16)
    hh = jnp.dot(x_hi, wrt_ref[...], preferred_element_type=F32)
    lh = jnp.dot(x_lo, wrt_ref[:, :LANES], preferred_element_type=F32)
    lg = hh[:, :LANES] + hh[:, LANES:] + lh + brt_ref[...]
    lane = lax.broadcasted_iota(I32, (tm, LANES), 1)
    big = jnp.int32(LANES)
    neg = jnp.float32(-jnp.inf)
    gmask = lane < N_GROUPS
    gl = jnp.where(gmask, lg, neg)
    gmax = jnp.max(gl, axis=-1, keepdims=True)
    g_idx = jnp.min(jnp.where(gmask & (gl == gmax), lane, big), axis=-1, keepdims=True)
    g_w = 1.0 / jnp.sum(jnp.where(gmask, jnp.exp(gl - gmax), 0.0), axis=-1, keepdims=True)
    e_lo = N_GROUPS + EXPERTS_PER_GROUP * g_idx
    emask = (lane >= e_lo) & (lane < e_lo + EXPERTS_PER_GROUP)
    el = jnp.where(emask, lg, neg)
    v1 = jnp.max(el, axis=-1, keepdims=True)
    i1 = jnp.min(jnp.where(emask & (el == v1), lane, big), axis=-1, keepdims=True)
    emask2 = emask & (lane != i1)
    el2 = jnp.where(emask2, lg, neg)
    v2 = jnp.max(el2, axis=-1, keepdims=True)
    i2 = jnp.min(jnp.where(emask2 & (el2 == v2), lane, big), axis=-1, keepdims=True)
    ex = jnp.exp(v2 - v1)
    den = 1.0 + ex
    w1 = g_w / den
    w2 = g_w * ex / den
    e1 = i1 - N_GROUPS
    e2 = i2 - N_GROUPS
    rw_ref[...] = jnp.where(lane == 0, w1, jnp.where(lane == 1, w2, 0.0))

    oh = (lane == e1).astype(F32) + (lane == e2).astype(F32)
    rowi = lax.broadcasted_iota(I32, (tm, tm), 0)
    coli = lax.broadcasted_iota(I32, (tm, tm), 1)
    tri = jnp.where(coli < rowi, 1.0, 0.0).astype(BF16)
    before = jnp.dot(tri, oh.astype(BF16), preferred_element_type=F32)
    cnt = jnp.sum(oh, axis=0, keepdims=True)
    units = jnp.floor((cnt + (SEG_ALIGN - 1.0)) * (1.0 / SEG_ALIGN))
    er = lax.broadcasted_iota(I32, (LANES, LANES), 0)
    ec = lax.broadcasted_iota(I32, (LANES, LANES), 1)
    upper = jnp.where(er < ec, 1.0, 0.0).astype(BF16)
    offs = SEG_ALIGN * jnp.dot(jnp.broadcast_to(units, (SUBLANES, LANES)).astype(BF16), upper,
                               preferred_element_type=F32)[0:1, :]
    pos = before + offs
    lp1 = jnp.sum(jnp.where(lane == e1, pos, 0.0), axis=-1, keepdims=True)
    lp2 = jnp.sum(jnp.where(lane == e2, pos, 0.0), axis=-1, keepdims=True)
    lpc = jnp.where(lane == 0, lp1, jnp.where(lane == 1, lp2, 0.0))
    lpc_ref[...] = lpc
    lpr_ref[...] = lpc.T[0:SUBLANES, :]
    cnt_ref[...] = jnp.broadcast_to(cnt, cnt_ref.shape)

    pr = jnp.dot(ret_ref[...], wro_ref[...], preferred_element_type=F32)
    pu = jnp.dot(lru_ref[...], wlo_ref[...], preferred_element_type=F32)
    merged = gm_ref[:, :D_MODEL].astype(F32) * pr + gm_ref[:, D_MODEL:].astype(F32) * pu
    y_ref[...] = alpha * x_ref[...] + jnp.dot(merged.astype(BF16), wo_ref[...], preferred_element_type=F32)


def _merge(ret, lru, gm, x2, w_ret_o, w_lru_o, w_out, ln_g, ln_b, w_rt, b_rt, alpha):
    n = x2.shape[0]
    tm = min(ROUTE_TM, n)
    n_t = n // tm
    inmap = lambda i: (jnp.minimum(i, n_t - 1), 0)
    rowmap = lambda i: (jnp.maximum(i - 1, 0), 0)
    full = lambda i: (0, 0)
    return pl.pallas_call(
        functools.partial(_merge_body, alpha),
        grid=(n_t + 1,),
        in_specs=[
            pl.BlockSpec((tm, RET_V), inmap),
            pl.BlockSpec((tm, LRU_WIDTH), inmap),
            pl.BlockSpec((tm, 2 * D_MODEL), inmap),
            pl.BlockSpec((tm, D_MODEL), inmap),
            pl.BlockSpec((RET_V, D_MODEL), full),
            pl.BlockSpec((LRU_WIDTH, D_MODEL), full),
            pl.BlockSpec((D_MODEL, D_MODEL), full),
            pl.BlockSpec((1, D_MODEL), full),
            pl.BlockSpec((1, D_MODEL), full),
            pl.BlockSpec((D_MODEL, 2 * LANES), full),
            pl.BlockSpec((1, LANES), full),
        ],
        out_specs=[
            pl.BlockSpec((tm, D_MODEL), rowmap),
            pl.BlockSpec((tm, LANES), rowmap),
            pl.BlockSpec((tm, LANES), rowmap),
            pl.BlockSpec((SUBLANES, tm), lambda i: (0, jnp.maximum(i - 1, 0))),
            pl.BlockSpec((None, SUBLANES, LANES), lambda i: (jnp.maximum(i - 1, 0), 0, 0)),
        ],
        out_shape=[
            jax.ShapeDtypeStruct((n, D_MODEL), F32),
            jax.ShapeDtypeStruct((n, LANES), F32),
            jax.ShapeDtypeStruct((n, LANES), F32),
            jax.ShapeDtypeStruct((SUBLANES, n), F32),
            jax.ShapeDtypeStruct((n_t, SUBLANES, LANES), F32),
        ],
        scratch_shapes=[pltpu.VMEM((tm, D_MODEL), F32)],
        compiler_params=_cparams(("arbitrary",)),
        name="merge_ln_route",
    )(ret, lru, gm, x2, w_ret_o, w_lru_o, w_out, ln_g.reshape(1, -1), ln_b.reshape(1, -1), w_rt, b_rt)


def _run_lists(units, run_off, gdst):
    k = jnp.arange(N_EXPERTS, dtype=I32)
    cnts, offs, dsts = [], [], []
    for b in range(RUN_BITS):
        bit = (units >> b) & 1
        low = (units & ((1 << b) - 1)) * SEG_ALIGN
        pos = jnp.cumsum(bit, axis=1) - bit
        hit = (bit[:, None, :] == 1) & (pos[:, None, :] == k[None, :, None])
        offs.append(jnp.sum(jnp.where(hit, (run_off + low)[:, None, :], 0), axis=2))
        dsts.append(jnp.sum(jnp.where(hit, (gdst + low)[:, None, :], 0), axis=2))
        cnts.append(jnp.sum(bit, axis=1))
    flat = lambda parts: jnp.stack(parts, axis=1).reshape(-1).astype(I32)
    return flat(cnts), flat(offs), flat(dsts)


def _run_copies(lists, tile, loc_ref, glob_hbm, sem, to_global, wait):
    cnt_ref, off_ref, dst_ref = lists
    for b in range(RUN_BITS):
        rows = SEG_ALIGN << b
        base = tile * RUN_BITS + b

        def piece(k, carry):
            off = pl.multiple_of(off_ref[base * N_EXPERTS + k], SEG_ALIGN)
            dst = pl.multiple_of(dst_ref[base * N_EXPERTS + k], SEG_ALIGN)
            l = loc_ref.at[pl.ds(off, rows), :]
            g = glob_hbm.at[pl.ds(dst, rows), :]
            cp = pltpu.make_async_copy(l, g, sem) if to_global else pltpu.make_async_copy(g, l, sem)
            if wait:
                cp.wait()
            else:
                cp.start()
            return carry

        lax.fori_loop(0, cnt_ref[base], piece, 0)


def _zero_rows(start, units, max_units, zero_ref, xs_hbm, sem, wait):
    pos = start
    for b in range((max_units - 1).bit_length()):
        rows = SEG_ALIGN << b
        bit = lax.bitwise_and(lax.shift_right_logical(units, b), 1)

        @pl.when(bit == 1)
        def _():
            dst = xs_hbm.at[pl.ds(pl.multiple_of(pos, SEG_ALIGN), rows), :]
            cp = pltpu.make_async_copy(zero_ref.at[pl.ds(0, rows), :], dst, sem)
            if wait:
                cp.wait()
            else:
                cp.start()

        pos = pos + bit * rows


def _onehot_rows(lpr_ref, n_rows):
    tm = lpr_ref.shape[1]
    sub = lax.broadcasted_iota(I32, (n_rows, tm), 0)
    lp1 = lpr_ref[0:1, :].astype(I32)
    lp2 = lpr_ref[1:2, :].astype(I32)
    return jnp.where((sub == lp1) | (sub == lp2), 1.0, 0.0).astype(BF16)


def _dispatch_body(cnt_ref, off_ref, dst_ref, pad_start_ref, pad_units_ref, total_ref, x1_ref, lpr_ref,
                   xs_hbm, loc_ref, zero_ref, sems, zsem):
    i = pl.program_id(0)
    last = pl.num_programs(0) - 1
    slot = lax.rem(i, 2)
    lists = (cnt_ref, off_ref, dst_ref)
    perm = _onehot_rows(lpr_ref, loc_ref.shape[1])
    loc_ref[slot] = jnp.dot(perm, x1_ref[...].astype(BF16), preferred_element_type=F32)
    _run_copies(lists, i, loc_ref.at[slot], xs_hbm, sems.at[slot], True, False)

    @pl.when(i > 0)
    def _():
        _run_copies(lists, i - 1, loc_ref.at[1 - slot], xs_hbm, sems.at[1 - slot], True, True)

    def zero_fill(wait):
        def region(e, carry):
            _zero_rows(pad_start_ref[e], pad_units_ref[e], zero_ref.shape[0] // SEG_ALIGN, zero_ref, xs_hbm,
                       zsem, wait)
            return carry
        lax.fori_loop(0, N_EXPERTS, region, 0)
        _zero_tail(total_ref[0], zero_ref, xs_hbm, zsem, wait)

    @pl.when(i == 0)
    def _():
        zero_ref[...] = jnp.zeros_like(zero_ref)
        zero_fill(False)

    @pl.when(i == last)
    def _():
        zero_fill(True)
        _run_copies(lists, i, loc_ref.at[slot], xs_hbm, sems.at[slot], True, True)


def _zero_tail(total, zero_ref, xs_hbm, sem, wait):
    zr = zero_ref.shape[0]
    shift = zr.bit_length() - 1
    assert zr == 1 << shift and xs_hbm.shape[0] % SEG_ALIGN == 0
    dead = xs_hbm.shape[0] - total
    n_full = lax.shift_right_logical(dead, shift)

    def full(k, carry):
        dst = xs_hbm.at[pl.ds(pl.multiple_of(total + k * zr, SEG_ALIGN), zr), :]
        cp = pltpu.make_async_copy(zero_ref, dst, sem)
        if wait:
            cp.wait()
        else:
            cp.start()
        return carry

    lax.fori_loop(0, n_full, full, 0)
    rem = lax.shift_right_logical(dead - n_full * zr, SEG_ALIGN.bit_length() - 1)
    _zero_rows(total + n_full * zr, rem, zr // SEG_ALIGN, zero_ref, xs_hbm, sem, wait)


def _local_rows(tm):
    return TOP_K * tm + N_EXPERTS * SEG_ALIGN


def _dispatch(lists, pad_start, pad_units, total, x1, lpr, m_max):
    n = x1.shape[0]
    tm = min(ROUTE_TM, n)
    return pl.pallas_call(
        _dispatch_body,
        grid_spec=pltpu.PrefetchScalarGridSpec(
            num_scalar_prefetch=6,
            grid=(n // tm,),
            in_specs=[pl.BlockSpec((tm, D_MODEL), lambda i, *_: (i, 0)),
                      pl.BlockSpec((SUBLANES, tm), lambda i, *_: (0, i))],
            out_specs=pl.BlockSpec(memory_space=pl.ANY),
            scratch_shapes=[pltpu.VMEM((2, _local_rows(tm), D_MODEL), F32),
                            pltpu.VMEM((MOE_RT, D_MODEL), F32),
                            pltpu.SemaphoreType.DMA((2,)), pltpu.SemaphoreType.DMA],
        ),
        out_shape=jax.ShapeDtypeStruct((m_max, D_MODEL), F32),
        compiler_params=_cparams(("arbitrary",)),
        name="dispatch",
    )(*lists, pad_start, pad_units, total, x1, lpr)


def _expert_body(exp_ref, live_ref, xs_ref, wg_ref, wu_ref, wd_ref, y_ref, wgb_ref, wub_ref, wdb_ref, cur_ref):
    g = pl.program_id(0)
    e = exp_ref[g]

    @pl.when(g == 0)
    def _():
        cur_ref[0] = -1

    @pl.when(g < live_ref[0])
    def _():
        @pl.when(cur_ref[0] != e)
        def _():
            wgb_ref[...] = wg_ref[...].astype(BF16)
            wub_ref[...] = wu_ref[...].astype(BF16)
            wdb_ref[...] = wd_ref[...].astype(BF16)
            cur_ref[0] = e

        xb = xs_ref[...].astype(BF16)
        hg = jnp.dot(xb, wgb_ref[...], preferred_element_type=F32)
        hu = jnp.dot(xb, wub_ref[...], preferred_element_type=F32)
        hm = (hg * jax.nn.sigmoid(hg) * hu).astype(BF16)
        y_ref[...] = jnp.dot(hm, wdb_ref[...], preferred_element_type=F32)


def _experts(tile_expert, n_live, xs, w_gate, w_up, w_down):
    m = xs.shape[0]
    rt = min(MOE_RT, m)
    rows = lambda g, e, n: (jnp.minimum(g, n[0] - 1), 0)
    return pl.pallas_call(
        _expert_body,
        grid_spec=pltpu.PrefetchScalarGridSpec(
            num_scalar_prefetch=2,
            grid=(m // rt,),
            in_specs=[
                pl.BlockSpec((rt, D_MODEL), rows),
                pl.BlockSpec((None, D_MODEL, D_EXPERT), lambda g, e, n: (e[g], 0, 0)),
                pl.BlockSpec((None, D_MODEL, D_EXPERT), lambda g, e, n: (e[g], 0, 0)),
                pl.BlockSpec((None, D_EXPERT, D_MODEL), lambda g, e, n: (e[g], 0, 0)),
            ],
            out_specs=pl.BlockSpec((rt, D_MODEL), rows),
            scratch_shapes=[
                pltpu.VMEM((D_MODEL, D_EXPERT), BF16),
                pltpu.VMEM((D_MODEL, D_EXPERT), BF16),
                pltpu.VMEM((D_EXPERT, D_MODEL), BF16),
                pltpu.SMEM((1,), I32),
            ],
        ),
        out_shape=jax.ShapeDtypeStruct((m, D_MODEL), F32),
        input_output_aliases={2: 0},
        compiler_params=_cparams(("arbitrary",)),
        name="experts",
    )(tile_expert, n_live, xs, w_gate, w_up, w_down)


def _combine_body(alpha, cnt_ref, off_ref, dst_ref, x1_ref, rw_ref, lpc_ref, g2_ref, b2_ref, ys_hbm, o_ref,
                  loc_ref, sems):
    i = pl.program_id(0)
    slot = lax.rem(i, 2)
    lists = (cnt_ref, off_ref, dst_ref)

    @pl.when(i == 0)
    def _():
        loc_ref[...] = jnp.zeros_like(loc_ref)
        _run_copies(lists, i, loc_ref.at[slot], ys_hbm, sems.at[slot], False, False)

    @pl.when(i + 1 < pl.num_programs(0))
    def _():
        _run_copies(lists, i + 1, loc_ref.at[1 - slot], ys_hbm, sems.at[1 - slot], False, False)

    _run_copies(lists, i, loc_ref.at[slot], ys_hbm, sems.at[slot], False, True)
    tm = x1_ref.shape[0]
    n_rows = loc_ref.shape[1]
    yb = loc_ref[slot].astype(BF16)
    lanes = lax.broadcasted_iota(I32, (tm, n_rows), 1)
    sel = (jnp.where(lanes == lpc_ref[:, 0:1].astype(I32), rw_ref[:, 0:1], 0.0)
           + jnp.where(lanes == lpc_ref[:, 1:2].astype(I32), rw_ref[:, 1:2], 0.0))
    moe = jnp.dot(sel.astype(BF16), yb, preferred_element_type=F32)
    y = alpha * x1_ref[...] + moe
    o_ref[...] = _layer_norm_rows(y, g2_ref[...], b2_ref[...])


def _combine(lists, x1, rw, lpc, ln_g, ln_b, ys, alpha):
    n = x1.shape[0]
    tm = min(ROUTE_TM, n)
    rowmap = lambda i, *_: (i, 0)
    full = lambda i, *_: (0, 0)
    return pl.pallas_call(
        functools.partial(_combine_body, alpha),
        grid_spec=pltpu.PrefetchScalarGridSpec(
            num_scalar_prefetch=3,
            grid=(n // tm,),
            in_specs=[
                pl.BlockSpec((tm, D_MODEL), rowmap),
                pl.BlockSpec((tm, LANES), rowmap),
                pl.BlockSpec((tm, LANES), rowmap),
                pl.BlockSpec((1, D_MODEL), full),
                pl.BlockSpec((1, D_MODEL), full),
                pl.BlockSpec(memory_space=pl.ANY),
            ],
            out_specs=pl.BlockSpec((tm, D_MODEL), rowmap),
            scratch_shapes=[pltpu.VMEM((2, _local_rows(tm), D_MODEL), F32), pltpu.SemaphoreType.DMA((2,))],
        ),
        out_shape=jax.ShapeDtypeStruct((n, D_MODEL), F32),
        compiler_params=_cparams(("arbitrary",)),
        name="combine_ln",
    )(*lists, x1, rw, lpc, ln_g.reshape(1, -1), ln_b.reshape(1, -1), ys)


def _rotary_tables(seq):
    half = RET_DK // 2
    inv = ROPE_BASE ** (-jnp.arange(half, dtype=F32) / half)
    split = min(ROT_SPLIT, seq)
    a_hi = (jnp.arange(seq // split, dtype=F32) * split)[:, None] * inv[None, :]
    a_lo = jnp.arange(split, dtype=F32)[:, None] * inv[None, :]
    ch, sh, cl, sl = jnp.cos(a_hi)[:, None], jnp.sin(a_hi)[:, None], jnp.cos(a_lo)[None], jnp.sin(a_lo)[None]
    return (ch * cl - sh * sl).reshape(seq, half), (sh * cl + ch * sl).reshape(seq, half)


def _router_weights(w_group, b_group, w_exp_router, b_exp_router):
    spare = LANES - N_GROUPS - N_EXPERTS
    w = jnp.pad(jnp.concatenate([w_group, w_exp_router], axis=1), ((0, 0), (0, spare)))
    b = jnp.pad(jnp.concatenate([b_group, b_exp_router]), (0, spare)).reshape(1, LANES)
    w_hi = w.astype(BF16)
    w_lo = (w - w_hi.astype(F32)).astype(BF16)
    return jnp.concatenate([w_hi, w_lo], axis=1), b


def _layer(x, depth, w_in, b_merge, conv_w, conv_b, w_rg_r, b_rg_r, w_rg_i, b_rg_i, lru_lambda,
           w_ret_o, w_lru_o, w_out, ln1_g, ln1_b, w_group, b_group, w_exp_router, b_exp_router,
           w_e_gate, w_e_up, w_e_down, ln2_g, ln2_b):
    B, S, D = x.shape
    n = B * S
    alpha = (2.0 * depth) ** 0.25
    x2 = x.reshape(n, D)
    wb = w_in.astype(BF16)
    cos, sin = _rotary_tables(S)
    gm, lru, ret = _proj_lru(x2, wb, cos, sin, b_merge, conv_w, conv_b, w_rg_r, b_rg_r, w_rg_i, b_rg_i,
                             lru_lambda, S)


    w_rt, b_rt = _router_weights(w_group, b_group, w_exp_router, b_exp_router)
    x1, rw, lpc, lpr, tcnt = _merge(ret, lru, gm, x2, w_ret_o.astype(BF16), w_lru_o.astype(BF16),
                                    w_out.astype(BF16), ln1_g, ln1_b, w_rt, b_rt, alpha)

    tm = min(ROUTE_TM, n)
    n_t = n // tm
    m_max = n_t * _local_rows(tm) + N_EXPERTS * MOE_RT
    rt = min(MOE_RT, m_max)
    cnt = tcnt[:, 0, :N_EXPERTS].astype(I32)
    units = (cnt + (SEG_ALIGN - 1)) // SEG_ALIGN
    run = units * SEG_ALIGN
    sizes = jnp.sum(run, axis=0)
    region = (sizes + (rt - 1)) // rt * rt
    e_end = jnp.cumsum(region)
    e_start = e_end - region
    gdst = e_start[None, :] + jnp.cumsum(run, axis=0) - run
    run_off = jnp.cumsum(run, axis=1) - run
    lists = _run_lists(units, run_off, gdst)
    pad_start = (e_start + sizes).astype(I32)
    pad_units = ((region - sizes) // SEG_ALIGN).astype(I32)
    total = e_end[-1:].astype(I32)
    tile_start = jnp.arange(m_max // rt, dtype=I32) * rt
    n_live = total // rt
    tile_expert = jnp.sum((e_end[None, :] <= jnp.minimum(tile_start, total - rt)[:, None]).astype(I32), axis=1)

    xs = _dispatch(lists, pad_start, pad_units, total, x1, lpr, m_max)
    ys = _experts(tile_expert, n_live, xs, w_e_gate, w_e_up, w_e_down)
    out = _combine(lists, x1, rw, lpc, ln2_g, ln2_b, ys, alpha)
    return out.reshape(B, S, D)


def kernel(x, w_in, b_merge, conv_w, conv_b, w_rg_r, b_rg_r, w_rg_i, b_rg_i, lru_lambda, w_ret_o, w_lru_o, w_out, ln1_g, ln1_b, w_group, b_group, w_exp_router, b_exp_router, w_e_gate, w_e_up, w_e_down, ln2_g, ln2_b):
    depth = w_in.shape[0]
    for l in range(depth):
        x = _layer(x, depth, w_in[l], b_merge[l], conv_w[l], conv_b[l], w_rg_r[l], b_rg_r[l], w_rg_i[l],
                   b_rg_i[l], lru_lambda[l], w_ret_o[l], w_lru_o[l], w_out[l], ln1_g[l], ln1_b[l],
                   w_group[l], b_group[l], w_exp_router[l], b_exp_router[l], w_e_gate[l], w_e_up[l],
                   w_e_down[l], ln2_g[l], ln2_b[l])
    return x
```

```python
import functools

import jax
import jax.numpy as jnp
import numpy as np
from jax import lax
from jax.experimental import pallas as pl
from jax.experimental.pallas import tpu as pltpu

F32 = jnp.float32
BF16 = jnp.bfloat16
I32 = jnp.int32

D_MODEL = 1024
RET_HEADS = 4
RET_DK = 256
RET_DV = 512
RET_QK = RET_HEADS * RET_DK
RET_V = RET_HEADS * RET_DV
ROPE_BASE = 10000.0
LRU_WIDTH = 1536
LRU_BLOCKS = 8
LRU_BLOCK = LRU_WIDTH // LRU_BLOCKS
LRU_PAIR = 2 * LRU_BLOCK
LRU_PAIRS = LRU_BLOCKS // 2
GATE_WIN = 2 * 128
CONV_WIDTH = 4
LRU_C = 8.0
N_GROUPS = 4
EXPERTS_PER_GROUP = 8
N_EXPERTS = N_GROUPS * EXPERTS_PER_GROUP
TOP_K = 2
D_EXPERT = 512
LN_EPS = 1e-5

LANES = 128
SUBLANES = 8
VMEM_LIMIT = 56 * 1024 * 1024

PROJ_TM = 256
ROT_SPLIT = 64
ROUTE_TM = 512
MOE_RT = 512
SEG_ALIGN = SUBLANES
RUN_BITS = (ROUTE_TM // SEG_ALIGN).bit_length()


def _cparams(sem):
    return pltpu.CompilerParams(dimension_semantics=sem, vmem_limit_bytes=VMEM_LIMIT)


PROJ_WIDTHS = (2 * RET_QK, RET_V, RET_V, LRU_WIDTH, LRU_WIDTH, 2 * D_MODEL)


def _sigmoid(x):
    return 0.5 * jnp.tanh(0.5 * x) + 0.5


def _seg_pitch(t):
    g = t // SUBLANES
    units = -(-g // SUBLANES)
    return SUBLANES * (units + 1 - units % 2)


def _proj_lru_body(per_seq, dec_ref, x_ref, w_ref, cos_ref, sin_ref, bm_ref, cw_ref, cb_ref, wr_ref, wi_ref,
                   br_ref, bi_ref, lam_ref, dm_ref, xi_ref, zeta_ref, gm_ref, lru_ref, ret_ref,
                   ubuf_ref, gbuf_ref, a_ref, b_ref, carry_ref, st_ref):
    T = x_ref.shape[0]
    G = T // SUBLANES
    P = _seg_pitch(T)
    ncb = LRU_WIDTH // LANES
    o_qk, o_v, o_sg, o_u, o_gl, o_gm = (int(o) for o in np.cumsum((0,) + PROJ_WIDTHS[:-1]))
    xb = x_ref[...].astype(BF16)

    def seg(start, lo, hi):
        return jnp.dot(xb, w_ref[:, start + lo:start + hi], preferred_element_type=F32)

    @pl.when(lax.rem(pl.program_id(0), per_seq) == 0)
    def _():
        ubuf_ref[0:SUBLANES, :] = jnp.zeros((SUBLANES, LRU_WIDTH), F32)
        carry_ref[...] = jnp.zeros_like(carry_ref)
        st_ref[...] = jnp.zeros_like(st_ref)

    cw = cw_ref[...]
    neg_c_sp = -LRU_C * jax.nn.softplus(-lam_ref[...])
    half = RET_DK // 2
    cos, sin = cos_ref[...], sin_ref[...]

    def lru_inputs(lo, hi):
        ubuf_ref[SUBLANES:SUBLANES + T, lo:hi] = seg(o_u, lo, hi)
        gbuf_ref[:, lo:hi] = jax.nn.gelu(seg(o_gl, lo, hi))
        uc = cb_ref[:, lo:hi] + cw[CONV_WIDTH - 1:CONV_WIDTH, lo:hi] * ubuf_ref[SUBLANES:SUBLANES + T, lo:hi]
        for j in range(CONV_WIDTH - 1):
            back = CONV_WIDTH - 1 - j
            uc = uc + cw[j:j + 1, lo:hi] * ubuf_ref[SUBLANES - back:SUBLANES - back + T, lo:hi]
        ubuf_ref[0:SUBLANES, lo:hi] = ubuf_ref[T:T + SUBLANES, lo:hi]
        return uc

    def lru_pair(p, uc):
        lo = p * LRU_PAIR
        ucb = uc.astype(BF16)
        even = ucb[:, :GATE_WIN]
        odd = ucb[:, LRU_PAIR - GATE_WIN:]

        def gate_map(w_ref):
            e = jnp.dot(even, w_ref[2 * p], preferred_element_type=F32)
            o = jnp.dot(odd, w_ref[2 * p + 1], preferred_element_type=F32)
            return jnp.concatenate([e[:, :LANES], e[:, LANES:] + o[:, :LANES], o[:, LANES:]], axis=1)

        r = _sigmoid(gate_map(wr_ref) + br_ref[:, lo:lo + LRU_PAIR])
        i = _sigmoid(gate_map(wi_ref) + bi_ref[:, lo:lo + LRU_PAIR])
        log_a = r * neg_c_sp[:, lo:lo + LRU_PAIR]
        a = jnp.exp(log_a)
        inp = jnp.sqrt(-jnp.tanh(log_a) * (a * a + 1.0)) * (i * uc)
        for cc in range(LRU_PAIR // LANES):
            cb = p * (LRU_PAIR // LANES) + cc
            for s in range(SUBLANES):
                a_ref[cb, s * P:s * P + G, :] = a[s * G:(s + 1) * G, cc * LANES:(cc + 1) * LANES]
                b_ref[cb, s * P:s * P + G, :] = inp[s * G:(s + 1) * G, cc * LANES:(cc + 1) * LANES]

    def rotary(start, h, scale):
        acc = seg(start, h * RET_DK, (h + 1) * RET_DK)
        t1, t2 = acc[:, :half], acc[:, half:]
        return (jnp.concatenate([t1 * cos - t2 * sin, t1 * sin + t2 * cos], axis=1) * scale).astype(BF16)

    def retention_head(h):
        vc = slice(h * RET_DV, (h + 1) * RET_DV)
        q = rotary(o_qk, h, 1.0)
        k = rotary(o_qk + RET_QK, h, RET_DK ** -0.5)
        v = seg(o_v, h * RET_DV, (h + 1) * RET_DV).astype(BF16)
        g = seg(o_sg, h * RET_DV, (h + 1) * RET_DV)
        scores = lax.dot_general(q, k, (((1,), (1,)), ((), ())), preferred_element_type=F32) * dm_ref[h]
        inner = jnp.dot(scores.astype(BF16), v, preferred_element_type=F32)
        st = st_ref[h]
        cross = jnp.dot(q, st.astype(BF16), preferred_element_type=F32) * xi_ref[h]
        kz = (k.astype(F32) * zeta_ref[h]).astype(BF16)
        upd = lax.dot_general(kz, v, (((0,), (0,)), ((), ())), preferred_element_type=F32)
        st_ref[h] = st * dec_ref[h] + upd
        o = inner + cross
        mu = jnp.mean(o, axis=-1, keepdims=True)
        oc = o - mu
        var = jnp.mean(oc * oc, axis=-1, keepdims=True)
        ret_ref[:, vc] = (g * _sigmoid(g) * (oc * lax.rsqrt(var + LN_EPS))).astype(ret_ref.dtype)

    def merge_gates(lo, hi):
        gm_ref[:, lo:hi] = _sigmoid(seg(o_gm, lo, hi) + bm_ref[:, lo:hi]).astype(gm_ref.dtype)

    assert LRU_PAIRS == RET_HEADS and (2 * D_MODEL) % LRU_PAIRS == 0
    gm_chunk = 2 * D_MODEL // LRU_PAIRS
    for p in range(0, LRU_PAIRS, 2):
        uc2 = lru_inputs(p * LRU_PAIR, (p + 2) * LRU_PAIR)
        for q in range(2):
            merge_gates((p + q) * gm_chunk, (p + q + 1) * gm_chunk)
            lru_pair(p + q, uc2[:, q * LRU_PAIR:(q + 1) * LRU_PAIR])
            retention_head(p + q)

    def step(j, hp):
        hs, ps = hp
        nh, npr = [], []
        for cb in range(ncb):
            a = a_ref[cb, pl.ds(j, SUBLANES, stride=P), :]
            b = b_ref[cb, pl.ds(j, SUBLANES, stride=P), :]
            hn = a * hs[cb] + b
            pn = a * ps[cb]
            b_ref[cb, pl.ds(j, SUBLANES, stride=P), :] = hn
            a_ref[cb, pl.ds(j, SUBLANES, stride=P), :] = pn
            nh.append(hn)
            npr.append(pn)
        return tuple(nh), tuple(npr)

    zeros = tuple(jnp.zeros((SUBLANES, LANES), F32) for _ in range(ncb))
    ones = tuple(jnp.ones((SUBLANES, LANES), F32) for _ in range(ncb))
    h_end, p_end = lax.fori_loop(0, G, step, (zeros, ones), unroll=True)

    for cb in range(ncb):
        cin = carry_ref[:, cb * LANES:(cb + 1) * LANES]
        for s in range(SUBLANES):
            rows = slice(s * G, (s + 1) * G)
            hseg = b_ref[cb, s * P:s * P + G, :] + a_ref[cb, s * P:s * P + G, :] * cin
            gate = gbuf_ref[rows, cb * LANES:(cb + 1) * LANES]
            lru_ref[rows, cb * LANES:(cb + 1) * LANES] = (gate * hseg).astype(lru_ref.dtype)
            cin = h_end[cb][s:s + 1, :] + p_end[cb][s:s + 1, :] * cin
        carry_ref[:, cb * LANES:(cb + 1) * LANES] = cin


def _gate_windows(w):
    spare = GATE_WIN - LRU_BLOCK
    w4 = w.reshape(LRU_PAIRS, 2, LRU_BLOCK, LRU_BLOCK)
    first = jnp.pad(w4[:, 0], ((0, 0), (0, spare), (0, spare)))
    second = jnp.pad(w4[:, 1], ((0, 0), (spare, 0), (spare, 0)))
    return jnp.stack([first, second], axis=1).reshape(LRU_BLOCKS, GATE_WIN, GATE_WIN).astype(BF16)


def _retention_tables(chunk):
    H = RET_HEADS
    log_g = jnp.log1p(-(2.0 ** (-5.0 - jnp.arange(H, dtype=F32))))
    pos = jnp.arange(chunk, dtype=F32)
    diff = pos[:, None] - pos[None, :]
    causal = diff >= 0
    d_mask = jnp.where(causal[None], jnp.exp(log_g[:, None, None] * jnp.where(causal, diff, 0.0)[None]), 0.0)
    xi = jnp.exp(log_g[:, None] * (pos + 1.0)[None])[:, :, None]
    zeta = jnp.exp(log_g[:, None] * (chunk - 1.0 - pos)[None])[:, :, None]
    return d_mask, xi, zeta, jnp.exp(log_g * chunk)


def _proj_lru(x2, w, cos, sin, b_merge, conv_w, conv_b, w_r, b_r, w_i, b_i, lam, seq):
    n, d = x2.shape
    tm = min(PROJ_TM, seq)
    per_seq = seq // tm
    assert sum(PROJ_WIDTHS) == w.shape[1]
    ncb = LRU_WIDTH // LANES
    d_mask, xi, zeta, chunk_decay = _retention_tables(tm)
    rows = lambda width: pl.BlockSpec((tm, width), lambda i: (i, 0))
    full2 = lambda shape: pl.BlockSpec(shape, lambda i: (0, 0))
    full3 = lambda shape: pl.BlockSpec(shape, lambda i: (0, 0, 0))
    pairs = full3((LRU_BLOCKS, GATE_WIN, GATE_WIN))
    rot = pl.BlockSpec((tm, RET_DK // 2), lambda i: (i % per_seq, 0))
    vec = lambda a: a.reshape(1, -1)
    out_widths = (2 * D_MODEL, LRU_WIDTH, RET_V)
    return pl.pallas_call(
        functools.partial(_proj_lru_body, per_seq),
        grid=(n // tm,),
        in_specs=[pl.BlockSpec(memory_space=pltpu.SMEM),
                  rows(d),
                  pl.BlockSpec(w.shape, lambda i: (0, 0), pipeline_mode=pl.Buffered(1)),
                  rot, rot, full2((1, 2 * D_MODEL)),
                  full2((CONV_WIDTH, LRU_WIDTH)), full2((1, LRU_WIDTH)), pairs, pairs,
                  full2((1, LRU_WIDTH)), full2((1, LRU_WIDTH)), full2((1, LRU_WIDTH)),
                  full3((RET_HEADS, tm, tm)), full3((RET_HEADS, tm, 1)), full3((RET_HEADS, tm, 1))],
        out_specs=[rows(width) for width in out_widths],
        out_shape=[jax.ShapeDtypeStruct((n, width), BF16) for width in out_widths],
        scratch_shapes=[
            pltpu.VMEM((tm + SUBLANES, LRU_WIDTH), F32),
            pltpu.VMEM((tm, LRU_WIDTH), F32),
            pltpu.VMEM((ncb, SUBLANES * _seg_pitch(tm), LANES), F32),
            pltpu.VMEM((ncb, SUBLANES * _seg_pitch(tm), LANES), F32),
            pltpu.VMEM((1, LRU_WIDTH), F32),
            pltpu.VMEM((RET_HEADS, RET_DK, RET_DV), F32),
        ],
        compiler_params=_cparams(("arbitrary",)),
        name="proj_mixers",
    )(chunk_decay, x2, w, cos, sin, vec(b_merge), conv_w, vec(conv_b), _gate_windows(w_r),
      _gate_windows(w_i), vec(b_r), vec(b_i), vec(lam), d_mask, xi, zeta)


def _layer_norm_rows(y, g, b):
    mu = jnp.mean(y, axis=-1, keepdims=True)
    yc = y - mu
    var = jnp.mean(yc * yc, axis=-1, keepdims=True)
    return yc * lax.rsqrt(var + LN_EPS) * g + b


def _merge_body(alpha, ret_ref, lru_ref, gm_ref, x_ref, wro_ref, wlo_ref, wo_ref, g1_ref, b1_ref,
                wrt_ref, brt_ref, x1_ref, rw_ref, lpc_ref, lpr_ref, cnt_ref, y_ref, mb_ref):
    @pl.when(pl.program_id(0) == 0)
    def _():
        y_ref[...] = jnp.zeros_like(y_ref)

    x1 = _layer_norm_rows(y_ref[...], g1_ref[...], b1_ref[...])
    x1_ref[...] = x1

    n_chunks = 4
    chunk = D_MODEL // n_chunks

    def merged_chunk(c):
        cols = slice(c * chunk, (c + 1) * chunk)
        gcols = slice(D_MODEL + c * chunk, D_MODEL + (c + 1) * chunk)
        pr = jnp.dot(ret_ref[...], wro_ref[:, cols], preferred_element_type=F32)
        pu = jnp.dot(lru_ref[...], wlo_ref[:, cols], preferred_element_type=F32)
        mb_ref[:, cols] = (gm_ref[:, cols].astype(F32) * pr + gm_ref[:, gcols].astype(F32) * pu).astype(BF16)

    merged_chunk(0)
    tm = x1.shape[0]
    x_hi = x1.astype(BF16)
    x_lo = (x1 - x_hi.astype(F32)).astype(BF16)
    hh = jnp.dot(x_hi, wrt_ref[...], preferred_element_type=F32)
    lh = jnp.dot(x_lo, wrt_ref[:, :LANES], preferred_element_type=F32)
    lg = hh[:, :LANES] + hh[:, LANES:] + lh + brt_ref[...]
    merged_chunk(1)
    lane = lax.broadcasted_iota(I32, (tm, LANES), 1)
    big = jnp.int32(LANES)
    neg = jnp.float32(-jnp.inf)
    gmask = lane < N_GROUPS
    gl = jnp.where(gmask, lg, neg)
    gmax = jnp.max(gl, axis=-1, keepdims=True)
    g_idx = jnp.min(jnp.where(gmask & (gl == gmax), lane, big), axis=-1, keepdims=True)
    g_w = 1.0 / jnp.sum(jnp.where(gmask, jnp.exp(gl - gmax), 0.0), axis=-1, keepdims=True)
    e_lo = N_GROUPS + EXPERTS_PER_GROUP * g_idx
    emask = (lane >= e_lo) & (lane < e_lo + EXPERTS_PER_GROUP)
    el = jnp.where(emask, lg, neg)
    v1 = jnp.max(el, axis=-1, keepdims=True)
    i1 = jnp.min(jnp.where(emask & (el == v1), lane, big), axis=-1, keepdims=True)
    emask2 = emask & (lane != i1)
    el2 = jnp.where(emask2, lg, neg)
    v2 = jnp.max(el2, axis=-1, keepdims=True)
    i2 = jnp.min(jnp.where(emask2 & (el2 == v2), lane, big), axis=-1, keepdims=True)
    ex = jnp.exp(v2 - v1)
    den = 1.0 + ex
    w1 = g_w / den
    w2 = g_w * ex / den
    e1 = i1 - N_GROUPS
    e2 = i2 - N_GROUPS
    rw_ref[...] = jnp.where(lane == 0, w1, jnp.where(lane == 1, w2, 0.0))
    merged_chunk(2)

    oh = (lane == e1).astype(F32) + (lane == e2).astype(F32)
    rowi = lax.broadcasted_iota(I32, (tm, tm), 0)
    coli = lax.broadcasted_iota(I32, (tm, tm), 1)
    tri = jnp.where(coli < rowi, 1.0, 0.0).astype(BF16)
    before = jnp.dot(tri, oh.astype(BF16), preferred_element_type=F32)
    cnt = jnp.sum(oh, axis=0, keepdims=True)
    units = jnp.floor((cnt + (SEG_ALIGN - 1.0)) * (1.0 / SEG_ALIGN))
    er = lax.broadcasted_iota(I32, (LANES, LANES), 0)
    ec = lax.broadcasted_iota(I32, (LANES, LANES), 1)
    upper = jnp.where(er < ec, 1.0, 0.0).astype(BF16)
    offs = SEG_ALIGN * jnp.dot(jnp.broadcast_to(units, (SUBLANES, LANES)).astype(BF16), upper,
                               preferred_element_type=F32)[0:1, :]
    pos = before + offs
    lp1 = jnp.sum(jnp.where(lane == e1, pos, 0.0), axis=-1, keepdims=True)
    lp2 = jnp.sum(jnp.where(lane == e2, pos, 0.0), axis=-1, keepdims=True)
    lpc = jnp.where(lane == 0, lp1, jnp.where(lane == 1, lp2, 0.0))
    lpc_ref[...] = lpc
    lpr_ref[...] = lpc.T[0:SUBLANES, :]
    cnt_ref[...] = jnp.broadcast_to(cnt, cnt_ref.shape)

    merged_chunk(3)

    for c in range(n_chunks):
        cols = slice(c * chunk, (c + 1) * chunk)
        y_ref[:, cols] = alpha * x_ref[:, cols] + jnp.dot(mb_ref[...], wo_ref[:, cols], preferred_element_type=F32)


def _merge(ret, lru, gm, x2, w_ret_o, w_lru_o, w_out, ln_g, ln_b, w_rt, b_rt, alpha):
    n = x2.shape[0]
    tm = min(ROUTE_TM, n)
    n_t = n // tm
    inmap = lambda i: (jnp.minimum(i, n_t - 1), 0)
    rowmap = lambda i: (jnp.maximum(i - 1, 0), 0)
    full = lambda i: (0, 0)
    return pl.pallas_call(
        functools.partial(_merge_body, alpha),
        grid=(n_t + 1,),
        in_specs=[
            pl.BlockSpec((tm, RET_V), inmap),
            pl.BlockSpec((tm, LRU_WIDTH), inmap),
            pl.BlockSpec((tm, 2 * D_MODEL), inmap),
            pl.BlockSpec((tm, D_MODEL), inmap),
            pl.BlockSpec((RET_V, D_MODEL), full),
            pl.BlockSpec((LRU_WIDTH, D_MODEL), full),
            pl.BlockSpec((D_MODEL, D_MODEL), full),
            pl.BlockSpec((1, D_MODEL), full),
            pl.BlockSpec((1, D_MODEL), full),
            pl.BlockSpec((D_MODEL, 2 * LANES), full),
            pl.BlockSpec((1, LANES), full),
        ],
        out_specs=[
            pl.BlockSpec((tm, D_MODEL), rowmap),
            pl.BlockSpec((tm, LANES), rowmap),
            pl.BlockSpec((tm, LANES), rowmap),
            pl.BlockSpec((SUBLANES, tm), lambda i: (0, jnp.maximum(i - 1, 0))),
            pl.BlockSpec((None, SUBLANES, LANES), lambda i: (jnp.maximum(i - 1, 0), 0, 0)),
        ],
        out_shape=[
            jax.ShapeDtypeStruct((n, D_MODEL), F32),
            jax.ShapeDtypeStruct((n, LANES), F32),
            jax.ShapeDtypeStruct((n, LANES), F32),
            jax.ShapeDtypeStruct((SUBLANES, n), F32),
            jax.ShapeDtypeStruct((n_t, SUBLANES, LANES), F32),
        ],
        scratch_shapes=[pltpu.VMEM((tm, D_MODEL), F32), pltpu.VMEM((tm, D_MODEL), BF16)],
        compiler_params=_cparams(("arbitrary",)),
        name="merge_ln_route",
    )(ret, lru, gm, x2, w_ret_o, w_lru_o, w_out, ln_g.reshape(1, -1), ln_b.reshape(1, -1), w_rt, b_rt)


def _run_lists(units, run_off, gdst):
    k = jnp.arange(N_EXPERTS, dtype=I32)
    cnts, offs, dsts = [], [], []
    for b in range(RUN_BITS):
        bit = (units >> b) & 1
        low = (units & ((1 << b) - 1)) * SEG_ALIGN
        pos = jnp.cumsum(bit, axis=1) - bit
        hit = (bit[:, None, :] == 1) & (pos[:, None, :] == k[None, :, None])
        offs.append(jnp.sum(jnp.where(hit, (run_off + low)[:, None, :], 0), axis=2))
        dsts.append(jnp.sum(jnp.where(hit, (gdst + low)[:, None, :], 0), axis=2))
        cnts.append(jnp.sum(bit, axis=1))
    flat = lambda parts: jnp.stack(parts, axis=1).reshape(-1).astype(I32)
    return flat(cnts), flat(offs), flat(dsts)


def _run_copies(lists, tile, loc_ref, glob_hbm, sem, to_global, wait):
    cnt_ref, off_ref, dst_ref = lists
    for b in range(RUN_BITS):
        rows = SEG_ALIGN << b
        base = tile * RUN_BITS + b

        def piece(k, carry):
            off = pl.multiple_of(off_ref[base * N_EXPERTS + k], SEG_ALIGN)
            dst = pl.multiple_of(dst_ref[base * N_EXPERTS + k], SEG_ALIGN)
            l = loc_ref.at[pl.ds(off, rows), :]
            g = glob_hbm.at[pl.ds(dst, rows), :]
            cp = pltpu.make_async_copy(l, g, sem) if to_global else pltpu.make_async_copy(g, l, sem)
            if wait:
                cp.wait()
            else:
                cp.start()
            return carry

        lax.fori_loop(0, cnt_ref[base], piece, 0)


def _zero_rows(start, units, max_units, zero_ref, xs_hbm, sem, wait):
    pos = start
    for b in range((max_units - 1).bit_length()):
        rows = SEG_ALIGN << b
        bit = lax.bitwise_and(lax.shift_right_logical(units, b), 1)

        @pl.when(bit == 1)
        def _():
            dst = xs_hbm.at[pl.ds(pl.multiple_of(pos, SEG_ALIGN), rows), :]
            cp = pltpu.make_async_copy(zero_ref.at[pl.ds(0, rows), :], dst, sem)
            if wait:
                cp.wait()
            else:
                cp.start()

        pos = pos + bit * rows


def _onehot_rows(lpr_ref, n_rows):
    tm = lpr_ref.shape[1]
    sub = lax.broadcasted_iota(I32, (n_rows, tm), 0)
    lp1 = lpr_ref[0:1, :].astype(I32)
    lp2 = lpr_ref[1:2, :].astype(I32)
    return jnp.where((sub == lp1) | (sub == lp2), 1.0, 0.0).astype(BF16)


def _dispatch_body(cnt_ref, off_ref, dst_ref, pad_start_ref, pad_units_ref, total_ref, x1_ref, lpr_ref,
                   xs_hbm, loc_ref, zero_ref, sems, zsem):
    i = pl.program_id(0)
    last = pl.num_programs(0) - 1
    slot = lax.rem(i, 2)
    lists = (cnt_ref, off_ref, dst_ref)
    perm = _onehot_rows(lpr_ref, loc_ref.shape[1])
    loc_ref[slot] = jnp.dot(perm, x1_ref[...].astype(BF16), preferred_element_type=F32)
    _run_copies(lists, i, loc_ref.at[slot], xs_hbm, sems.at[slot], True, False)

    @pl.when(i > 0)
    def _():
        _run_copies(lists, i - 1, loc_ref.at[1 - slot], xs_hbm, sems.at[1 - slot], True, True)

    def zero_fill(wait):
        def region(e, carry):
            _zero_rows(pad_start_ref[e], pad_units_ref[e], zero_ref.shape[0] // SEG_ALIGN, zero_ref, xs_hbm,
                       zsem, wait)
            return carry
        lax.fori_loop(0, N_EXPERTS, region, 0)
        _zero_tail(total_ref[0], zero_ref, xs_hbm, zsem, wait)

    @pl.when(i == 0)
    def _():
        zero_ref[...] = jnp.zeros_like(zero_ref)
        zero_fill(False)

    @pl.when(i == last)
    def _():
        zero_fill(True)
        _run_copies(lists, i, loc_ref.at[slot], xs_hbm, sems.at[slot], True, True)


def _zero_tail(total, zero_ref, xs_hbm, sem, wait):
    zr = zero_ref.shape[0]
    shift = zr.bit_length() - 1
    assert zr == 1 << shift and xs_hbm.shape[0] % SEG_ALIGN == 0
    dead = xs_hbm.shape[0] - total
    n_full = lax.shift_right_logical(dead, shift)

    def full(k, carry):
        dst = xs_hbm.at[pl.ds(pl.multiple_of(total + k * zr, SEG_ALIGN), zr), :]
        cp = pltpu.make_async_copy(zero_ref, dst, sem)
        if wait:
            cp.wait()
        else:
            cp.start()
        return carry

    lax.fori_loop(0, n_full, full, 0)
    rem = lax.shift_right_logical(dead - n_full * zr, SEG_ALIGN.bit_length() - 1)
    _zero_rows(total + n_full * zr, rem, zr // SEG_ALIGN, zero_ref, xs_hbm, sem, wait)


def _local_rows(tm):
    return TOP_K * tm + N_EXPERTS * SEG_ALIGN


def _dispatch(lists, pad_start, pad_units, total, x1, lpr, m_max):
    n = x1.shape[0]
    tm = min(ROUTE_TM, n)
    return pl.pallas_call(
        _dispatch_body,
        grid_spec=pltpu.PrefetchScalarGridSpec(
            num_scalar_prefetch=6,
            grid=(n // tm,),
            in_specs=[pl.BlockSpec((tm, D_MODEL), lambda i, *_: (i, 0)),
                      pl.BlockSpec((SUBLANES, tm), lambda i, *_: (0, i))],
            out_specs=pl.BlockSpec(memory_space=pl.ANY),
            scratch_shapes=[pltpu.VMEM((2, _local_rows(tm), D_MODEL), F32),
                            pltpu.VMEM((MOE_RT, D_MODEL), F32),
                            pltpu.SemaphoreType.DMA((2,)), pltpu.SemaphoreType.DMA],
        ),
        out_shape=jax.ShapeDtypeStruct((m_max, D_MODEL), F32),
        compiler_params=_cparams(("arbitrary",)),
        name="dispatch",
    )(*lists, pad_start, pad_units, total, x1, lpr)


def _expert_body(exp_ref, live_ref, xs_ref, wg_ref, wu_ref, wd_ref, y_ref, wgb_ref, wub_ref, wdb_ref, cur_ref):
    g = pl.program_id(0)
    e = exp_ref[g]

    @pl.when(g == 0)
    def _():
        cur_ref[0] = -1

    @pl.when(g < live_ref[0])
    def _():
        @pl.when(cur_ref[0] != e)
        def _():
            wgb_ref[...] = wg_ref[...].astype(BF16)
            wub_ref[...] = wu_ref[...].astype(BF16)
            wdb_ref[...] = wd_ref[...].astype(BF16)
            cur_ref[0] = e

        xb = xs_ref[...].astype(BF16)
        hg = jnp.dot(xb, wgb_ref[...], preferred_element_type=F32)
        hu = jnp.dot(xb, wub_ref[...], preferred_element_type=F32)
        hm = (hg * jax.nn.sigmoid(hg) * hu).astype(BF16)
        y_ref[...] = jnp.dot(hm, wdb_ref[...], preferred_element_type=F32)


def _experts(tile_expert, n_live, xs, w_gate, w_up, w_down):
    m = xs.shape[0]
    rt = min(MOE_RT, m)
    rows = lambda g, e, n: (jnp.minimum(g, n[0] - 1), 0)
    return pl.pallas_call(
        _expert_body,
        grid_spec=pltpu.PrefetchScalarGridSpec(
            num_scalar_prefetch=2,
            grid=(m // rt,),
            in_specs=[
                pl.BlockSpec((rt, D_MODEL), rows),
                pl.BlockSpec((None, D_MODEL, D_EXPERT), lambda g, e, n: (e[g], 0, 0)),
                pl.BlockSpec((None, D_MODEL, D_EXPERT), lambda g, e, n: (e[g], 0, 0)),
                pl.BlockSpec((None, D_EXPERT, D_MODEL), lambda g, e, n: (e[g], 0, 0)),
            ],
            out_specs=pl.BlockSpec((rt, D_MODEL), rows),
            scratch_shapes=[
                pltpu.VMEM((D_MODEL, D_EXPERT), BF16),
                pltpu.VMEM((D_MODEL, D_EXPERT), BF16),
                pltpu.VMEM((D_EXPERT, D_MODEL), BF16),
                pltpu.SMEM((1,), I32),
            ],
        ),
        out_shape=jax.ShapeDtypeStruct((m, D_MODEL), F32),
        input_output_aliases={2: 0},
        compiler_params=_cparams(("arbitrary",)),
        name="experts",
    )(tile_expert, n_live, xs, w_gate, w_up, w_down)


def _combine_body(alpha, cnt_ref, off_ref, dst_ref, x1_ref, rw_ref, lpc_ref, g2_ref, b2_ref, ys_hbm, o_ref,
                  loc_ref, sems):
    i = pl.program_id(0)
    slot = lax.rem(i, 2)
    lists = (cnt_ref, off_ref, dst_ref)

    @pl.when(i == 0)
    def _():
        loc_ref[...] = jnp.zeros_like(loc_ref)
        _run_copies(lists, i, loc_ref.at[slot], ys_hbm, sems.at[slot], False, False)

    @pl.when(i + 1 < pl.num_programs(0))
    def _():
        _run_copies(lists, i + 1, loc_ref.at[1 - slot], ys_hbm, sems.at[1 - slot], False, False)

    _run_copies(lists, i, loc_ref.at[slot], ys_hbm, sems.at[slot], False, True)
    tm = x1_ref.shape[0]
    n_rows = loc_ref.shape[1]
    yb = loc_ref[slot].astype(BF16)
    lanes = lax.broadcasted_iota(I32, (tm, n_rows), 1)
    sel = (jnp.where(lanes == lpc_ref[:, 0:1].astype(I32), rw_ref[:, 0:1], 0.0)
           + jnp.where(lanes == lpc_ref[:, 1:2].astype(I32), rw_ref[:, 1:2], 0.0))
    moe = jnp.dot(sel.astype(BF16), yb, preferred_element_type=F32)
    y = alpha * x1_ref[...] + moe
    o_ref[...] = _layer_norm_rows(y, g2_ref[...], b2_ref[...])


def _combine(lists, x1, rw, lpc, ln_g, ln_b, ys, alpha):
    n = x1.shape[0]
    tm = min(ROUTE_TM, n)
    rowmap = lambda i, *_: (i, 0)
    full = lambda i, *_: (0, 0)
    return pl.pallas_call(
        functools.partial(_combine_body, alpha),
        grid_spec=pltpu.PrefetchScalarGridSpec(
            num_scalar_prefetch=3,
            grid=(n // tm,),
            in_specs=[
                pl.BlockSpec((tm, D_MODEL), rowmap),
                pl.BlockSpec((tm, LANES), rowmap),
                pl.BlockSpec((tm, LANES), rowmap),
                pl.BlockSpec((1, D_MODEL), full),
                pl.BlockSpec((1, D_MODEL), full),
                pl.BlockSpec(memory_space=pl.ANY),
            ],
            out_specs=pl.BlockSpec((tm, D_MODEL), rowmap),
            scratch_shapes=[pltpu.VMEM((2, _local_rows(tm), D_MODEL), F32), pltpu.SemaphoreType.DMA((2,))],
        ),
        out_shape=jax.ShapeDtypeStruct((n, D_MODEL), F32),
        compiler_params=_cparams(("arbitrary",)),
        name="combine_ln",
    )(*lists, x1, rw, lpc, ln_g.reshape(1, -1), ln_b.reshape(1, -1), ys)


def _rotary_tables(seq):
    half = RET_DK // 2
    inv = ROPE_BASE ** (-jnp.arange(half, dtype=F32) / half)
    split = min(ROT_SPLIT, seq)
    a_hi = (jnp.arange(seq // split, dtype=F32) * split)[:, None] * inv[None, :]
    a_lo = jnp.arange(split, dtype=F32)[:, None] * inv[None, :]
    ch, sh, cl, sl = jnp.cos(a_hi)[:, None], jnp.sin(a_hi)[:, None], jnp.cos(a_lo)[None], jnp.sin(a_lo)[None]
    return (ch * cl - sh * sl).reshape(seq, half), (sh * cl + ch * sl).reshape(seq, half)


def _router_weights(w_group, b_group, w_exp_router, b_exp_router):
    spare = LANES - N_GROUPS - N_EXPERTS
    w = jnp.pad(jnp.concatenate([w_group, w_exp_router], axis=1), ((0, 0), (0, spare)))
    b = jnp.pad(jnp.concatenate([b_group, b_exp_router]), (0, spare)).reshape(1, LANES)
    w_hi = w.astype(BF16)
    w_lo = (w - w_hi.astype(F32)).astype(BF16)
    return jnp.concatenate([w_hi, w_lo], axis=1), b


def _layer(x, depth, w_in, b_merge, conv_w, conv_b, w_rg_r, b_rg_r, w_rg_i, b_rg_i, lru_lambda,
           w_ret_o, w_lru_o, w_out, ln1_g, ln1_b, w_group, b_group, w_exp_router, b_exp_router,
           w_e_gate, w_e_up, w_e_down, ln2_g, ln2_b):
    B, S, D = x.shape
    n = B * S
    alpha = (2.0 * depth) ** 0.25
    x2 = x.reshape(n, D)
    wb = w_in.astype(BF16)
    cos, sin = _rotary_tables(S)
    gm, lru, ret = _proj_lru(x2, wb, cos, sin, b_merge, conv_w, conv_b, w_rg_r, b_rg_r, w_rg_i, b_rg_i,
                             lru_lambda, S)


    w_rt, b_rt = _router_weights(w_group, b_group, w_exp_router, b_exp_router)
    x1, rw, lpc, lpr, tcnt = _merge(ret, lru, gm, x2, w_ret_o.astype(BF16), w_lru_o.astype(BF16),
                                    w_out.astype(BF16), ln1_g, ln1_b, w_rt, b_rt, alpha)

    tm = min(ROUTE_TM, n)
    n_t = n // tm
    m_max = n_t * _local_rows(tm) + N_EXPERTS * MOE_RT
    rt = min(MOE_RT, m_max)
    cnt = tcnt[:, 0, :N_EXPERTS].astype(I32)
    units = (cnt + (SEG_ALIGN - 1)) // SEG_ALIGN
    run = units * SEG_ALIGN
    sizes = jnp.sum(run, axis=0)
    region = (sizes + (rt - 1)) // rt * rt
    e_end = jnp.cumsum(region)
    e_start = e_end - region
    gdst = e_start[None, :] + jnp.cumsum(run, axis=0) - run
    run_off = jnp.cumsum(run, axis=1) - run
    lists = _run_lists(units, run_off, gdst)
    pad_start = (e_start + sizes).astype(I32)
    pad_units = ((region - sizes) // SEG_ALIGN).astype(I32)
    total = e_end[-1:].astype(I32)
    tile_start = jnp.arange(m_max // rt, dtype=I32) * rt
    n_live = total // rt
    tile_expert = jnp.sum((e_end[None, :] <= jnp.minimum(tile_start, total - rt)[:, None]).astype(I32), axis=1)

    xs = _dispatch(lists, pad_start, pad_units, total, x1, lpr, m_max)
    ys = _experts(tile_expert, n_live, xs, w_e_gate, w_e_up, w_e_down)
    out = _combine(lists, x1, rw, lpc, ln2_g, ln2_b, ys, alpha)
    return out.reshape(B, S, D)


def kernel(x, w_in, b_merge, conv_w, conv_b, w_rg_r, b_rg_r, w_rg_i, b_rg_i, lru_lambda, w_ret_o, w_lru_o, w_out, ln1_g, ln1_b, w_group, b_group, w_exp_router, b_exp_router, w_e_gate, w_e_up, w_e_down, ln2_g, ln2_b):
    depth = w_in.shape[0]
    for l in range(depth):
        x = _layer(x, depth, w_in[l], b_merge[l], conv_w[l], conv_b[l], w_rg_r[l], b_rg_r[l], w_rg_i[l],
                   b_rg_i[l], lru_lambda[l], w_ret_o[l], w_lru_o[l], w_out[l], ln1_g[l], ln1_b[l],
                   w_group[l], b_group[l], w_exp_router[l], b_exp_router[l], w_e_gate[l], w_e_up[l],
                   w_e_down[l], ln2_g[l], ln2_b[l])
    return x
```

```python
import functools

import jax
import jax.numpy as jnp
import numpy as np
from jax import lax
from jax.experimental import pallas as pl
from jax.experimental.pallas import tpu as pltpu

F32 = jnp.float32
BF16 = jnp.bfloat16
I32 = jnp.int32

D_MODEL = 1024
RET_HEADS = 4
RET_DK = 256
RET_DV = 512
RET_QK = RET_HEADS * RET_DK
RET_V = RET_HEADS * RET_DV
ROPE_BASE = 10000.0
LRU_WIDTH = 1536
LRU_BLOCKS = 8
LRU_BLOCK = LRU_WIDTH // LRU_BLOCKS
LRU_PAIR = 2 * LRU_BLOCK
LRU_PAIRS = LRU_BLOCKS // 2
CONV_WIDTH = 4
LRU_C = 8.0
N_GROUPS = 4
EXPERTS_PER_GROUP = 8
N_EXPERTS = N_GROUPS * EXPERTS_PER_GROUP
TOP_K = 2
D_EXPERT = 512
LN_EPS = 1e-5

LANES = 128
SUBLANES = 8
VMEM_LIMIT = 56 * 1024 * 1024

GATE_WIN = 2 * LANES
PROJ_TM = 256
ROT_SPLIT = 64
ROUTE_TM = 512
MERGE_CHUNKS = 4
MOE_RT = 512
SEG_ALIGN = SUBLANES
RUN_BITS = (ROUTE_TM // SEG_ALIGN).bit_length()


def _cparams(sem):
    return pltpu.CompilerParams(dimension_semantics=sem, vmem_limit_bytes=VMEM_LIMIT)


PROJ_WIDTHS = (2 * RET_QK, RET_V, RET_V, LRU_WIDTH, LRU_WIDTH, 2 * D_MODEL)


def _sigmoid(x):
    return 0.5 * jnp.tanh(0.5 * x) + 0.5


def _seg_pitch(t):
    g = t // SUBLANES
    units = -(-g // SUBLANES)
    return SUBLANES * (units + 1 - units % 2)


def _proj_lru_body(per_seq, dec_ref, x_ref, w_ref, cos_ref, sin_ref, bm_ref, cw_ref, cb_ref, wr_ref, wi_ref,
                   br_ref, bi_ref, lam_ref, dm_ref, xi_ref, zeta_ref, gm_ref, lru_ref, ret_ref,
                   ubuf_ref, gbuf_ref, a_ref, b_ref, carry_ref, st_ref):
    T = x_ref.shape[0]
    G = T // SUBLANES
    P = _seg_pitch(T)
    ncb = LRU_WIDTH // LANES
    o_qk, o_v, o_sg, o_u, o_gl, o_gm = (int(o) for o in np.cumsum((0,) + PROJ_WIDTHS[:-1]))
    xb = x_ref[...].astype(BF16)

    def seg(start, lo, hi):
        return jnp.dot(xb, w_ref[:, start + lo:start + hi], preferred_element_type=F32)

    @pl.when(lax.rem(pl.program_id(0), per_seq) == 0)
    def _():
        ubuf_ref[0:SUBLANES, :] = jnp.zeros((SUBLANES, LRU_WIDTH), F32)
        carry_ref[...] = jnp.zeros_like(carry_ref)
        st_ref[...] = jnp.zeros_like(st_ref)

    cw = cw_ref[...]
    neg_c_sp = -LRU_C * jax.nn.softplus(-lam_ref[...])
    half = RET_DK // 2
    cos, sin = cos_ref[...], sin_ref[...]

    def lru_inputs(lo, hi):
        ubuf_ref[SUBLANES:SUBLANES + T, lo:hi] = seg(o_u, lo, hi)
        gbuf_ref[:, lo:hi] = jax.nn.gelu(seg(o_gl, lo, hi))
        uc = cb_ref[:, lo:hi] + cw[CONV_WIDTH - 1:CONV_WIDTH, lo:hi] * ubuf_ref[SUBLANES:SUBLANES + T, lo:hi]
        for j in range(CONV_WIDTH - 1):
            back = CONV_WIDTH - 1 - j
            uc = uc + cw[j:j + 1, lo:hi] * ubuf_ref[SUBLANES - back:SUBLANES - back + T, lo:hi]
        ubuf_ref[0:SUBLANES, lo:hi] = ubuf_ref[T:T + SUBLANES, lo:hi]
        return uc

    def lru_pair(p, uc):
        lo = p * LRU_PAIR
        ucb = uc.astype(BF16)
        even = ucb[:, :GATE_WIN]
        odd = ucb[:, LRU_PAIR - GATE_WIN:]

        def gate_map(w_ref):
            e = jnp.dot(even, w_ref[2 * p], preferred_element_type=F32)
            o = jnp.dot(odd, w_ref[2 * p + 1], preferred_element_type=F32)
            return jnp.concatenate([e[:, :LANES], e[:, LANES:] + o[:, :LANES], o[:, LANES:]], axis=1)

        r = _sigmoid(gate_map(wr_ref) + br_ref[:, lo:lo + LRU_PAIR])
        i = _sigmoid(gate_map(wi_ref) + bi_ref[:, lo:lo + LRU_PAIR])
        log_a = r * neg_c_sp[:, lo:lo + LRU_PAIR]
        a = jnp.exp(log_a)
        inp = jnp.sqrt(-jnp.tanh(log_a) * (a * a + 1.0)) * (i * uc)
        for cc in range(LRU_PAIR // LANES):
            cb = p * (LRU_PAIR // LANES) + cc
            for s in range(SUBLANES):
                a_ref[cb, s * P:s * P + G, :] = a[s * G:(s + 1) * G, cc * LANES:(cc + 1) * LANES]
                b_ref[cb, s * P:s * P + G, :] = inp[s * G:(s + 1) * G, cc * LANES:(cc + 1) * LANES]

    def rotary(start, h, scale):
        acc = seg(start, h * RET_DK, (h + 1) * RET_DK)
        t1, t2 = acc[:, :half], acc[:, half:]
        return (jnp.concatenate([t1 * cos - t2 * sin, t1 * sin + t2 * cos], axis=1) * scale).astype(BF16)

    def retention_head(h):
        vc = slice(h * RET_DV, (h + 1) * RET_DV)
        q = rotary(o_qk, h, 1.0)
        k = rotary(o_qk + RET_QK, h, RET_DK ** -0.5)
        v = seg(o_v, h * RET_DV, (h + 1) * RET_DV).astype(BF16)
        g = seg(o_sg, h * RET_DV, (h + 1) * RET_DV)
        scores = lax.dot_general(q, k, (((1,), (1,)), ((), ())), preferred_element_type=F32) * dm_ref[h]
        inner = jnp.dot(scores.astype(BF16), v, preferred_element_type=F32)
        st = st_ref[h]
        cross = jnp.dot(q, st.astype(BF16), preferred_element_type=F32) * xi_ref[h]
        kz = (k.astype(F32) * zeta_ref[h]).astype(BF16)
        upd = lax.dot_general(kz, v, (((0,), (0,)), ((), ())), preferred_element_type=F32)
        st_ref[h] = st * dec_ref[h] + upd
        o = inner + cross
        mu = jnp.mean(o, axis=-1, keepdims=True)
        oc = o - mu
        var = jnp.mean(oc * oc, axis=-1, keepdims=True)
        ret_ref[:, vc] = (g * _sigmoid(g) * (oc * lax.rsqrt(var + LN_EPS))).astype(ret_ref.dtype)

    def merge_gates(lo, hi):
        gm_ref[:, lo:hi] = _sigmoid(seg(o_gm, lo, hi) + bm_ref[:, lo:hi]).astype(gm_ref.dtype)

    assert LRU_PAIRS == RET_HEADS and (2 * D_MODEL) % LRU_PAIRS == 0
    gm_chunk = 2 * D_MODEL // LRU_PAIRS
    for p in range(0, LRU_PAIRS, 2):
        uc2 = lru_inputs(p * LRU_PAIR, (p + 2) * LRU_PAIR)
        for q in range(2):
            merge_gates((p + q) * gm_chunk, (p + q + 1) * gm_chunk)
            lru_pair(p + q, uc2[:, q * LRU_PAIR:(q + 1) * LRU_PAIR])
            retention_head(p + q)

    def step(j, hp):
        hs, ps = hp
        nh, npr = [], []
        for cb in range(ncb):
            a = a_ref[cb, pl.ds(j, SUBLANES, stride=P), :]
            b = b_ref[cb, pl.ds(j, SUBLANES, stride=P), :]
            hn = a * hs[cb] + b
            pn = a * ps[cb]
            b_ref[cb, pl.ds(j, SUBLANES, stride=P), :] = hn
            a_ref[cb, pl.ds(j, SUBLANES, stride=P), :] = pn
            nh.append(hn)
            npr.append(pn)
        return tuple(nh), tuple(npr)

    zeros = tuple(jnp.zeros((SUBLANES, LANES), F32) for _ in range(ncb))
    ones = tuple(jnp.ones((SUBLANES, LANES), F32) for _ in range(ncb))
    h_end, p_end = lax.fori_loop(0, G, step, (zeros, ones), unroll=True)

    for cb in range(ncb):
        cin = carry_ref[:, cb * LANES:(cb + 1) * LANES]
        for s in range(SUBLANES):
            rows = slice(s * G, (s + 1) * G)
            hseg = b_ref[cb, s * P:s * P + G, :] + a_ref[cb, s * P:s * P + G, :] * cin
            gate = gbuf_ref[rows, cb * LANES:(cb + 1) * LANES]
            lru_ref[rows, cb * LANES:(cb + 1) * LANES] = (gate * hseg).astype(lru_ref.dtype)
            cin = h_end[cb][s:s + 1, :] + p_end[cb][s:s + 1, :] * cin
        carry_ref[:, cb * LANES:(cb + 1) * LANES] = cin


def _gate_windows(w):
    spare = GATE_WIN - LRU_BLOCK
    w4 = w.reshape(LRU_PAIRS, 2, LRU_BLOCK, LRU_BLOCK)
    first = jnp.pad(w4[:, 0], ((0, 0), (0, spare), (0, spare)))
    second = jnp.pad(w4[:, 1], ((0, 0), (spare, 0), (spare, 0)))
    return jnp.stack([first, second], axis=1).reshape(LRU_BLOCKS, GATE_WIN, GATE_WIN).astype(BF16)


def _retention_tables(chunk):
    H = RET_HEADS
    log_g = jnp.log1p(-(2.0 ** (-5.0 - jnp.arange(H, dtype=F32))))
    pos = jnp.arange(chunk, dtype=F32)
    diff = pos[:, None] - pos[None, :]
    causal = diff >= 0
    d_mask = jnp.where(causal[None], jnp.exp(log_g[:, None, None] * jnp.where(causal, diff, 0.0)[None]), 0.0)
    xi = jnp.exp(log_g[:, None] * (pos + 1.0)[None])[:, :, None]
    zeta = jnp.exp(log_g[:, None] * (chunk - 1.0 - pos)[None])[:, :, None]
    return d_mask, xi, zeta, jnp.exp(log_g * chunk)


def _proj_lru(x2, w, cos, sin, b_merge, conv_w, conv_b, w_r, b_r, w_i, b_i, lam, seq):
    n, d = x2.shape
    tm = min(PROJ_TM, seq)
    per_seq = seq // tm
    assert sum(PROJ_WIDTHS) == w.shape[1]
    ncb = LRU_WIDTH // LANES
    d_mask, xi, zeta, chunk_decay = _retention_tables(tm)
    rows = lambda width: pl.BlockSpec((tm, width), lambda i: (i, 0))
    full2 = lambda shape: pl.BlockSpec(shape, lambda i: (0, 0))
    full3 = lambda shape: pl.BlockSpec(shape, lambda i: (0, 0, 0))
    pairs = full3((LRU_BLOCKS, GATE_WIN, GATE_WIN))
    rot = pl.BlockSpec((tm, RET_DK // 2), lambda i: (i % per_seq, 0))
    vec = lambda a: a.reshape(1, -1)
    out_widths = (2 * D_MODEL, LRU_WIDTH, RET_V)
    return pl.pallas_call(
        functools.partial(_proj_lru_body, per_seq),
        grid=(n // tm,),
        in_specs=[pl.BlockSpec(memory_space=pltpu.SMEM),
                  rows(d),
                  pl.BlockSpec(w.shape, lambda i: (0, 0), pipeline_mode=pl.Buffered(1)),
                  rot, rot, full2((1, 2 * D_MODEL)),
                  full2((CONV_WIDTH, LRU_WIDTH)), full2((1, LRU_WIDTH)), pairs, pairs,
                  full2((1, LRU_WIDTH)), full2((1, LRU_WIDTH)), full2((1, LRU_WIDTH)),
                  full3((RET_HEADS, tm, tm)), full3((RET_HEADS, tm, 1)), full3((RET_HEADS, tm, 1))],
        out_specs=[rows(width) for width in out_widths],
        out_shape=[jax.ShapeDtypeStruct((n, width), BF16) for width in out_widths],
        scratch_shapes=[
            pltpu.VMEM((tm + SUBLANES, LRU_WIDTH), F32),
            pltpu.VMEM((tm, LRU_WIDTH), F32),
            pltpu.VMEM((ncb, SUBLANES * _seg_pitch(tm), LANES), F32),
            pltpu.VMEM((ncb, SUBLANES * _seg_pitch(tm), LANES), F32),
            pltpu.VMEM((1, LRU_WIDTH), F32),
            pltpu.VMEM((RET_HEADS, RET_DK, RET_DV), F32),
        ],
        compiler_params=_cparams(("arbitrary",)),
        name="proj_mixers",
    )(chunk_decay, x2, w, cos, sin, vec(b_merge), conv_w, vec(conv_b), _gate_windows(w_r),
      _gate_windows(w_i), vec(b_r), vec(b_i), vec(lam), d_mask, xi, zeta)


def _layer_norm_rows(y, g, b):
    mu = jnp.mean(y, axis=-1, keepdims=True)
    yc = y - mu
    var = jnp.mean(yc * yc, axis=-1, keepdims=True)
    return yc * lax.rsqrt(var + LN_EPS) * g + b


def _merge_body(alpha, ret_ref, lru_ref, gm_ref, x_ref, wro_ref, wlo_ref, wo_ref, g1_ref, b1_ref,
                wrt_ref, brt_ref, x1_ref, rw_ref, lpc_ref, lpr_ref, cnt_ref, y_ref, mb_ref):
    @pl.when(pl.program_id(0) == 0)
    def _():
        y_ref[...] = jnp.zeros_like(y_ref)

    x1 = _layer_norm_rows(y_ref[...], g1_ref[...], b1_ref[...])
    x1_ref[...] = x1

    assert MERGE_CHUNKS == 4
    chunk = D_MODEL // MERGE_CHUNKS

    def merged_chunk(c):
        cols = slice(c * chunk, (c + 1) * chunk)
        gcols = slice(D_MODEL + c * chunk, D_MODEL + (c + 1) * chunk)
        pr = jnp.dot(ret_ref[...], wro_ref[:, cols], preferred_element_type=F32)
        pu = jnp.dot(lru_ref[...], wlo_ref[:, cols], preferred_element_type=F32)
        mb_ref[:, cols] = (gm_ref[:, cols].astype(F32) * pr + gm_ref[:, gcols].astype(F32) * pu).astype(BF16)

    merged_chunk(0)
    tm = x1.shape[0]
    x_hi = x1.astype(BF16)
    x_lo = (x1 - x_hi.astype(F32)).astype(BF16)
    hh = jnp.dot(x_hi, wrt_ref[...], preferred_element_type=F32)
    lh = jnp.dot(x_lo, wrt_ref[:, :LANES], preferred_element_type=F32)
    lg = hh[:, :LANES] + hh[:, LANES:] + lh + brt_ref[...]
    merged_chunk(1)
    lane = lax.broadcasted_iota(I32, (tm, LANES), 1)
    big = jnp.int32(LANES)
    neg = jnp.float32(-jnp.inf)
    gmask = lane < N_GROUPS
    gl = jnp.where(gmask, lg, neg)
    gmax = jnp.max(gl, axis=-1, keepdims=True)
    g_idx = jnp.min(jnp.where(gmask & (gl == gmax), lane, big), axis=-1, keepdims=True)
    g_w = 1.0 / jnp.sum(jnp.where(gmask, jnp.exp(gl - gmax), 0.0), axis=-1, keepdims=True)
    e_lo = N_GROUPS + EXPERTS_PER_GROUP * g_idx
    emask = (lane >= e_lo) & (lane < e_lo + EXPERTS_PER_GROUP)
    el = jnp.where(emask, lg, neg)
    v1 = jnp.max(el, axis=-1, keepdims=True)
    i1 = jnp.min(jnp.where(emask & (el == v1), lane, big), axis=-1, keepdims=True)
    emask2 = emask & (lane != i1)
    el2 = jnp.where(emask2, lg, neg)
    v2 = jnp.max(el2, axis=-1, keepdims=True)
    i2 = jnp.min(jnp.where(emask2 & (el2 == v2), lane, big), axis=-1, keepdims=True)
    ex = jnp.exp(v2 - v1)
    den = 1.0 + ex
    w1 = g_w / den
    w2 = g_w * ex / den
    e1 = i1 - N_GROUPS
    e2 = i2 - N_GROUPS
    rw_ref[...] = jnp.where(lane == 0, w1, jnp.where(lane == 1, w2, 0.0))
    merged_chunk(2)

    oh = (lane == e1).astype(F32) + (lane == e2).astype(F32)
    rowi = lax.broadcasted_iota(I32, (tm, tm), 0)
    coli = lax.broadcasted_iota(I32, (tm, tm), 1)
    tri = jnp.where(coli < rowi, 1.0, 0.0).astype(BF16)
    before = jnp.dot(tri, oh.astype(BF16), preferred_element_type=F32)
    cnt = jnp.sum(oh, axis=0, keepdims=True)
    units = jnp.floor((cnt + (SEG_ALIGN - 1.0)) * (1.0 / SEG_ALIGN))
    er = lax.broadcasted_iota(I32, (LANES, LANES), 0)
    ec = lax.broadcasted_iota(I32, (LANES, LANES), 1)
    upper = jnp.where(er < ec, 1.0, 0.0).astype(BF16)
    offs = SEG_ALIGN * jnp.dot(jnp.broadcast_to(units, (SUBLANES, LANES)).astype(BF16), upper,
                               preferred_element_type=F32)[0:1, :]
    pos = before + offs
    lp1 = jnp.sum(jnp.where(lane == e1, pos, 0.0), axis=-1, keepdims=True)
    lp2 = jnp.sum(jnp.where(lane == e2, pos, 0.0), axis=-1, keepdims=True)
    lpc = jnp.where(lane == 0, lp1, jnp.where(lane == 1, lp2, 0.0))
    lpc_ref[...] = lpc
    lpr_ref[...] = lpc.T[0:SUBLANES, :]
    cnt_ref[...] = jnp.broadcast_to(cnt, cnt_ref.shape)

    merged_chunk(3)

    for c in range(MERGE_CHUNKS):
        cols = slice(c * chunk, (c + 1) * chunk)
        y_ref[:, cols] = alpha * x_ref[:, cols] + jnp.dot(mb_ref[...], wo_ref[:, cols], preferred_element_type=F32)


def _merge(ret, lru, gm, x2, w_ret_o, w_lru_o, w_out, ln_g, ln_b, w_rt, b_rt, alpha):
    n = x2.shape[0]
    tm = min(ROUTE_TM, n)
    n_t = n // tm
    inmap = lambda i: (jnp.minimum(i, n_t - 1), 0)
    rowmap = lambda i: (jnp.maximum(i - 1, 0), 0)
    full = lambda i: (0, 0)
    return pl.pallas_call(
        functools.partial(_merge_body, alpha),
        grid=(n_t + 1,),
        in_specs=[
            pl.BlockSpec((tm, RET_V), inmap),
            pl.BlockSpec((tm, LRU_WIDTH), inmap),
            pl.BlockSpec((tm, 2 * D_MODEL), inmap),
            pl.BlockSpec((tm, D_MODEL), inmap),
            pl.BlockSpec((RET_V, D_MODEL), full),
            pl.BlockSpec((LRU_WIDTH, D_MODEL), full),
            pl.BlockSpec((D_MODEL, D_MODEL), full),
            pl.BlockSpec((1, D_MODEL), full),
            pl.BlockSpec((1, D_MODEL), full),
            pl.BlockSpec((D_MODEL, 2 * LANES), full),
            pl.BlockSpec((1, LANES), full),
        ],
        out_specs=[
            pl.BlockSpec((tm, D_MODEL), rowmap),
            pl.BlockSpec((tm, LANES), rowmap),
            pl.BlockSpec((tm, LANES), rowmap),
            pl.BlockSpec((SUBLANES, tm), lambda i: (0, jnp.maximum(i - 1, 0))),
            pl.BlockSpec((None, SUBLANES, LANES), lambda i: (jnp.maximum(i - 1, 0), 0, 0)),
        ],
        out_shape=[
            jax.ShapeDtypeStruct((n, D_MODEL), F32),
            jax.ShapeDtypeStruct((n, LANES), F32),
            jax.ShapeDtypeStruct((n, LANES), F32),
            jax.ShapeDtypeStruct((SUBLANES, n), F32),
            jax.ShapeDtypeStruct((n_t, SUBLANES, LANES), F32),
        ],
        scratch_shapes=[pltpu.VMEM((tm, D_MODEL), F32), pltpu.VMEM((tm, D_MODEL), BF16)],
        compiler_params=_cparams(("arbitrary",)),
        name="merge_ln_route",
    )(ret, lru, gm, x2, w_ret_o, w_lru_o, w_out, ln_g.reshape(1, -1), ln_b.reshape(1, -1), w_rt, b_rt)


def _run_lists(units, run_off, gdst):
    k = jnp.arange(N_EXPERTS, dtype=I32)
    cnts, offs, dsts = [], [], []
    for b in range(RUN_BITS):
        bit = (units >> b) & 1
        low = (units & ((1 << b) - 1)) * SEG_ALIGN
        pos = jnp.cumsum(bit, axis=1) - bit
        hit = (bit[:, None, :] == 1) & (pos[:, None, :] == k[None, :, None])
        offs.append(jnp.sum(jnp.where(hit, (run_off + low)[:, None, :], 0), axis=2))
        dsts.append(jnp.sum(jnp.where(hit, (gdst + low)[:, None, :], 0), axis=2))
        cnts.append(jnp.sum(bit, axis=1))
    flat = lambda parts: jnp.stack(parts, axis=1).reshape(-1).astype(I32)
    return flat(cnts), flat(offs), flat(dsts)


def _run_copies(lists, tile, loc_ref, glob_hbm, sem, to_global, wait):
    cnt_ref, off_ref, dst_ref = lists
    for b in range(RUN_BITS):
        rows = SEG_ALIGN << b
        base = tile * RUN_BITS + b

        def piece(k, carry):
            off = pl.multiple_of(off_ref[base * N_EXPERTS + k], SEG_ALIGN)
            dst = pl.multiple_of(dst_ref[base * N_EXPERTS + k], SEG_ALIGN)
            l = loc_ref.at[pl.ds(off, rows), :]
            g = glob_hbm.at[pl.ds(dst, rows), :]
            cp = pltpu.make_async_copy(l, g, sem) if to_global else pltpu.make_async_copy(g, l, sem)
            if wait:
                cp.wait()
            else:
                cp.start()
            return carry

        lax.fori_loop(0, cnt_ref[base], piece, 0)


def _zero_rows(start, units, max_units, zero_ref, xs_hbm, sem, wait):
    pos = start
    for b in range((max_units - 1).bit_length()):
        rows = SEG_ALIGN << b
        bit = lax.bitwise_and(lax.shift_right_logical(units, b), 1)

        @pl.when(bit == 1)
        def _():
            dst = xs_hbm.at[pl.ds(pl.multiple_of(pos, SEG_ALIGN), rows), :]
            cp = pltpu.make_async_copy(zero_ref.at[pl.ds(0, rows), :], dst, sem)
            if wait:
                cp.wait()
            else:
                cp.start()

        pos = pos + bit * rows


def _onehot_rows(lpr_ref, n_rows):
    tm = lpr_ref.shape[1]
    sub = lax.broadcasted_iota(I32, (n_rows, tm), 0)
    lp1 = lpr_ref[0:1, :].astype(I32)
    lp2 = lpr_ref[1:2, :].astype(I32)
    return jnp.where((sub == lp1) | (sub == lp2), 1.0, 0.0).astype(BF16)


def _dispatch_body(cnt_ref, off_ref, dst_ref, pad_start_ref, pad_units_ref, total_ref, x1_ref, lpr_ref,
                   xs_hbm, loc_ref, zero_ref, sems, zsem):
    i = pl.program_id(0)
    last = pl.num_programs(0) - 1
    slot = lax.rem(i, 2)
    lists = (cnt_ref, off_ref, dst_ref)
    perm = _onehot_rows(lpr_ref, loc_ref.shape[1])
    loc_ref[slot] = jnp.dot(perm, x1_ref[...].astype(BF16), preferred_element_type=F32)
    _run_copies(lists, i, loc_ref.at[slot], xs_hbm, sems.at[slot], True, False)

    @pl.when(i > 0)
    def _():
        _run_copies(lists, i - 1, loc_ref.at[1 - slot], xs_hbm, sems.at[1 - slot], True, True)

    def zero_fill(wait):
        def region(e, carry):
            _zero_rows(pad_start_ref[e], pad_units_ref[e], zero_ref.shape[0] // SEG_ALIGN, zero_ref, xs_hbm,
                       zsem, wait)
            return carry
        lax.fori_loop(0, N_EXPERTS, region, 0)
        _zero_tail(total_ref[0], zero_ref, xs_hbm, zsem, wait)

    @pl.when(i == 0)
    def _():
        zero_ref[...] = jnp.zeros_like(zero_ref)
        zero_fill(False)

    @pl.when(i == last)
    def _():
        zero_fill(True)
        _run_copies(lists, i, loc_ref.at[slot], xs_hbm, sems.at[slot], True, True)


def _zero_tail(total, zero_ref, xs_hbm, sem, wait):
    zr = zero_ref.shape[0]
    shift = zr.bit_length() - 1
    assert zr == 1 << shift and xs_hbm.shape[0] % SEG_ALIGN == 0
    dead = xs_hbm.shape[0] - total
    n_full = lax.shift_right_logical(dead, shift)

    def full(k, carry):
        dst = xs_hbm.at[pl.ds(pl.multiple_of(total + k * zr, SEG_ALIGN), zr), :]
        cp = pltpu.make_async_copy(zero_ref, dst, sem)
        if wait:
            cp.wait()
        else:
            cp.start()
        return carry

    lax.fori_loop(0, n_full, full, 0)
    rem = lax.shift_right_logical(dead - n_full * zr, SEG_ALIGN.bit_length() - 1)
    _zero_rows(total + n_full * zr, rem, zr // SEG_ALIGN, zero_ref, xs_hbm, sem, wait)


def _local_rows(tm):
    return TOP_K * tm + N_EXPERTS * SEG_ALIGN


def _dispatch(lists, pad_start, pad_units, total, x1, lpr, m_max):
    n = x1.shape[0]
    tm = min(ROUTE_TM, n)
    return pl.pallas_call(
        _dispatch_body,
        grid_spec=pltpu.PrefetchScalarGridSpec(
            num_scalar_prefetch=6,
            grid=(n // tm,),
            in_specs=[pl.BlockSpec((tm, D_MODEL), lambda i, *_: (i, 0)),
                      pl.BlockSpec((SUBLANES, tm), lambda i, *_: (0, i))],
            out_specs=pl.BlockSpec(memory_space=pl.ANY),
            scratch_shapes=[pltpu.VMEM((2, _local_rows(tm), D_MODEL), F32),
                            pltpu.VMEM((MOE_RT, D_MODEL), F32),
                            pltpu.SemaphoreType.DMA((2,)), pltpu.SemaphoreType.DMA],
        ),
        out_shape=jax.ShapeDtypeStruct((m_max, D_MODEL), F32),
        compiler_params=_cparams(("arbitrary",)),
        name="dispatch",
    )(*lists, pad_start, pad_units, total, x1, lpr)


def _expert_body(exp_ref, live_ref, xs_ref, wg_ref, wu_ref, wd_ref, y_ref, wgb_ref, wub_ref, wdb_ref, cur_ref):
    g = pl.program_id(0)
    e = exp_ref[g]

    @pl.when(g == 0)
    def _():
        cur_ref[0] = -1

    @pl.when(g < live_ref[0])
    def _():
        @pl.when(cur_ref[0] != e)
        def _():
            wgb_ref[...] = wg_ref[...].astype(BF16)
            wub_ref[...] = wu_ref[...].astype(BF16)
            wdb_ref[...] = wd_ref[...].astype(BF16)
            cur_ref[0] = e

        xb = xs_ref[...].astype(BF16)
        hg = jnp.dot(xb, wgb_ref[...], preferred_element_type=F32)
        hu = jnp.dot(xb, wub_ref[...], preferred_element_type=F32)
        hm = (hg * jax.nn.sigmoid(hg) * hu).astype(BF16)
        y_ref[...] = jnp.dot(hm, wdb_ref[...], preferred_element_type=F32)


def _experts(tile_expert, n_live, xs, w_gate, w_up, w_down):
    m = xs.shape[0]
    rt = min(MOE_RT, m)
    rows = lambda g, e, n: (jnp.minimum(g, n[0] - 1), 0)
    return pl.pallas_call(
        _expert_body,
        grid_spec=pltpu.PrefetchScalarGridSpec(
            num_scalar_prefetch=2,
            grid=(m // rt,),
            in_specs=[
                pl.BlockSpec((rt, D_MODEL), rows),
                pl.BlockSpec((None, D_MODEL, D_EXPERT), lambda g, e, n: (e[g], 0, 0)),
                pl.BlockSpec((None, D_MODEL, D_EXPERT), lambda g, e, n: (e[g], 0, 0)),
                pl.BlockSpec((None, D_EXPERT, D_MODEL), lambda g, e, n: (e[g], 0, 0)),
            ],
            out_specs=pl.BlockSpec((rt, D_MODEL), rows),
            scratch_shapes=[
                pltpu.VMEM((D_MODEL, D_EXPERT), BF16),
                pltpu.VMEM((D_MODEL, D_EXPERT), BF16),
                pltpu.VMEM((D_EXPERT, D_MODEL), BF16),
                pltpu.SMEM((1,), I32),
            ],
        ),
        out_shape=jax.ShapeDtypeStruct((m, D_MODEL), F32),
        input_output_aliases={2: 0},
        compiler_params=_cparams(("arbitrary",)),
        name="experts",
    )(tile_expert, n_live, xs, w_gate, w_up, w_down)


def _combine_body(alpha, cnt_ref, off_ref, dst_ref, x1_ref, rw_ref, lpc_ref, g2_ref, b2_ref, ys_hbm, o_ref,
                  loc_ref, sems):
    i = pl.program_id(0)
    slot = lax.rem(i, 2)
    lists = (cnt_ref, off_ref, dst_ref)

    @pl.when(i == 0)
    def _():
        loc_ref[...] = jnp.zeros_like(loc_ref)
        _run_copies(lists, i, loc_ref.at[slot], ys_hbm, sems.at[slot], False, False)

    @pl.when(i + 1 < pl.num_programs(0))
    def _():
        _run_copies(lists, i + 1, loc_ref.at[1 - slot], ys_hbm, sems.at[1 - slot], False, False)

    _run_copies(lists, i, loc_ref.at[slot], ys_hbm, sems.at[slot], False, True)
    tm = x1_ref.shape[0]
    n_rows = loc_ref.shape[1]
    yb = loc_ref[slot].astype(BF16)
    lanes = lax.broadcasted_iota(I32, (tm, n_rows), 1)
    sel = (jnp.where(lanes == lpc_ref[:, 0:1].astype(I32), rw_ref[:, 0:1], 0.0)
           + jnp.where(lanes == lpc_ref[:, 1:2].astype(I32), rw_ref[:, 1:2], 0.0))
    moe = jnp.dot(sel.astype(BF16), yb, preferred_element_type=F32)
    y = alpha * x1_ref[...] + moe
    o_ref[...] = _layer_norm_rows(y, g2_ref[...], b2_ref[...])


def _combine(lists, x1, rw, lpc, ln_g, ln_b, ys, alpha):
    n = x1.shape[0]
    tm = min(ROUTE_TM, n)
    rowmap = lambda i, *_: (i, 0)
    full = lambda i, *_: (0, 0)
    return pl.pallas_call(
        functools.partial(_combine_body, alpha),
        grid_spec=pltpu.PrefetchScalarGridSpec(
            num_scalar_prefetch=3,
            grid=(n // tm,),
            in_specs=[
                pl.BlockSpec((tm, D_MODEL), rowmap),
                pl.BlockSpec((tm, LANES), rowmap),
                pl.BlockSpec((tm, LANES), rowmap),
                pl.BlockSpec((1, D_MODEL), full),
                pl.BlockSpec((1, D_MODEL), full),
                pl.BlockSpec(memory_space=pl.ANY),
            ],
            out_specs=pl.BlockSpec((tm, D_MODEL), rowmap),
            scratch_shapes=[pltpu.VMEM((2, _local_rows(tm), D_MODEL), F32), pltpu.SemaphoreType.DMA((2,))],
        ),
        out_shape=jax.ShapeDtypeStruct((n, D_MODEL), F32),
        compiler_params=_cparams(("arbitrary",)),
        name="combine_ln",
    )(*lists, x1, rw, lpc, ln_g.reshape(1, -1), ln_b.reshape(1, -1), ys)


def _rotary_tables(seq):
    half = RET_DK // 2
    inv = ROPE_BASE ** (-jnp.arange(half, dtype=F32) / half)
    split = min(ROT_SPLIT, seq)
    a_hi = (jnp.arange(seq // split, dtype=F32) * split)[:, None] * inv[None, :]
    a_lo = jnp.arange(split, dtype=F32)[:, None] * inv[None, :]
    ch, sh, cl, sl = jnp.cos(a_hi)[:, None], jnp.sin(a_hi)[:, None], jnp.cos(a_lo)[None], jnp.sin(a_lo)[None]
    return (ch * cl - sh * sl).reshape(seq, half), (sh * cl + ch * sl).reshape(seq, half)


def _router_weights(w_group, b_group, w_exp_router, b_exp_router):
    spare = LANES - N_GROUPS - N_EXPERTS
    w = jnp.pad(jnp.concatenate([w_group, w_exp_router], axis=1), ((0, 0), (0, spare)))
    b = jnp.pad(jnp.concatenate([b_group, b_exp_router]), (0, spare)).reshape(1, LANES)
    w_hi = w.astype(BF16)
    w_lo = (w - w_hi.astype(F32)).astype(BF16)
    return jnp.concatenate([w_hi, w_lo], axis=1), b


def _layer(x, depth, w_in, b_merge, conv_w, conv_b, w_rg_r, b_rg_r, w_rg_i, b_rg_i, lru_lambda,
           w_ret_o, w_lru_o, w_out, ln1_g, ln1_b, w_group, b_group, w_exp_router, b_exp_router,
           w_e_gate, w_e_up, w_e_down, ln2_g, ln2_b):
    B, S, D = x.shape
    n = B * S
    alpha = (2.0 * depth) ** 0.25
    x2 = x.reshape(n, D)
    wb = w_in.astype(BF16)
    cos, sin = _rotary_tables(S)
    gm, lru, ret = _proj_lru(x2, wb, cos, sin, b_merge, conv_w, conv_b, w_rg_r, b_rg_r, w_rg_i, b_rg_i,
                             lru_lambda, S)


    w_rt, b_rt = _router_weights(w_group, b_group, w_exp_router, b_exp_router)
    x1, rw, lpc, lpr, tcnt = _merge(ret, lru, gm, x2, w_ret_o.astype(BF16), w_lru_o.astype(BF16),
                                    w_out.astype(BF16), ln1_g, ln1_b, w_rt, b_rt, alpha)

    tm = min(ROUTE_TM, n)
    n_t = n // tm
    m_max = n_t * _local_rows(tm) + N_EXPERTS * MOE_RT
    rt = min(MOE_RT, m_max)
    cnt = tcnt[:, 0, :N_EXPERTS].astype(I32)
    units = (cnt + (SEG_ALIGN - 1)) // SEG_ALIGN
    run = units * SEG_ALIGN
    sizes = jnp.sum(run, axis=0)
    region = (sizes + (rt - 1)) // rt * rt
    e_end = jnp.cumsum(region)
    e_start = e_end - region
    gdst = e_start[None, :] + jnp.cumsum(run, axis=0) - run
    run_off = jnp.cumsum(run, axis=1) - run
    lists = _run_lists(units, run_off, gdst)
    pad_start = (e_start + sizes).astype(I32)
    pad_units = ((region - sizes) // SEG_ALIGN).astype(I32)
    total = e_end[-1:].astype(I32)
    tile_start = jnp.arange(m_max // rt, dtype=I32) * rt
    n_live = total // rt
    tile_expert = jnp.sum((e_end[None, :] <= jnp.minimum(tile_start, total - rt)[:, None]).astype(I32), axis=1)

    xs = _dispatch(lists, pad_start, pad_units, total, x1, lpr, m_max)
    ys = _experts(tile_expert, n_live, xs, w_e_gate, w_e_up, w_e_down)
    out = _combine(lists, x1, rw, lpc, ln2_g, ln2_b, ys, alpha)
    return out.reshape(B, S, D)


def kernel(x, w_in, b_merge, conv_w, conv_b, w_rg_r, b_rg_r, w_rg_i, b_rg_i, lru_lambda, w_ret_o, w_lru_o, w_out, ln1_g, ln1_b, w_group, b_group, w_exp_router, b_exp_router, w_e_gate, w_e_up, w_e_down, ln2_g, ln2_b):
    depth = w_in.shape[0]
    for l in range(depth):
        x = _layer(x, depth, w_in[l], b_merge[l], conv_w[l], conv_b[l], w_rg_r[l], b_rg_r[l], w_rg_i[l],
                   b_rg_i[l], lru_lambda[l], w_ret_o[l], w_lru_o[l], w_out[l], ln1_g[l], ln1_b[l],
                   w_group[l], b_group[l], w_exp_router[l], b_exp_router[l], w_e_gate[l], w_e_up[l],
                   w_e_down[l], ln2_g[l], ln2_b[l])
    return x
```
